```python
import math
import jax, jax.numpy as jnp
from jax import lax
import numpy as np

D_MODEL = 1024
BATCH = 8
SEQ = 2048
DEPTH = 2
DEC_BATCH = 128
DEC_SEQ = 4
PAST_LEN = 16384
PAGE_SIZE = 128

D_LRU = D_MODEL // 2
LRU_HEADS = 8
LRU_HEAD_DIM = D_LRU // LRU_HEADS
CONV_W = 4
C_GATE = 8.0
D_S5 = D_MODEL - D_LRU
S5_GROUP = 16
N_S5_GROUPS = D_S5 // S5_GROUP
S5_STATE = 64
D_IN = 2 * D_LRU + D_S5
D_FF = 2816
FFN_RES = 0.5
N_MOD = 9
EPS = 1e-6

kernel_name = "hymba_rglru_s5_macaron_adaln_step"


def rmsnorm(x, g):
    xf = x.astype(jnp.float32)
    y = xf * lax.rsqrt(jnp.mean(xf * xf, axis=-1, keepdims=True) + EPS)
    return (y * g.astype(jnp.float32)).astype(x.dtype)


def modulate(h, shift, scale):
    return h * (1.0 + scale[:, None, :]) + shift[:, None, :]


def swiglu(h, w1, w3, w2):
    return (jax.nn.silu(h @ w1) * (h @ w3)) @ w2


def _lin_op(e1, e2):
    a1, b1 = e1
    a2, b2 = e2
    return a2 * a1, a2 * b1 + b2


def _cplx_op(e1, e2):
    a1r, a1i, b1r, b1i = e1
    a2r, a2i, b2r, b2i = e2
    return (a2r * a1r - a2i * a1i,
            a2r * a1i + a2i * a1r,
            a2r * b1r - a2i * b1i + b2r,
            a2r * b1i + a2i * b1r + b2i)


def lru_scan(a, b, h0):
    b = b.at[:, 0].add(a[:, 0] * h0)
    _, h = lax.associative_scan(_lin_op, (a, b), axis=1)
    return h


def s5_scan(ar, ai, br, bi, s0r, s0i):
    br = br.at[:, 0].add(ar * s0r - ai * s0i)
    bi = bi.at[:, 0].add(ar * s0i + ai * s0r)
    Ar = jnp.broadcast_to(ar, br.shape)
    Ai = jnp.broadcast_to(ai, bi.shape)
    _, _, sr, si = lax.associative_scan(_cplx_op, (Ar, Ai, br, bi), axis=1)
    return sr, si


def mixer(h, conv0, h0, sr0, si0, w_in, conv_w, conv_b, w_rg, b_rg, w_ig, b_ig, lam,
          a_re, a_im, log_dt, b_re, b_im, c_re, c_im, d_skip, w_glu, b_glu, w_out):
    B, T, _ = h.shape
    z = h @ w_in
    xb = z[..., :D_LRU]
    yb = z[..., D_LRU:2 * D_LRU]
    u = z[..., 2 * D_LRU:]

    xp = jnp.concatenate([conv0.astype(xb.dtype), xb], axis=1)
    new_conv = xp[:, -(CONV_W - 1):]
    xc = conv_b + sum(xp[:, k:k + T] * conv_w[k] for k in range(CONV_W))
    xc32 = xc.astype(jnp.float32)
    xh = xc32.reshape(B, T, LRU_HEADS, LRU_HEAD_DIM)
    r = jax.nn.sigmoid(jnp.einsum('bthi,hij->bthj', xh, w_rg.astype(jnp.float32)).reshape(B, T, D_LRU)
                       + b_rg.astype(jnp.float32))
    ig = jax.nn.sigmoid(jnp.einsum('bthi,hij->bthj', xh, w_ig.astype(jnp.float32)).reshape(B, T, D_LRU)
                        + b_ig.astype(jnp.float32))
    log_a = -C_GATE * r * jax.nn.softplus(-lam.astype(jnp.float32))
    a = jnp.exp(log_a)
    mult = jnp.sqrt(-jnp.expm1(2.0 * log_a))
    hs = lru_scan(a, mult * ig * xc32, h0.astype(jnp.float32))
    new_h = hs[:, -1]
    y_lru = jax.nn.gelu(yb) * hs.astype(yb.dtype)

    Ar = jnp.minimum(a_re.astype(jnp.float32), -1e-4)
    Ai = a_im.astype(jnp.float32)
    dt = jnp.exp(log_dt.astype(jnp.float32))[:, None]
    mag = jnp.exp(Ar * dt)
    ab_r = mag * jnp.cos(Ai * dt)
    ab_i = mag * jnp.sin(Ai * dt)
    den = Ar * Ar + Ai * Ai
    f_r = ((ab_r - 1.0) * Ar + ab_i * Ai) / den
    f_i = (ab_i * Ar - (ab_r - 1.0) * Ai) / den
    Br = b_re.astype(jnp.float32)
    Bi = b_im.astype(jnp.float32)
    bb_r = f_r[..., None] * Br - f_i[..., None] * Bi
    bb_i = f_r[..., None] * Bi + f_i[..., None] * Br
    ug = u.astype(jnp.float32).reshape(B, T, N_S5_GROUPS, S5_GROUP)
    bu_r = jnp.einsum('btgj,gnj->btgn', ug, bb_r)
    bu_i = jnp.einsum('btgj,gnj->btgn', ug, bb_i)
    sr, si = s5_scan(ab_r, ab_i, bu_r, bu_i, sr0.astype(jnp.float32), si0.astype(jnp.float32))
    ys = (jnp.einsum('gjn,btgn->btgj', c_re.astype(jnp.float32), sr)
          - jnp.einsum('gjn,btgn->btgj', c_im.astype(jnp.float32), si))
    ys = ys.reshape(B, T, D_S5).astype(u.dtype) + d_skip * u
    g = jax.nn.gelu(ys)
    y_s5 = g * jax.nn.sigmoid(g @ w_glu + b_glu)

    out = jnp.concatenate([y_lru, y_s5], axis=-1) @ w_out
    return out, new_conv, new_h, sr[:, -1], si[:, -1]


def setup_inputs(seed: int = 0) -> dict:
    key = jax.random.key(seed)
    ks = iter(jax.random.split(key, 64))

    def nrm(shape, scale=1.0):
        return scale * jax.random.normal(next(ks), shape, jnp.float32)

    def gain(shape):
        return 1.0 + 0.05 * nrm(shape)

    L = DEPTH
    a8 = jax.random.uniform(next(ks), (L, D_LRU), jnp.float32, 0.9, 0.999)
    a_init = a8 ** (1.0 / C_GATE)
    lam = jnp.log(a_init) - jnp.log1p(-a_init)
    n_idx = jnp.arange(S5_STATE, dtype=jnp.float32)
    log_dt = jax.random.uniform(next(ks), (L, N_S5_GROUPS), jnp.float32,
                                math.log(0.001), math.log(0.1))
    return {
        "x_prompt": nrm((BATCH, SEQ, D_MODEL)),
        "x_sample": nrm((DEC_BATCH, DEC_SEQ, D_MODEL)),
        "c_prompt": nrm((BATCH, D_MODEL)),
        "c_sample": nrm((DEC_BATCH, D_MODEL)),
        "state_lru_conv": nrm((L, DEC_BATCH, CONV_W - 1, D_LRU)),
        "state_lru_h": nrm((L, DEC_BATCH, D_LRU), 0.5),
        "state_s5_re": nrm((L, DEC_BATCH, N_S5_GROUPS, S5_STATE), 0.1),
        "state_s5_im": nrm((L, DEC_BATCH, N_S5_GROUPS, S5_STATE), 0.1),
        "w_ada": nrm((L, D_MODEL, N_MOD * D_MODEL), 0.5 * D_MODEL ** -0.5),
        "b_ada": nrm((L, N_MOD * D_MODEL), 0.01),
        "norm_ffn1": gain((L, D_MODEL)),
        "w1_ffn1": nrm((L, D_MODEL, D_FF), D_MODEL ** -0.5),
        "w3_ffn1": nrm((L, D_MODEL, D_FF), D_MODEL ** -0.5),
        "w2_ffn1": nrm((L, D_FF, D_MODEL), D_FF ** -0.5),
        "norm_mix": gain((L, D_MODEL)),
        "w_in": nrm((L, D_MODEL, D_IN), D_MODEL ** -0.5),
        "conv_w": nrm((L, CONV_W, D_LRU), CONV_W ** -0.5),
        "conv_b": nrm((L, D_LRU), 0.01),
        "w_rg": nrm((L, LRU_HEADS, LRU_HEAD_DIM, LRU_HEAD_DIM), LRU_HEAD_DIM ** -0.5),
        "b_rg": nrm((L, D_LRU), 0.01),
        "w_ig": nrm((L, LRU_HEADS, LRU_HEAD_DIM, LRU_HEAD_DIM), LRU_HEAD_DIM ** -0.5),
        "b_ig": nrm((L, D_LRU), 0.01),
        "lru_lambda": lam,
        "s5_a_re": -0.5 + nrm((L, N_S5_GROUPS, S5_STATE), 0.01),
        "s5_a_im": jnp.pi * n_idx + nrm((L, N_S5_GROUPS, S5_STATE), 0.01),
        "s5_log_dt": log_dt,
        "s5_b_re": nrm((L, N_S5_GROUPS, S5_STATE, S5_GROUP), (2 * S5_GROUP) ** -0.5),
        "s5_b_im": nrm((L, N_S5_GROUPS, S5_STATE, S5_GROUP), (2 * S5_GROUP) ** -0.5),
        "s5_c_re": nrm((L, N_S5_GROUPS, S5_GROUP, S5_STATE), (2 * S5_STATE) ** -0.5),
        "s5_c_im": nrm((L, N_S5_GROUPS, S5_GROUP, S5_STATE), (2 * S5_STATE) ** -0.5),
        "s5_d": nrm((L, D_S5)),
        "w_glu": nrm((L, D_S5, D_S5), D_S5 ** -0.5),
        "b_glu": nrm((L, D_S5), 0.01),
        "w_out": nrm((L, D_MODEL, D_MODEL), D_MODEL ** -0.5),
        "norm_ffn2": gain((L, D_MODEL)),
        "w1_ffn2": nrm((L, D_MODEL, D_FF), D_MODEL ** -0.5),
        "w3_ffn2": nrm((L, D_MODEL, D_FF), D_MODEL ** -0.5),
        "w2_ffn2": nrm((L, D_FF, D_MODEL), D_FF ** -0.5),
        "norm_final": gain((D_MODEL,)),
    }


def reference(x_prompt, x_sample, c_prompt, c_sample, state_lru_conv, state_lru_h, state_s5_re,
              state_s5_im, w_ada, b_ada, norm_ffn1, w1_ffn1, w3_ffn1, w2_ffn1, norm_mix, w_in,
              conv_w, conv_b, w_rg, b_rg, w_ig, b_ig, lru_lambda, s5_a_re, s5_a_im, s5_log_dt,
              s5_b_re, s5_b_im, s5_c_re, s5_c_im, s5_d, w_glu, b_glu, w_out, norm_ffn2, w1_ffn2,
              w3_ffn2, w2_ffn2, norm_final):
    def run(x, c, conv0, h0, sr0, si0):
        convs, hs, srs, sis = [], [], [], []
        for l in range(DEPTH):
            mod = jax.nn.silu(c) @ w_ada[l] + b_ada[l]
            m = jnp.split(mod, N_MOD, axis=-1)
            h = modulate(rmsnorm(x, norm_ffn1[l]), m[0], m[1])
            x = x + FFN_RES * m[2][:, None, :] * swiglu(h, w1_ffn1[l], w3_ffn1[l], w2_ffn1[l])
            h = modulate(rmsnorm(x, norm_mix[l]), m[3], m[4])
            y, cv, hh, sr, si = mixer(h, conv0[l], h0[l], sr0[l], si0[l], w_in[l], conv_w[l],
                                      conv_b[l], w_rg[l], b_rg[l], w_ig[l], b_ig[l],
                                      lru_lambda[l], s5_a_re[l], s5_a_im[l], s5_log_dt[l],
                                      s5_b_re[l], s5_b_im[l], s5_c_re[l], s5_c_im[l], s5_d[l],
                                      w_glu[l], b_glu[l], w_out[l])
            x = x + m[5][:, None, :] * y
            h = modulate(rmsnorm(x, norm_ffn2[l]), m[6], m[7])
            x = x + FFN_RES * m[8][:, None, :] * swiglu(h, w1_ffn2[l], w3_ffn2[l], w2_ffn2[l])
            convs.append(cv)
            hs.append(hh)
            srs.append(sr)
            sis.append(si)
        return (rmsnorm(x, norm_final), jnp.stack(convs), jnp.stack(hs),
                jnp.stack(srs), jnp.stack(sis))

    Bp = x_prompt.shape[0]
    z_conv = jnp.zeros((DEPTH, Bp, CONV_W - 1, D_LRU), x_prompt.dtype)
    z_h = jnp.zeros((DEPTH, Bp, D_LRU), jnp.float32)
    z_s = jnp.zeros((DEPTH, Bp, N_S5_GROUPS, S5_STATE), jnp.float32)
    y_prompt, p_conv, p_h, p_sr, p_si = run(x_prompt, c_prompt, z_conv, z_h, z_s, z_s)
    y_sample, s_conv, s_h, s_sr, s_si = run(x_sample, c_sample, state_lru_conv, state_lru_h,
                                            state_s5_re, state_s5_im)
    return (y_prompt, y_sample, p_conv, p_h, p_sr, p_si, s_conv, s_h, s_sr, s_si)
```

```python
import functools

import jax
import jax.numpy as jnp
from jax import lax
from jax.experimental import pallas as pl
from jax.experimental.pallas import tpu as pltpu

EPS = 1e-6
C_GATE = 8.0
FFN_RES = 0.5
CONV_W = 4
N_MOD = 9
S5_A_RE_MAX = -1e-4
BF16 = jnp.bfloat16
F32 = jnp.float32

VMEM_LIMIT_BYTES = 56 * 1024 * 1024
FFN_ROWS = 512
MIXER_ROWS = 512


def _const_spec(shape, index):
    return pl.BlockSpec(shape, lambda i: index, pipeline_mode=pl.Buffered(1))


def _params(semantics):
    return pltpu.CompilerParams(dimension_semantics=semantics,
                                vmem_limit_bytes=VMEM_LIMIT_BYTES)


def _rmsnorm(x, g):
    return x * lax.rsqrt(jnp.mean(x * x, axis=-1, keepdims=True) + EPS) * g


def _rows3(v, nb):
    return v.reshape(v.shape[0] // nb, nb, v.shape[1])


def _ada_kernel(c_ref, w_ref, b_ref, o_ref):
    c = c_ref[...]
    s = (c * jax.nn.sigmoid(c)).astype(BF16)
    o_ref[...] = jnp.dot(s, w_ref[...].astype(BF16), preferred_element_type=F32) + b_ref[...]


def _ada_call(c_all, w_ada, b_ada):
    L, D, _ = w_ada.shape
    nseq = c_all.shape[0]
    return pl.pallas_call(
        _ada_kernel,
        grid=(L, N_MOD),
        in_specs=[
            pl.BlockSpec((nseq, D), lambda l, k: (0, 0)),
            pl.BlockSpec((None, D, D), lambda l, k: (l, 0, k)),
            pl.BlockSpec((None, None, 1, D), lambda l, k: (l, k, 0, 0)),
        ],
        out_specs=pl.BlockSpec((None, None, nseq, D), lambda l, k: (l, k, 0, 0)),
        out_shape=jax.ShapeDtypeStruct((L, N_MOD, nseq, D), F32),
        compiler_params=_params(("arbitrary", "arbitrary")),
        name="adaln_mod",
    )(c_all, w_ada, b_ada.reshape(L, N_MOD, 1, D))


def _s5_discretise(a_re, a_im, log_dt):
    ar = jnp.minimum(a_re, S5_A_RE_MAX)
    dt = jnp.exp(log_dt)
    mag = jnp.exp(ar * dt)
    return ar, a_im, mag * jnp.cos(a_im * dt), mag * jnp.sin(a_im * dt)


def _s5prep_kernel(are_c, aim_c, ldt_c, bre_ref, bim_ref, are_r, aim_r, ldt_r,
                   bbr_ref, bbi_ref, abr_ref, abi_ref):
    ar, ai, abr, abi = _s5_discretise(are_c[...], aim_c[...], ldt_c[...])
    den = ar * ar + ai * ai
    f_r = ((abr - 1.0) * ar + abi * ai) / den
    f_i = (abi * ar - (abr - 1.0) * ai) / den
    b_r = bre_ref[...]
    b_i = bim_ref[...]
    bbr_ref[...] = f_r * b_r - f_i * b_i
    bbi_ref[...] = f_r * b_i + f_i * b_r
    _, _, abr_row, abi_row = _s5_discretise(are_r[...], aim_r[...], ldt_r[...])
    abr_ref[...] = abr_row
    abi_ref[...] = abi_row


def _s5prep_call(a_re, a_im, log_dt, b_re, b_im):
    L, G, N = a_re.shape
    J = b_re.shape[-1]
    GN = G * N
    ldt = jnp.broadcast_to(log_dt[:, :, None], (L, G, N))
    col = lambda v: v.reshape(L, GN, 1)
    row = lambda v: v.reshape(L, 1, GN)
    cspec = pl.BlockSpec((None, GN, 1), lambda l: (l, 0, 0))
    rspec = pl.BlockSpec((None, 1, GN), lambda l: (l, 0, 0))
    bspec = pl.BlockSpec((None, GN, J), lambda l: (l, 0, 0))
    return pl.pallas_call(
        _s5prep_kernel,
        grid=(L,),
        in_specs=[cspec, cspec, cspec, bspec, bspec, rspec, rspec, rspec],
        out_specs=[bspec, bspec, rspec, rspec],
        out_shape=[jax.ShapeDtypeStruct((L, GN, J), F32), jax.ShapeDtypeStruct((L, GN, J), F32),
                   jax.ShapeDtypeStruct((L, 1, GN), F32), jax.ShapeDtypeStruct((L, 1, GN), F32)],
        compiler_params=_params(("arbitrary",)),
        name="s5_discretise",
    )(col(a_re), col(a_im), col(ldt), b_re.reshape(L, GN, J), b_im.reshape(L, GN, J),
      row(a_re), row(a_im), row(ldt))


def _ffn_kernel(x_ref, mod_ref, g_ref, w1_ref, w3_ref, w2_ref, gf_ref, o_ref, *, nb, final_norm):
    x = x_ref[...]
    y = _rmsnorm(x, g_ref[...])
    h = (_rows3(y, nb) * (1.0 + mod_ref[1][None]) + mod_ref[0][None]).reshape(x.shape)
    h = h.astype(BF16)
    a = jnp.dot(h, w1_ref[...], preferred_element_type=F32)
    b = jnp.dot(h, w3_ref[...], preferred_element_type=F32)
    act = (a * jax.nn.sigmoid(a) * b).astype(BF16)
    f = jnp.dot(act, w2_ref[...], preferred_element_type=F32)
    out = (_rows3(x, nb) + (FFN_RES * mod_ref[2])[None] * _rows3(f, nb)).reshape(x.shape)
    if final_norm:
        out = _rmsnorm(out, gf_ref[...])
    o_ref[...] = out


def _ffn_call(x, mod, layer, mod_group, mod_rowblk, nb, g, w1, w3, w2, gf, final_norm, name):
    rows, D = x.shape
    F = w1.shape[-1]
    tm = min(FFN_ROWS, rows)
    assert rows % tm == 0 and tm % nb == 0
    return pl.pallas_call(
        functools.partial(_ffn_kernel, nb=nb, final_norm=final_norm),
        grid=(rows // tm,),
        in_specs=[
            pl.BlockSpec((tm, D), lambda i: (i, 0)),
            _const_spec((None, 3, nb, D), (layer, mod_group, mod_rowblk, 0)),
            _const_spec((None, 1, D), (layer, 0, 0)),
            _const_spec((None, D, F), (layer, 0, 0)),
            _const_spec((None, D, F), (layer, 0, 0)),
            _const_spec((None, F, D), (layer, 0, 0)),
            _const_spec((1, D), (0, 0)),
        ],
        out_specs=pl.BlockSpec((tm, D), lambda i: (i, 0)),
        out_shape=jax.ShapeDtypeStruct((rows, D), F32),
        compiler_params=_params(("arbitrary",)),
        name=name,
    )(x, mod, g, w1, w3, w2, gf)


def _softplus(v):
    return jnp.maximum(v, 0.0) + jnp.log1p(jnp.exp(-jnp.abs(v)))


def _mixer_kernel(x_ref, mod_ref, gn_ref, conv0_ref, h0_ref, s0_ref,
                  win_ref, convw_ref, convb_ref, wg_ref, bg_ref, lam_ref,
                  abar_ref, bb_ref, cb_ref, d_ref, wglu_ref, bglu_ref, wout_ref,
                  xo_ref, convo_ref, ho_ref, so_ref,
                  xp_scr, a_scr, b_scr, s_scr, *, nb):
    rows, D = x_ref.shape
    tt = rows // nb
    DL = ho_ref.shape[-1]
    DS = d_ref.shape[-1]
    HS = DS // 2
    HN = so_ref.shape[-1] // 2

    @pl.when(pl.program_id(0) == 0)
    def _():
        convo_ref[...] = conv0_ref[...]
        ho_ref[...] = h0_ref[...]
        so_ref[...] = s0_ref[...]

    x = x_ref[...]
    y = _rmsnorm(x, gn_ref[...])
    h = (_rows3(y, nb) * (1.0 + mod_ref[1][None]) + mod_ref[0][None]).reshape(rows, D)
    z = jnp.dot(h.astype(BF16), win_ref[...], preferred_element_type=F32)
    xb = z[:, :DL]
    yb = z[:, DL:2 * DL]
    u = z[:, 2 * DL:]

    halo = (CONV_W - 1) * nb
    xp_scr[0:halo, :] = convo_ref[...]
    xp_scr[halo:halo + rows, :] = xb
    cw = convw_ref[...]
    xc = convb_ref[...] + sum(xp_scr[k * nb:k * nb + rows, :] * cw[k:k + 1, :]
                              for k in range(CONV_W))
    convo_ref[...] = xp_scr[rows:rows + halo, :]

    gpre = jnp.dot(xc.astype(BF16), wg_ref[...], preferred_element_type=F32) + bg_ref[...]
    r = jax.nn.sigmoid(gpre[:, :DL])
    ig = jax.nn.sigmoid(gpre[:, DL:])
    log_a = -C_GATE * r * _softplus(-lam_ref[...])
    a = jnp.exp(log_a)
    mult = jnp.sqrt(-jnp.tanh(log_a) * (a * a + 1.0))
    a_scr[...] = a
    b_scr[...] = mult * ig * xc

    def lru_step(t, hcur):
        sl = pl.ds(pl.multiple_of(t * nb, nb), nb)
        hnew = a_scr[sl, :] * hcur + b_scr[sl, :]
        b_scr[sl, :] = hnew
        return hnew

    ho_ref[...] = lax.fori_loop(0, tt, lru_step, ho_ref[...])
    y_lru = jax.nn.gelu(yb) * b_scr[...]

    ub = u.astype(BF16)
    for hf in range(2):
        s_scr[hf] = jnp.dot(ub[:, hf * HS:(hf + 1) * HS], bb_ref[hf], preferred_element_type=F32)

    def s5_step(t, carry):
        sl = pl.ds(pl.multiple_of(t * nb, nb), nb)
        for hf in range(2):
            st = so_ref[hf]
            ab = abar_ref[hf]
            bu = s_scr[hf, sl, :]
            s_re, s_im = st[:, :HN], st[:, HN:]
            a_r, a_i = ab[:, :HN], ab[:, HN:]
            new = jnp.concatenate([a_r * s_re - a_i * s_im + bu[:, :HN],
                                   a_r * s_im + a_i * s_re + bu[:, HN:]], axis=-1)
            so_ref[hf] = new
            s_scr[hf, sl, :] = new
        return carry

    lax.fori_loop(0, tt, s5_step, 0)

    ys = []
    for hf in range(2):
        sb = s_scr[hf].astype(BF16)
        ys.append(jnp.dot(sb[:, :HN], cb_ref[hf, 0], preferred_element_type=F32)
                  - jnp.dot(sb[:, HN:], cb_ref[hf, 1], preferred_element_type=F32))
    ys = jnp.concatenate(ys, axis=-1) + d_ref[...] * u
    g = jax.nn.gelu(ys)
    y_s5 = g * jax.nn.sigmoid(
        jnp.dot(g.astype(BF16), wglu_ref[...], preferred_element_type=F32) + bglu_ref[...])

    ycat = jnp.concatenate([y_lru, y_s5], axis=-1).astype(BF16)
    out = jnp.dot(ycat, wout_ref[...], preferred_element_type=F32)
    xo_ref[...] = (_rows3(x, nb) + mod_ref[2][None] * _rows3(out, nb)).reshape(rows, D)


def _mixer_call(x, mod, layer, mod_rowblk, nb, conv0, h0, s0, wts, name):
    rows, D = x.shape
    tm = min(MIXER_ROWS, rows)
    assert rows % tm == 0 and tm % nb == 0
    DL = h0.shape[-1]
    NS2 = s0.shape[-1]
    halo = (CONV_W - 1) * nb

    def wspec(w, per_layer=True):
        if per_layer:
            return _const_spec((None,) + w.shape[1:], (layer,) + (0,) * (w.ndim - 1))
        return _const_spec(w.shape, (0,) * w.ndim)

    state_specs = [
        _const_spec((None, halo, DL), (layer, 0, 0)),
        _const_spec((None, nb, DL), (layer, 0, 0)),
        _const_spec((None, 2, nb, NS2), (layer, 0, 0, 0)),
    ]
    out_state_specs = [
        pl.BlockSpec((halo, DL), lambda i: (0, 0)),
        pl.BlockSpec((nb, DL), lambda i: (0, 0)),
        pl.BlockSpec((2, nb, NS2), lambda i: (0, 0, 0)),
    ]
    return pl.pallas_call(
        functools.partial(_mixer_kernel, nb=nb),
        grid=(rows // tm,),
        in_specs=[pl.BlockSpec((tm, D), lambda i: (i, 0)),
                  _const_spec((None, 3, nb, D), (layer, 1, mod_rowblk, 0))]
                 + [wspec(wts[0])] + state_specs + [wspec(w) for w in wts[1:]],
        out_specs=[pl.BlockSpec((tm, D), lambda i: (i, 0))] + out_state_specs,
        out_shape=[jax.ShapeDtypeStruct((rows, D), F32),
                   jax.ShapeDtypeStruct((halo, DL), F32),
                   jax.ShapeDtypeStruct((nb, DL), F32),
                   jax.ShapeDtypeStruct((2, nb, NS2), F32)],
        scratch_shapes=[pltpu.VMEM((tm + halo, DL), F32),
                        pltpu.VMEM((tm, DL), F32),
                        pltpu.VMEM((tm, DL), F32),
                        pltpu.VMEM((2, tm, NS2), F32)],
        compiler_params=_params(("arbitrary",)),
        name=name,
    )(x, mod, wts[0], conv0, h0, s0, *wts[1:])


def _block_diag(w):
    *lead, nblk, r, c = w.shape
    eye = jnp.eye(nblk, dtype=bool)[:, None, :, None]
    full = jnp.where(eye, w[..., :, :, None, :], jnp.zeros((), w.dtype))
    return full.reshape(*lead, nblk * r, nblk * c)


def _halves(re, im):
    n = re.shape[-1] // 2
    return jnp.stack([jnp.concatenate([re[..., :n], im[..., :n]], axis=-1),
                      jnp.concatenate([re[..., n:], im[..., n:]], axis=-1)], axis=-2)


def _unhalve(s):
    n = s.shape[-1] // 2
    return (jnp.concatenate([s[0, :, :n], s[1, :, :n]], axis=-1),
            jnp.concatenate([s[0, :, n:], s[1, :, n:]], axis=-1))


def kernel(x_prompt, x_sample, c_prompt, c_sample, state_lru_conv, state_lru_h, state_s5_re, state_s5_im, w_ada, b_ada, norm_ffn1, w1_ffn1, w3_ffn1, w2_ffn1, norm_mix, w_in, conv_w, conv_b, w_rg, b_rg, w_ig, b_ig, lru_lambda, s5_a_re, s5_a_im, s5_log_dt, s5_b_re, s5_b_im, s5_c_re, s5_c_im, s5_d, w_glu, b_glu, w_out, norm_ffn2, w1_ffn2, w3_ffn2, w2_ffn2, norm_final):
    L, D, _ = w_ada.shape
    Bp, Tp, _ = x_prompt.shape
    Bs, Ts, _ = x_sample.shape
    _, G, N = s5_a_re.shape
    J = s5_b_re.shape[-1]
    DL = lru_lambda.shape[-1]
    GN = G * N
    assert Bs % Bp == 0 and Bp % 8 == 0 and G % 2 == 0

    mod = _ada_call(jnp.concatenate([c_sample, c_prompt], axis=0), w_ada, b_ada)

    bb_r, bb_i, ab_r, ab_i = _s5prep_call(s5_a_re, s5_a_im, s5_log_dt, s5_b_re, s5_b_im)
    abar = _halves(ab_r, ab_i)
    abar = jnp.swapaxes(abar, 1, 2)

    def in_blocks(bb):
        t = bb.reshape(L, 2, G // 2, N, J).swapaxes(-1, -2)
        return _block_diag(t)

    bmat = jnp.concatenate([in_blocks(bb_r), in_blocks(bb_i)], axis=-1).astype(BF16)

    def out_blocks(c):
        t = c.reshape(L, 2, G // 2, J, N).swapaxes(-1, -2)
        return _block_diag(t)

    cmat = jnp.stack([out_blocks(s5_c_re), out_blocks(s5_c_im)], axis=2).astype(BF16)
    wgate = jnp.concatenate([_block_diag(w_rg), _block_diag(w_ig)], axis=-1).astype(BF16)
    bgate = jnp.concatenate([b_rg, b_ig], axis=-1).reshape(L, 1, 2 * DL)

    r3 = lambda v: v.reshape(L, 1, v.shape[-1])
    mixer_wts = [r3(norm_mix), w_in.astype(BF16), conv_w, r3(conv_b), wgate, bgate,
                 r3(lru_lambda), abar, bmat, cmat, r3(s5_d), w_glu.astype(BF16), r3(b_glu),
                 w_out.astype(BF16)]
    ffn1_wts = (r3(norm_ffn1), w1_ffn1.astype(BF16), w3_ffn1.astype(BF16), w2_ffn1.astype(BF16))
    ffn2_wts = (r3(norm_ffn2), w1_ffn2.astype(BF16), w3_ffn2.astype(BF16), w2_ffn2.astype(BF16))
    gfin = norm_final.reshape(1, D)

    def run(x, nb, mod_rowblk, conv0, h0, s0, tag):
        T = x.shape[1]
        xt = jnp.swapaxes(x, 0, 1).reshape(T * nb, D)
        convs, hs, srs, sis = [], [], [], []
        for l in range(L):
            xt = _ffn_call(xt, mod, l, 0, mod_rowblk, nb, *ffn1_wts, gfin, False,
                           f"ffn1_{tag}{l}")
            xt, cv, hh, ss = _mixer_call(xt, mod, l, mod_rowblk, nb, conv0, h0, s0, mixer_wts,
                                         f"mixer_{tag}{l}")
            xt = _ffn_call(xt, mod, l, 2, mod_rowblk, nb, *ffn2_wts, gfin, l == L - 1,
                           f"ffn2_{tag}{l}")
            sr, si = _unhalve(ss)
            convs.append(jnp.swapaxes(cv.reshape(CONV_W - 1, nb, DL), 0, 1))
            hs.append(hh)
            srs.append(sr.reshape(nb, G, N))
            sis.append(si.reshape(nb, G, N))
        y = jnp.swapaxes(xt.reshape(T, nb, D), 0, 1)
        return y, jnp.stack(convs), jnp.stack(hs), jnp.stack(srs), jnp.stack(sis)

    zeros = lambda *s: jnp.zeros(s, F32)
    p_out = run(x_prompt, Bp, Bs // Bp, zeros(L, (CONV_W - 1) * Bp, DL), zeros(L, Bp, DL),
                zeros(L, 2, Bp, GN), "p")
    conv_s = jnp.swapaxes(state_lru_conv, 1, 2).reshape(L, (CONV_W - 1) * Bs, DL)
    s0_s = _halves(state_s5_re.reshape(L, Bs, GN), state_s5_im.reshape(L, Bs, GN))
    s_out = run(x_sample, Bs, 0, conv_s, state_lru_h, jnp.swapaxes(s0_s, 1, 2), "s")
    return (p_out[0], s_out[0]) + p_out[1:] + s_out[1:]
```

```python
import functools

import jax
import jax.numpy as jnp
from jax import lax
from jax.experimental import pallas as pl
from jax.experimental.pallas import tpu as pltpu

EPS = 1e-6
C_GATE = 8.0
FFN_RES = 0.5
CONV_W = 4
N_MOD = 9
S5_A_RE_MAX = -1e-4
BF16 = jnp.bfloat16
F32 = jnp.float32

VMEM_LIMIT_BYTES = 56 * 1024 * 1024
FFN_ROWS = 512
MIXER_ROWS = 512
SCAN_ROWS = 8
SCAN_LANES = 512


def _const_spec(shape, index):
    return pl.BlockSpec(shape, lambda i: index, pipeline_mode=pl.Buffered(1))


def _params(semantics):
    return pltpu.CompilerParams(dimension_semantics=semantics,
                                vmem_limit_bytes=VMEM_LIMIT_BYTES)


def _rmsnorm(x, g):
    return x * lax.rsqrt(jnp.mean(x * x, axis=-1, keepdims=True) + EPS) * g


def _rows3(v, nb):
    return v.reshape(v.shape[0] // nb, nb, v.shape[1])


def _ada_kernel(c_ref, w_ref, b_ref, o_ref):
    c = c_ref[...]
    s = (c * jax.nn.sigmoid(c)).astype(BF16)
    o_ref[...] = jnp.dot(s, w_ref[...].astype(BF16), preferred_element_type=F32) + b_ref[...]


def _ada_call(c_all, w_ada, b_ada):
    L, D, _ = w_ada.shape
    nseq = c_all.shape[0]
    return pl.pallas_call(
        _ada_kernel,
        grid=(L, N_MOD),
        in_specs=[
            pl.BlockSpec((nseq, D), lambda l, k: (0, 0)),
            pl.BlockSpec((None, D, D), lambda l, k: (l, 0, k)),
            pl.BlockSpec((None, None, 1, D), lambda l, k: (l, k, 0, 0)),
        ],
        out_specs=pl.BlockSpec((None, None, nseq, D), lambda l, k: (l, k, 0, 0)),
        out_shape=jax.ShapeDtypeStruct((L, N_MOD, nseq, D), F32),
        compiler_params=_params(("arbitrary", "arbitrary")),
        name="adaln_mod",
    )(c_all, w_ada, b_ada.reshape(L, N_MOD, 1, D))


def _s5_discretise(a_re, a_im, log_dt):
    ar = jnp.minimum(a_re, S5_A_RE_MAX)
    dt = jnp.exp(log_dt)
    mag = jnp.exp(ar * dt)
    return ar, a_im, mag * jnp.cos(a_im * dt), mag * jnp.sin(a_im * dt)


def _s5prep_kernel(are_c, aim_c, ldt_c, bre_ref, bim_ref, are_r, aim_r, ldt_r,
                   bbr_ref, bbi_ref, abr_ref, abi_ref):
    ar, ai, abr, abi = _s5_discretise(are_c[...], aim_c[...], ldt_c[...])
    den = ar * ar + ai * ai
    f_r = ((abr - 1.0) * ar + abi * ai) / den
    f_i = (abi * ar - (abr - 1.0) * ai) / den
    b_r = bre_ref[...]
    b_i = bim_ref[...]
    bbr_ref[...] = f_r * b_r - f_i * b_i
    bbi_ref[...] = f_r * b_i + f_i * b_r
    _, _, abr_row, abi_row = _s5_discretise(are_r[...], aim_r[...], ldt_r[...])
    abr_ref[...] = abr_row
    abi_ref[...] = abi_row


def _s5prep_call(a_re, a_im, log_dt, b_re, b_im):
    L, G, N = a_re.shape
    J = b_re.shape[-1]
    GN = G * N
    ldt = jnp.broadcast_to(log_dt[:, :, None], (L, G, N))
    col = lambda v: v.reshape(L, GN, 1)
    row = lambda v: v.reshape(L, 1, GN)
    cspec = pl.BlockSpec((None, GN, 1), lambda l: (l, 0, 0))
    rspec = pl.BlockSpec((None, 1, GN), lambda l: (l, 0, 0))
    bspec = pl.BlockSpec((None, GN, J), lambda l: (l, 0, 0))
    return pl.pallas_call(
        _s5prep_kernel,
        grid=(L,),
        in_specs=[cspec, cspec, cspec, bspec, bspec, rspec, rspec, rspec],
        out_specs=[bspec, bspec, rspec, rspec],
        out_shape=[jax.ShapeDtypeStruct((L, GN, J), F32), jax.ShapeDtypeStruct((L, GN, J), F32),
                   jax.ShapeDtypeStruct((L, 1, GN), F32), jax.ShapeDtypeStruct((L, 1, GN), F32)],
        compiler_params=_params(("arbitrary",)),
        name="s5_discretise",
    )(col(a_re), col(a_im), col(ldt), b_re.reshape(L, GN, J), b_im.reshape(L, GN, J),
      row(a_re), row(a_im), row(ldt))


def _ffn_kernel(x_ref, mod_ref, g_ref, w1_ref, w3_ref, w2_ref, gf_ref, o_ref, *, nb, final_norm):
    x = x_ref[...]
    y = _rmsnorm(x, g_ref[...])
    h = (_rows3(y, nb) * (1.0 + mod_ref[1][None]) + mod_ref[0][None]).reshape(x.shape)
    h = h.astype(BF16)
    a = jnp.dot(h, w1_ref[...], preferred_element_type=F32)
    b = jnp.dot(h, w3_ref[...], preferred_element_type=F32)
    act = (a * jax.nn.sigmoid(a) * b).astype(BF16)
    f = jnp.dot(act, w2_ref[...], preferred_element_type=F32)
    out = (_rows3(x, nb) + (FFN_RES * mod_ref[2])[None] * _rows3(f, nb)).reshape(x.shape)
    if final_norm:
        out = _rmsnorm(out, gf_ref[...])
    o_ref[...] = out


def _ffn_call(x, mod, layer, mod_group, mod_rowblk, nb, g, w1, w3, w2, gf, final_norm, name):
    rows, D = x.shape
    F = w1.shape[-1]
    tm = min(FFN_ROWS, rows)
    assert rows % tm == 0 and tm % nb == 0
    return pl.pallas_call(
        functools.partial(_ffn_kernel, nb=nb, final_norm=final_norm),
        grid=(rows // tm,),
        in_specs=[
            pl.BlockSpec((tm, D), lambda i: (i, 0)),
            _const_spec((None, 3, nb, D), (layer, mod_group, mod_rowblk, 0)),
            _const_spec((None, 1, D), (layer, 0, 0)),
            _const_spec((None, D, F), (layer, 0, 0)),
            _const_spec((None, D, F), (layer, 0, 0)),
            _const_spec((None, F, D), (layer, 0, 0)),
            _const_spec((1, D), (0, 0)),
        ],
        out_specs=pl.BlockSpec((tm, D), lambda i: (i, 0)),
        out_shape=jax.ShapeDtypeStruct((rows, D), F32),
        compiler_params=_params(("arbitrary",)),
        name=name,
    )(x, mod, g, w1, w3, w2, gf)


def _softplus(v):
    return jnp.maximum(v, 0.0) + jnp.log1p(jnp.exp(-jnp.abs(v)))


def _mixer_kernel(x_ref, mod_ref, gn_ref, conv0_ref, h0_ref, s0_ref,
                  win_ref, convw_ref, convb_ref, wg_ref, bg_ref, lam_ref,
                  abar_ref, bb_ref, cb_ref, d_ref, wglu_ref, bglu_ref, wout_ref,
                  xo_ref, convo_ref, ho_ref, so_ref,
                  xp_scr, a_scr, b_scr, s_scr, *, nb):
    rows, D = x_ref.shape
    tt = rows // nb
    DL = ho_ref.shape[-1]
    DS = d_ref.shape[-1]
    HS = DS // 2
    HN = so_ref.shape[-1] // 2

    @pl.when(pl.program_id(0) == 0)
    def _():
        convo_ref[...] = conv0_ref[...]
        ho_ref[...] = h0_ref[...]
        so_ref[...] = s0_ref[...]

    x = x_ref[...]
    y = _rmsnorm(x, gn_ref[...])
    h = (_rows3(y, nb) * (1.0 + mod_ref[1][None]) + mod_ref[0][None]).reshape(rows, D)
    z = jnp.dot(h.astype(BF16), win_ref[...], preferred_element_type=F32)
    xb = z[:, :DL]
    yb = z[:, DL:2 * DL]
    u = z[:, 2 * DL:]

    halo = (CONV_W - 1) * nb
    xp_scr[0:halo, :] = convo_ref[...]
    xp_scr[halo:halo + rows, :] = xb
    cw = convw_ref[...]
    xc = convb_ref[...] + sum(xp_scr[k * nb:k * nb + rows, :] * cw[k:k + 1, :]
                              for k in range(CONV_W))
    convo_ref[...] = xp_scr[rows:rows + halo, :]

    gpre = jnp.dot(xc.astype(BF16), wg_ref[...], preferred_element_type=F32) + bg_ref[...]
    r = jax.nn.sigmoid(gpre[:, :DL])
    ig = jax.nn.sigmoid(gpre[:, DL:])
    log_a = -C_GATE * r * _softplus(-lam_ref[...])
    a = jnp.exp(log_a)
    mult = jnp.sqrt(-jnp.tanh(log_a) * (a * a + 1.0))
    a_scr[...] = a
    b_scr[...] = mult * ig * xc

    for r0 in range(0, nb, SCAN_ROWS):
        hcur = ho_ref[r0:r0 + SCAN_ROWS, :]
        for t in range(tt):
            sl = slice(t * nb + r0, t * nb + r0 + SCAN_ROWS)
            hcur = a_scr[sl, :] * hcur + b_scr[sl, :]
            b_scr[sl, :] = hcur
        ho_ref[r0:r0 + SCAN_ROWS, :] = hcur
    y_lru = jax.nn.gelu(yb) * b_scr[...]

    ub = u.astype(BF16)
    for hf in range(2):
        s_scr[hf] = jnp.dot(ub[:, hf * HS:(hf + 1) * HS], bb_ref[hf], preferred_element_type=F32)

    for hf in range(2):
        for c0 in range(0, HN, SCAN_LANES):
            re_l = slice(c0, c0 + SCAN_LANES)
            im_l = slice(HN + c0, HN + c0 + SCAN_LANES)
            a_r = jnp.broadcast_to(abar_ref[hf, :, re_l], (SCAN_ROWS, SCAN_LANES))
            a_i = jnp.broadcast_to(abar_ref[hf, :, im_l], (SCAN_ROWS, SCAN_LANES))
            for r0 in range(0, nb, SCAN_ROWS):
                s_re = so_ref[hf, r0:r0 + SCAN_ROWS, re_l]
                s_im = so_ref[hf, r0:r0 + SCAN_ROWS, im_l]
                for t in range(tt):
                    sl = slice(t * nb + r0, t * nb + r0 + SCAN_ROWS)
                    n_re = a_r * s_re - a_i * s_im + s_scr[hf, sl, re_l]
                    n_im = a_r * s_im + a_i * s_re + s_scr[hf, sl, im_l]
                    s_re, s_im = n_re, n_im
                    s_scr[hf, sl, re_l] = s_re
                    s_scr[hf, sl, im_l] = s_im
                so_ref[hf, r0:r0 + SCAN_ROWS, re_l] = s_re
                so_ref[hf, r0:r0 + SCAN_ROWS, im_l] = s_im

    ys = []
    for hf in range(2):
        sb = s_scr[hf].astype(BF16)
        ys.append(jnp.dot(sb[:, :HN], cb_ref[hf, 0], preferred_element_type=F32)
                  - jnp.dot(sb[:, HN:], cb_ref[hf, 1], preferred_element_type=F32))
    ys = jnp.concatenate(ys, axis=-1) + d_ref[...] * u
    g = jax.nn.gelu(ys)
    y_s5 = g * jax.nn.sigmoid(
        jnp.dot(g.astype(BF16), wglu_ref[...], preferred_element_type=F32) + bglu_ref[...])

    ycat = jnp.concatenate([y_lru, y_s5], axis=-1).astype(BF16)
    out = jnp.dot(ycat, wout_ref[...], preferred_element_type=F32)
    xo_ref[...] = (_rows3(x, nb) + mod_ref[2][None] * _rows3(out, nb)).reshape(rows, D)


def _mixer_call(x, mod, layer, mod_rowblk, nb, conv0, h0, s0, wts, name):
    rows, D = x.shape
    tm = min(MIXER_ROWS, rows)
    assert rows % tm == 0 and tm % nb == 0
    DL = h0.shape[-1]
    NS2 = s0.shape[-1]
    halo = (CONV_W - 1) * nb

    def wspec(w, per_layer=True):
        if per_layer:
            return _const_spec((None,) + w.shape[1:], (layer,) + (0,) * (w.ndim - 1))
        return _const_spec(w.shape, (0,) * w.ndim)

    state_specs = [
        _const_spec((None, halo, DL), (layer, 0, 0)),
        _const_spec((None, nb, DL), (layer, 0, 0)),
        _const_spec((None, 2, nb, NS2), (layer, 0, 0, 0)),
    ]
    out_state_specs = [
        pl.BlockSpec((halo, DL), lambda i: (0, 0)),
        pl.BlockSpec((nb, DL), lambda i: (0, 0)),
        pl.BlockSpec((2, nb, NS2), lambda i: (0, 0, 0)),
    ]
    return pl.pallas_call(
        functools.partial(_mixer_kernel, nb=nb),
        grid=(rows // tm,),
        in_specs=[pl.BlockSpec((tm, D), lambda i: (i, 0)),
                  _const_spec((None, 3, nb, D), (layer, 1, mod_rowblk, 0))]
                 + [wspec(wts[0])] + state_specs + [wspec(w) for w in wts[1:]],
        out_specs=[pl.BlockSpec((tm, D), lambda i: (i, 0))] + out_state_specs,
        out_shape=[jax.ShapeDtypeStruct((rows, D), F32),
                   jax.ShapeDtypeStruct((halo, DL), F32),
                   jax.ShapeDtypeStruct((nb, DL), F32),
                   jax.ShapeDtypeStruct((2, nb, NS2), F32)],
        scratch_shapes=[pltpu.VMEM((tm + halo, DL), F32),
                        pltpu.VMEM((tm, DL), F32),
                        pltpu.VMEM((tm, DL), F32),
                        pltpu.VMEM((2, tm, NS2), F32)],
        compiler_params=_params(("arbitrary",)),
        name=name,
    )(x, mod, wts[0], conv0, h0, s0, *wts[1:])


def _block_diag(w):
    *lead, nblk, r, c = w.shape
    eye = jnp.eye(nblk, dtype=bool)[:, None, :, None]
    full = jnp.where(eye, w[..., :, :, None, :], jnp.zeros((), w.dtype))
    return full.reshape(*lead, nblk * r, nblk * c)


def _halves(re, im):
    n = re.shape[-1] // 2
    return jnp.stack([jnp.concatenate([re[..., :n], im[..., :n]], axis=-1),
                      jnp.concatenate([re[..., n:], im[..., n:]], axis=-1)], axis=-2)


def _unhalve(s):
    n = s.shape[-1] // 2
    return (jnp.concatenate([s[0, :, :n], s[1, :, :n]], axis=-1),
            jnp.concatenate([s[0, :, n:], s[1, :, n:]], axis=-1))


def kernel(x_prompt, x_sample, c_prompt, c_sample, state_lru_conv, state_lru_h, state_s5_re, state_s5_im, w_ada, b_ada, norm_ffn1, w1_ffn1, w3_ffn1, w2_ffn1, norm_mix, w_in, conv_w, conv_b, w_rg, b_rg, w_ig, b_ig, lru_lambda, s5_a_re, s5_a_im, s5_log_dt, s5_b_re, s5_b_im, s5_c_re, s5_c_im, s5_d, w_glu, b_glu, w_out, norm_ffn2, w1_ffn2, w3_ffn2, w2_ffn2, norm_final):
    L, D, _ = w_ada.shape
    Bp, Tp, _ = x_prompt.shape
    Bs, Ts, _ = x_sample.shape
    _, G, N = s5_a_re.shape
    J = s5_b_re.shape[-1]
    DL = lru_lambda.shape[-1]
    GN = G * N
    assert Bs % Bp == 0 and Bp % 8 == 0 and G % 2 == 0

    mod = _ada_call(jnp.concatenate([c_sample, c_prompt], axis=0), w_ada, b_ada)

    bb_r, bb_i, ab_r, ab_i = _s5prep_call(s5_a_re, s5_a_im, s5_log_dt, s5_b_re, s5_b_im)
    abar = _halves(ab_r, ab_i)
    abar = jnp.swapaxes(abar, 1, 2)

    def in_blocks(bb):
        t = bb.reshape(L, 2, G // 2, N, J).swapaxes(-1, -2)
        return _block_diag(t)

    bmat = jnp.concatenate([in_blocks(bb_r), in_blocks(bb_i)], axis=-1).astype(BF16)

    def out_blocks(c):
        t = c.reshape(L, 2, G // 2, J, N).swapaxes(-1, -2)
        return _block_diag(t)

    cmat = jnp.stack([out_blocks(s5_c_re), out_blocks(s5_c_im)], axis=2).astype(BF16)
    wgate = jnp.concatenate([_block_diag(w_rg), _block_diag(w_ig)], axis=-1).astype(BF16)
    bgate = jnp.concatenate([b_rg, b_ig], axis=-1).reshape(L, 1, 2 * DL)

    r3 = lambda v: v.reshape(L, 1, v.shape[-1])
    mixer_wts = [r3(norm_mix), w_in.astype(BF16), conv_w, r3(conv_b), wgate, bgate,
                 r3(lru_lambda), abar, bmat, cmat, r3(s5_d), w_glu.astype(BF16), r3(b_glu),
                 w_out.astype(BF16)]
    ffn1_wts = (r3(norm_ffn1), w1_ffn1.astype(BF16), w3_ffn1.astype(BF16), w2_ffn1.astype(BF16))
    ffn2_wts = (r3(norm_ffn2), w1_ffn2.astype(BF16), w3_ffn2.astype(BF16), w2_ffn2.astype(BF16))
    gfin = norm_final.reshape(1, D)

    def run(x, nb, mod_rowblk, conv0, h0, s0, tag):
        T = x.shape[1]
        xt = jnp.swapaxes(x, 0, 1).reshape(T * nb, D)
        convs, hs, srs, sis = [], [], [], []
        for l in range(L):
            xt = _ffn_call(xt, mod, l, 0, mod_rowblk, nb, *ffn1_wts, gfin, False,
                           f"ffn1_{tag}{l}")
            xt, cv, hh, ss = _mixer_call(xt, mod, l, mod_rowblk, nb, conv0, h0, s0, mixer_wts,
                                         f"mixer_{tag}{l}")
            xt = _ffn_call(xt, mod, l, 2, mod_rowblk, nb, *ffn2_wts, gfin, l == L - 1,
                           f"ffn2_{tag}{l}")
            sr, si = _unhalve(ss)
            convs.append(jnp.swapaxes(cv.reshape(CONV_W - 1, nb, DL), 0, 1))
            hs.append(hh)
            srs.append(sr.reshape(nb, G, N))
            sis.append(si.reshape(nb, G, N))
        y = jnp.swapaxes(xt.reshape(T, nb, D), 0, 1)
        return y, jnp.stack(convs), jnp.stack(hs), jnp.stack(srs), jnp.stack(sis)

    zeros = lambda *s: jnp.zeros(s, F32)
    p_out = run(x_prompt, Bp, Bs // Bp, zeros(L, (CONV_W - 1) * Bp, DL), zeros(L, Bp, DL),
                zeros(L, 2, Bp, GN), "p")
    conv_s = jnp.swapaxes(state_lru_conv, 1, 2).reshape(L, (CONV_W - 1) * Bs, DL)
    s0_s = _halves(state_s5_re.reshape(L, Bs, GN), state_s5_im.reshape(L, Bs, GN))
    s_out = run(x_sample, Bs, 0, conv_s, state_lru_h, jnp.swapaxes(s0_s, 1, 2), "s")
    return (p_out[0], s_out[0]) + p_out[1:] + s_out[1:]
```

```python
import functools

import jax
import jax.numpy as jnp
from jax import lax
from jax.experimental import pallas as pl
from jax.experimental.pallas import tpu as pltpu

EPS = 1e-6
C_GATE = 8.0
FFN_RES = 0.5
CONV_W = 4
N_MOD = 9
S5_A_RE_MAX = -1e-4
BF16 = jnp.bfloat16
F32 = jnp.float32

VMEM_LIMIT_BYTES = 56 * 1024 * 1024
FFN_ROWS = 1024
FFN_SUB_ROWS = 512
MIXER_ROWS = 512
SCAN_ROWS = 8
SCAN_LANES = 512


def _const_spec(shape, index):
    return pl.BlockSpec(shape, lambda i: index, pipeline_mode=pl.Buffered(1))


def _params(semantics):
    return pltpu.CompilerParams(dimension_semantics=semantics,
                                vmem_limit_bytes=VMEM_LIMIT_BYTES)


def _rmsnorm(x, g):
    return x * lax.rsqrt(jnp.mean(x * x, axis=-1, keepdims=True) + EPS) * g


def _rows3(v, nb):
    return v.reshape(v.shape[0] // nb, nb, v.shape[1])


def _ada_kernel(c_ref, w_ref, b_ref, o_ref):
    c = c_ref[...]
    s = (c * jax.nn.sigmoid(c)).astype(BF16)
    o_ref[...] = jnp.dot(s, w_ref[...].astype(BF16), preferred_element_type=F32) + b_ref[...]


def _ada_call(c_all, w_ada, b_ada):
    L, D, _ = w_ada.shape
    nseq = c_all.shape[0]
    return pl.pallas_call(
        _ada_kernel,
        grid=(L, N_MOD),
        in_specs=[
            pl.BlockSpec((nseq, D), lambda l, k: (0, 0)),
            pl.BlockSpec((None, D, D), lambda l, k: (l, 0, k)),
            pl.BlockSpec((None, None, 1, D), lambda l, k: (l, k, 0, 0)),
        ],
        out_specs=pl.BlockSpec((None, None, nseq, D), lambda l, k: (l, k, 0, 0)),
        out_shape=jax.ShapeDtypeStruct((L, N_MOD, nseq, D), F32),
        compiler_params=_params(("arbitrary", "arbitrary")),
        name="adaln_mod",
    )(c_all, w_ada, b_ada.reshape(L, N_MOD, 1, D))


def _s5_discretise(a_re, a_im, log_dt):
    ar = jnp.minimum(a_re, S5_A_RE_MAX)
    dt = jnp.exp(log_dt)
    mag = jnp.exp(ar * dt)
    return ar, a_im, mag * jnp.cos(a_im * dt), mag * jnp.sin(a_im * dt)


def _s5prep_kernel(are_c, aim_c, ldt_c, bre_ref, bim_ref, are_r, aim_r, ldt_r,
                   bbr_ref, bbi_ref, abr_ref, abi_ref):
    ar, ai, abr, abi = _s5_discretise(are_c[...], aim_c[...], ldt_c[...])
    den = ar * ar + ai * ai
    f_r = ((abr - 1.0) * ar + abi * ai) / den
    f_i = (abi * ar - (abr - 1.0) * ai) / den
    b_r = bre_ref[...]
    b_i = bim_ref[...]
    bbr_ref[...] = f_r * b_r - f_i * b_i
    bbi_ref[...] = f_r * b_i + f_i * b_r
    _, _, abr_row, abi_row = _s5_discretise(are_r[...], aim_r[...], ldt_r[...])
    abr_ref[...] = abr_row
    abi_ref[...] = abi_row


def _s5prep_call(a_re, a_im, log_dt, b_re, b_im):
    L, G, N = a_re.shape
    J = b_re.shape[-1]
    GN = G * N
    ldt = jnp.broadcast_to(log_dt[:, :, None], (L, G, N))
    col = lambda v: v.reshape(L, GN, 1)
    row = lambda v: v.reshape(L, 1, GN)
    cspec = pl.BlockSpec((None, GN, 1), lambda l: (l, 0, 0))
    rspec = pl.BlockSpec((None, 1, GN), lambda l: (l, 0, 0))
    bspec = pl.BlockSpec((None, GN, J), lambda l: (l, 0, 0))
    return pl.pallas_call(
        _s5prep_kernel,
        grid=(L,),
        in_specs=[cspec, cspec, cspec, bspec, bspec, rspec, rspec, rspec],
        out_specs=[bspec, bspec, rspec, rspec],
        out_shape=[jax.ShapeDtypeStruct((L, GN, J), F32), jax.ShapeDtypeStruct((L, GN, J), F32),
                   jax.ShapeDtypeStruct((L, 1, GN), F32), jax.ShapeDtypeStruct((L, 1, GN), F32)],
        compiler_params=_params(("arbitrary",)),
        name="s5_discretise",
    )(col(a_re), col(a_im), col(ldt), b_re.reshape(L, GN, J), b_im.reshape(L, GN, J),
      row(a_re), row(a_im), row(ldt))


def _ffn_kernel(x_ref, mod_ref, g_ref, w1_ref, w3_ref, w2_ref, gf_ref, o_ref, *, nb, final_norm):
    sub = min(FFN_SUB_ROWS, x_ref.shape[0])
    for r0 in range(0, x_ref.shape[0], sub):
        x = x_ref[r0:r0 + sub, :]
        y = _rmsnorm(x, g_ref[...])
        h = (_rows3(y, nb) * (1.0 + mod_ref[1][None]) + mod_ref[0][None]).reshape(x.shape)
        h = h.astype(BF16)
        a = jnp.dot(h, w1_ref[...], preferred_element_type=F32)
        b = jnp.dot(h, w3_ref[...], preferred_element_type=F32)
        act = (a * jax.nn.sigmoid(a) * b).astype(BF16)
        f = jnp.dot(act, w2_ref[...], preferred_element_type=F32)
        out = (_rows3(x, nb) + (FFN_RES * mod_ref[2])[None] * _rows3(f, nb)).reshape(x.shape)
        if final_norm:
            out = _rmsnorm(out, gf_ref[...])
        o_ref[r0:r0 + sub, :] = out


def _ffn_call(x, mod, layer, mod_group, mod_rowblk, nb, g, w1, w3, w2, gf, final_norm, name):
    rows, D = x.shape
    F = w1.shape[-1]
    tm = min(FFN_ROWS, rows)
    assert rows % tm == 0 and tm % nb == 0
    return pl.pallas_call(
        functools.partial(_ffn_kernel, nb=nb, final_norm=final_norm),
        grid=(rows // tm,),
        in_specs=[
            pl.BlockSpec((tm, D), lambda i: (i, 0)),
            _const_spec((None, 3, nb, D), (layer, mod_group, mod_rowblk, 0)),
            _const_spec((None, 1, D), (layer, 0, 0)),
            _const_spec((None, D, F), (layer, 0, 0)),
            _const_spec((None, D, F), (layer, 0, 0)),
            _const_spec((None, F, D), (layer, 0, 0)),
            _const_spec((1, D), (0, 0)),
        ],
        out_specs=pl.BlockSpec((tm, D), lambda i: (i, 0)),
        out_shape=jax.ShapeDtypeStruct((rows, D), F32),
        compiler_params=_params(("arbitrary",)),
        name=name,
    )(x, mod, g, w1, w3, w2, gf)


def _softplus(v):
    return jnp.maximum(v, 0.0) + jnp.log1p(jnp.exp(-jnp.abs(v)))


def _mixer_kernel(x_ref, mod_ref, gn_ref, conv0_ref, h0_ref, s0_ref,
                  win_ref, convw_ref, convb_ref, wg_ref, bg_ref, lam_ref,
                  abar_ref, bb_ref, cb_ref, d_ref, wglu_ref, bglu_ref, wout_ref,
                  xo_ref, convo_ref, ho_ref, so_ref,
                  xp_scr, a_scr, b_scr, s_scr, *, nb):
    rows, D = x_ref.shape
    tt = rows // nb
    DL = ho_ref.shape[-1]
    DS = d_ref.shape[-1]
    HS = DS // 2
    HN = so_ref.shape[-1] // 2

    @pl.when(pl.program_id(0) == 0)
    def _():
        convo_ref[...] = conv0_ref[...]
        ho_ref[...] = h0_ref[...]
        so_ref[...] = s0_ref[...]

    x = x_ref[...]
    y = _rmsnorm(x, gn_ref[...])
    h = (_rows3(y, nb) * (1.0 + mod_ref[1][None]) + mod_ref[0][None]).reshape(rows, D)
    z = jnp.dot(h.astype(BF16), win_ref[...], preferred_element_type=F32)
    xb = z[:, :DL]
    yb = z[:, DL:2 * DL]
    u = z[:, 2 * DL:]

    halo = (CONV_W - 1) * nb
    xp_scr[0:halo, :] = convo_ref[...]
    xp_scr[halo:halo + rows, :] = xb
    cw = convw_ref[...]
    xc = convb_ref[...] + sum(xp_scr[k * nb:k * nb + rows, :] * cw[k:k + 1, :]
                              for k in range(CONV_W))
    convo_ref[...] = xp_scr[rows:rows + halo, :]

    gpre = jnp.dot(xc.astype(BF16), wg_ref[...], preferred_element_type=F32) + bg_ref[...]
    r = jax.nn.sigmoid(gpre[:, :DL])
    ig = jax.nn.sigmoid(gpre[:, DL:])
    log_a = -C_GATE * r * _softplus(-lam_ref[...])
    a = jnp.exp(log_a)
    mult = jnp.sqrt(-jnp.tanh(log_a) * (a * a + 1.0))
    a_scr[...] = a
    b_scr[...] = mult * ig * xc

    for r0 in range(0, nb, SCAN_ROWS):
        hcur = ho_ref[r0:r0 + SCAN_ROWS, :]
        for t in range(tt):
            sl = slice(t * nb + r0, t * nb + r0 + SCAN_ROWS)
            hcur = a_scr[sl, :] * hcur + b_scr[sl, :]
            b_scr[sl, :] = hcur
        ho_ref[r0:r0 + SCAN_ROWS, :] = hcur
    y_lru = jax.nn.gelu(yb) * b_scr[...]

    ub = u.astype(BF16)
    for hf in range(2):
        s_scr[hf] = jnp.dot(ub[:, hf * HS:(hf + 1) * HS], bb_ref[hf], preferred_element_type=F32)

    for hf in range(2):
        for c0 in range(0, HN, SCAN_LANES):
            re_l = slice(c0, c0 + SCAN_LANES)
            im_l = slice(HN + c0, HN + c0 + SCAN_LANES)
            a_r = jnp.broadcast_to(abar_ref[hf, :, re_l], (SCAN_ROWS, SCAN_LANES))
            a_i = jnp.broadcast_to(abar_ref[hf, :, im_l], (SCAN_ROWS, SCAN_LANES))
            for r0 in range(0, nb, SCAN_ROWS):
                s_re = so_ref[hf, r0:r0 + SCAN_ROWS, re_l]
                s_im = so_ref[hf, r0:r0 + SCAN_ROWS, im_l]
                for t in range(tt):
                    sl = slice(t * nb + r0, t * nb + r0 + SCAN_ROWS)
                    n_re = a_r * s_re - a_i * s_im + s_scr[hf, sl, re_l]
                    n_im = a_r * s_im + a_i * s_re + s_scr[hf, sl, im_l]
                    s_re, s_im = n_re, n_im
                    s_scr[hf, sl, re_l] = s_re
                    s_scr[hf, sl, im_l] = s_im
                so_ref[hf, r0:r0 + SCAN_ROWS, re_l] = s_re
                so_ref[hf, r0:r0 + SCAN_ROWS, im_l] = s_im

    ys = []
    for hf in range(2):
        sb = s_scr[hf].astype(BF16)
        ys.append(jnp.dot(sb[:, :HN], cb_ref[hf, 0], preferred_element_type=F32)
                  - jnp.dot(sb[:, HN:], cb_ref[hf, 1], preferred_element_type=F32))
    ys = jnp.concatenate(ys, axis=-1) + d_ref[...] * u
    g = jax.nn.gelu(ys)
    y_s5 = g * jax.nn.sigmoid(
        jnp.dot(g.astype(BF16), wglu_ref[...], preferred_element_type=F32) + bglu_ref[...])

    ycat = jnp.concatenate([y_lru, y_s5], axis=-1).astype(BF16)
    out = jnp.dot(ycat, wout_ref[...], preferred_element_type=F32)
    xo_ref[...] = (_rows3(x, nb) + mod_ref[2][None] * _rows3(out, nb)).reshape(rows, D)


def _mixer_call(x, mod, layer, mod_rowblk, nb, conv0, h0, s0, wts, name):
    rows, D = x.shape
    tm = min(MIXER_ROWS, rows)
    assert rows % tm == 0 and tm % nb == 0
    DL = h0.shape[-1]
    NS2 = s0.shape[-1]
    halo = (CONV_W - 1) * nb

    def wspec(w, per_layer=True):
        if per_layer:
            return _const_spec((None,) + w.shape[1:], (layer,) + (0,) * (w.ndim - 1))
        return _const_spec(w.shape, (0,) * w.ndim)

    state_specs = [
        _const_spec((None, halo, DL), (layer, 0, 0)),
        _const_spec((None, nb, DL), (layer, 0, 0)),
        _const_spec((None, 2, nb, NS2), (layer, 0, 0, 0)),
    ]
    out_state_specs = [
        pl.BlockSpec((halo, DL), lambda i: (0, 0)),
        pl.BlockSpec((nb, DL), lambda i: (0, 0)),
        pl.BlockSpec((2, nb, NS2), lambda i: (0, 0, 0)),
    ]
    return pl.pallas_call(
        functools.partial(_mixer_kernel, nb=nb),
        grid=(rows // tm,),
        in_specs=[pl.BlockSpec((tm, D), lambda i: (i, 0)),
                  _const_spec((None, 3, nb, D), (layer, 1, mod_rowblk, 0))]
                 + [wspec(wts[0])] + state_specs + [wspec(w) for w in wts[1:]],
        out_specs=[pl.BlockSpec((tm, D), lambda i: (i, 0))] + out_state_specs,
        out_shape=[jax.ShapeDtypeStruct((rows, D), F32),
                   jax.ShapeDtypeStruct((halo, DL), F32),
                   jax.ShapeDtypeStruct((nb, DL), F32),
                   jax.ShapeDtypeStruct((2, nb, NS2), F32)],
        scratch_shapes=[pltpu.VMEM((tm + halo, DL), F32),
                        pltpu.VMEM((tm, DL), F32),
                        pltpu.VMEM((tm, DL), F32),
                        pltpu.VMEM((2, tm, NS2), F32)],
        compiler_params=_params(("arbitrary",)),
        name=name,
    )(x, mod, wts[0], conv0, h0, s0, *wts[1:])


def _block_diag(w):
    *lead, nblk, r, c = w.shape
    eye = jnp.eye(nblk, dtype=bool)[:, None, :, None]
    full = jnp.where(eye, w[..., :, :, None, :], jnp.zeros((), w.dtype))
    return full.reshape(*lead, nblk * r, nblk * c)


def _halves(re, im):
    n = re.shape[-1] // 2
    return jnp.stack([jnp.concatenate([re[..., :n], im[..., :n]], axis=-1),
                      jnp.concatenate([re[..., n:], im[..., n:]], axis=-1)], axis=-2)


def _unhalve(s):
    n = s.shape[-1] // 2
    return (jnp.concatenate([s[0, :, :n], s[1, :, :n]], axis=-1),
            jnp.concatenate([s[0, :, n:], s[1, :, n:]], axis=-1))


def kernel(x_prompt, x_sample, c_prompt, c_sample, state_lru_conv, state_lru_h, state_s5_re, state_s5_im, w_ada, b_ada, norm_ffn1, w1_ffn1, w3_ffn1, w2_ffn1, norm_mix, w_in, conv_w, conv_b, w_rg, b_rg, w_ig, b_ig, lru_lambda, s5_a_re, s5_a_im, s5_log_dt, s5_b_re, s5_b_im, s5_c_re, s5_c_im, s5_d, w_glu, b_glu, w_out, norm_ffn2, w1_ffn2, w3_ffn2, w2_ffn2, norm_final):
    L, D, _ = w_ada.shape
    Bp, Tp, _ = x_prompt.shape
    Bs, Ts, _ = x_sample.shape
    _, G, N = s5_a_re.shape
    J = s5_b_re.shape[-1]
    DL = lru_lambda.shape[-1]
    GN = G * N
    assert Bs % Bp == 0 and Bp % 8 == 0 and G % 2 == 0

    mod = _ada_call(jnp.concatenate([c_sample, c_prompt], axis=0), w_ada, b_ada)

    bb_r, bb_i, ab_r, ab_i = _s5prep_call(s5_a_re, s5_a_im, s5_log_dt, s5_b_re, s5_b_im)
    abar = _halves(ab_r, ab_i)
    abar = jnp.swapaxes(abar, 1, 2)

    def in_blocks(bb):
        t = bb.reshape(L, 2, G // 2, N, J).swapaxes(-1, -2)
        return _block_diag(t)

    bmat = jnp.concatenate([in_blocks(bb_r), in_blocks(bb_i)], axis=-1).astype(BF16)

    def out_blocks(c):
        t = c.reshape(L, 2, G // 2, J, N).swapaxes(-1, -2)
        return _block_diag(t)

    cmat = jnp.stack([out_blocks(s5_c_re), out_blocks(s5_c_im)], axis=2).astype(BF16)
    wgate = jnp.concatenate([_block_diag(w_rg), _block_diag(w_ig)], axis=-1).astype(BF16)
    bgate = jnp.concatenate([b_rg, b_ig], axis=-1).reshape(L, 1, 2 * DL)

    r3 = lambda v: v.reshape(L, 1, v.shape[-1])
    mixer_wts = [r3(norm_mix), w_in.astype(BF16), conv_w, r3(conv_b), wgate, bgate,
                 r3(lru_lambda), abar, bmat, cmat, r3(s5_d), w_glu.astype(BF16), r3(b_glu),
                 w_out.astype(BF16)]
    ffn1_wts = (r3(norm_ffn1), w1_ffn1.astype(BF16), w3_ffn1.astype(BF16), w2_ffn1.astype(BF16))
    ffn2_wts = (r3(norm_ffn2), w1_ffn2.astype(BF16), w3_ffn2.astype(BF16), w2_ffn2.astype(BF16))
    gfin = norm_final.reshape(1, D)

    def run(x, nb, mod_rowblk, conv0, h0, s0, tag):
        T = x.shape[1]
        xt = jnp.swapaxes(x, 0, 1).reshape(T * nb, D)
        convs, hs, srs, sis = [], [], [], []
        for l in range(L):
            xt = _ffn_call(xt, mod, l, 0, mod_rowblk, nb, *ffn1_wts, gfin, False,
                           f"ffn1_{tag}{l}")
            xt, cv, hh, ss = _mixer_call(xt, mod, l, mod_rowblk, nb, conv0, h0, s0, mixer_wts,
                                         f"mixer_{tag}{l}")
            xt = _ffn_call(xt, mod, l, 2, mod_rowblk, nb, *ffn2_wts, gfin, l == L - 1,
                           f"ffn2_{tag}{l}")
            sr, si = _unhalve(ss)
            convs.append(jnp.swapaxes(cv.reshape(CONV_W - 1, nb, DL), 0, 1))
            hs.append(hh)
            srs.append(sr.reshape(nb, G, N))
            sis.append(si.reshape(nb, G, N))
        y = jnp.swapaxes(xt.reshape(T, nb, D), 0, 1)
        return y, jnp.stack(convs), jnp.stack(hs), jnp.stack(srs), jnp.stack(sis)

    zeros = lambda *s: jnp.zeros(s, F32)
    p_out = run(x_prompt, Bp, Bs // Bp, zeros(L, (CONV_W - 1) * Bp, DL), zeros(L, Bp, DL),
                zeros(L, 2, Bp, GN), "p")
    conv_s = jnp.swapaxes(state_lru_conv, 1, 2).reshape(L, (CONV_W - 1) * Bs, DL)
    s0_s = _halves(state_s5_re.reshape(L, Bs, GN), state_s5_im.reshape(L, Bs, GN))
    s_out = run(x_sample, Bs, 0, conv_s, state_lru_h, jnp.swapaxes(s0_s, 1, 2), "s")
    return (p_out[0], s_out[0]) + p_out[1:] + s_out[1:]
```

```python
import functools

import jax
import jax.numpy as jnp
from jax import lax
from jax.experimental import pallas as pl
from jax.experimental.pallas import tpu as pltpu

EPS = 1e-6
C_GATE = 8.0
FFN_RES = 0.5
CONV_W = 4
N_MOD = 9
S5_A_RE_MAX = -1e-4
BF16 = jnp.bfloat16
F32 = jnp.float32

VMEM_LIMIT_BYTES = 56 * 1024 * 1024
FFN_ROWS = 1024
FFN_SUB_ROWS = 512
MIXER_ROWS = 512
SCAN_ROWS = 8
SCAN_LANES = 512


def _const_spec(shape, index):
    return pl.BlockSpec(shape, lambda i: index, pipeline_mode=pl.Buffered(1))


def _layer_spec(w, layer):
    return _const_spec((None,) + w.shape[1:], (layer,) + (0,) * (w.ndim - 1))


def _params(semantics):
    return pltpu.CompilerParams(dimension_semantics=semantics,
                                vmem_limit_bytes=VMEM_LIMIT_BYTES)


def _rmsnorm(x, g):
    return x * lax.rsqrt(jnp.mean(x * x, axis=-1, keepdims=True) + EPS) * g


def _rows3(v, nb):
    return v.reshape(v.shape[0] // nb, nb, v.shape[1])


def _ada_kernel(c_ref, w_ref, b_ref, o_ref):
    c = c_ref[...]
    s = (c * jax.nn.sigmoid(c)).astype(BF16)
    o_ref[...] = jnp.dot(s, w_ref[...].astype(BF16), preferred_element_type=F32) + b_ref[...]


def _ada_call(c_all, w_ada, b_ada):
    L, D, _ = w_ada.shape
    nseq = c_all.shape[0]
    return pl.pallas_call(
        _ada_kernel,
        grid=(L, N_MOD),
        in_specs=[
            pl.BlockSpec((nseq, D), lambda l, k: (0, 0)),
            pl.BlockSpec((None, D, D), lambda l, k: (l, 0, k)),
            pl.BlockSpec((None, None, 1, D), lambda l, k: (l, k, 0, 0)),
        ],
        out_specs=pl.BlockSpec((None, None, nseq, D), lambda l, k: (l, k, 0, 0)),
        out_shape=jax.ShapeDtypeStruct((L, N_MOD, nseq, D), F32),
        compiler_params=_params(("arbitrary", "arbitrary")),
        name="adaln_mod",
    )(c_all, w_ada, b_ada.reshape(L, N_MOD, 1, D))


def _block_diag_rows(t, nblk):
    r, width = t.shape
    c = width // nblk
    tiled = jnp.concatenate([t] * nblk, axis=0)
    row_blk = lax.broadcasted_iota(jnp.int32, tiled.shape, 0) // r
    col_blk = lax.broadcasted_iota(jnp.int32, tiled.shape, 1) // c
    return jnp.where(row_blk == col_blk, tiled, 0.0)


def _prep_kernel(are_ref, aim_ref, ldt_ref, bt_ref, ct_ref, wt_ref,
                 abr_ref, abi_ref, bmat_ref, cmat_ref, wg_ref):
    ar = jnp.minimum(are_ref[...], S5_A_RE_MAX)
    ai = aim_ref[...]
    dt = jnp.exp(ldt_ref[...])
    mag = jnp.exp(ar * dt)
    abr = mag * jnp.cos(ai * dt)
    abi = mag * jnp.sin(ai * dt)
    abr_ref[...] = abr
    abi_ref[...] = abi
    den = ar * ar + ai * ai
    f_r = ((abr - 1.0) * ar + abi * ai) / den
    f_i = (abi * ar - (abr - 1.0) * ai) / den
    b_r = bt_ref[0]
    b_i = bt_ref[1]
    bb = (f_r * b_r - f_i * b_i, f_r * b_i + f_i * b_r)
    hn = bb[0].shape[-1] // 2
    nblk = bmat_ref.shape[-2] // bb[0].shape[0]
    for part in range(2):
        for hf in range(2):
            cols = slice(hf * hn, (hf + 1) * hn)
            bmat_ref[part, hf] = _block_diag_rows(bb[part][:, cols], nblk).astype(BF16)
            cmat_ref[hf, part] = _block_diag_rows(ct_ref[part][:, cols], nblk).T.astype(BF16)
    heads = wg_ref.shape[0] // wt_ref.shape[1]
    dl = wg_ref.shape[0]
    for gate in range(2):
        wg_ref[:, gate * dl:(gate + 1) * dl] = _block_diag_rows(wt_ref[gate], heads).astype(BF16)


def _prep_call(a_re, a_im, log_dt, b_re, b_im, c_re, c_im, w_rg, w_ig):
    L, G, N = a_re.shape
    J = b_re.shape[-1]
    H, HD, _ = w_rg.shape[1:]
    GN, HN, DL = G * N, G * N // 2, H * HD
    row = lambda v: v.reshape(L, 1, GN)
    ldt = jnp.broadcast_to(log_dt[:, :, None], (L, G, N))
    bt = jnp.stack([b_re, b_im], axis=1).transpose(0, 1, 4, 2, 3).reshape(L, 2, J, GN)
    ct = jnp.stack([c_re, c_im], axis=1).transpose(0, 1, 3, 2, 4).reshape(L, 2, J, GN)
    wt = jnp.stack([w_rg, w_ig], axis=1).transpose(0, 1, 3, 2, 4).reshape(L, 2, HD, DL)
    rspec = pl.BlockSpec((None, 1, GN), lambda l: (l, 0, 0))
    jspec = pl.BlockSpec((None, 2, J, GN), lambda l: (l, 0, 0, 0))
    return pl.pallas_call(
        _prep_kernel,
        grid=(L,),
        in_specs=[rspec, rspec, rspec, jspec, jspec,
                  pl.BlockSpec((None, 2, HD, DL), lambda l: (l, 0, 0, 0))],
        out_specs=[rspec, rspec,
                   pl.BlockSpec((None, 2, 2, G // 2 * J, HN), lambda l: (l, 0, 0, 0, 0)),
                   pl.BlockSpec((None, 2, 2, HN, G // 2 * J), lambda l: (l, 0, 0, 0, 0)),
                   pl.BlockSpec((None, DL, 2 * DL), lambda l: (l, 0, 0))],
        out_shape=[jax.ShapeDtypeStruct((L, 1, GN), F32), jax.ShapeDtypeStruct((L, 1, GN), F32),
                   jax.ShapeDtypeStruct((L, 2, 2, G // 2 * J, HN), BF16),
                   jax.ShapeDtypeStruct((L, 2, 2, HN, G // 2 * J), BF16),
                   jax.ShapeDtypeStruct((L, DL, 2 * DL), BF16)],
        compiler_params=_params(("arbitrary",)),
        name="param_prep",
    )(row(a_re), row(a_im), row(ldt), bt, ct, wt)


def _ffn_rows(x_ref, mod_ref, o_ref, g_ref, w1_ref, w3_ref, w2_ref, gf_ref, nb, final_norm):
    sub = min(FFN_SUB_ROWS, x_ref.shape[0])
    for r0 in range(0, x_ref.shape[0], sub):
        x = x_ref[r0:r0 + sub, :]
        y = _rmsnorm(x, g_ref[...])
        h = (_rows3(y, nb) * (1.0 + mod_ref[1][None]) + mod_ref[0][None]).reshape(x.shape)
        h = h.astype(BF16)
        a = jnp.dot(h, w1_ref[...], preferred_element_type=F32)
        b = jnp.dot(h, w3_ref[...], preferred_element_type=F32)
        act = (a * jax.nn.sigmoid(a) * b).astype(BF16)
        f = jnp.dot(act, w2_ref[...], preferred_element_type=F32)
        out = (_rows3(x, nb) + (FFN_RES * mod_ref[2])[None] * _rows3(f, nb)).reshape(x.shape)
        if final_norm:
            out = _rmsnorm(out, gf_ref[...])
        o_ref[r0:r0 + sub, :] = out


def _ffn_kernel(xp_ref, xs_ref, modp_ref, mods_ref, g_ref, w1_ref, w3_ref, w2_ref, gf_ref,
                op_ref, os_ref, *, nbp, nbs, n_prompt, final_norm):
    wts = (g_ref, w1_ref, w3_ref, w2_ref, gf_ref)

    @pl.when(pl.program_id(0) < n_prompt)
    def _():
        _ffn_rows(xp_ref, modp_ref, op_ref, *wts, nbp, final_norm)

    @pl.when(pl.program_id(0) == n_prompt)
    def _():
        _ffn_rows(xs_ref, mods_ref, os_ref, *wts, nbs, final_norm)


def _ffn_call(xp, xs, nbp, nbs, mod, layer, mod_group, g, w1, w3, w2, gf, final_norm, name):
    rows_p, D = xp.shape
    rows_s = xs.shape[0]
    tm = min(FFN_ROWS, rows_p)
    assert rows_p % tm == 0 and tm % nbp == 0 and rows_s % nbs == 0 and nbs % nbp == 0
    n_prompt = rows_p // tm
    prompt_tile = pl.BlockSpec((tm, D), lambda i: (jnp.minimum(i, n_prompt - 1), 0))
    return pl.pallas_call(
        functools.partial(_ffn_kernel, nbp=nbp, nbs=nbs, n_prompt=n_prompt,
                          final_norm=final_norm),
        grid=(n_prompt + 1,),
        in_specs=[
            prompt_tile,
            _const_spec((rows_s, D), (0, 0)),
            _const_spec((None, 3, nbp, D), (layer, mod_group, nbs // nbp, 0)),
            _const_spec((None, 3, nbs, D), (layer, mod_group, 0, 0)),
            _layer_spec(g, layer), _layer_spec(w1, layer), _layer_spec(w3, layer),
            _layer_spec(w2, layer),
            _const_spec((1, D), (0, 0)),
        ],
        out_specs=[prompt_tile, pl.BlockSpec((rows_s, D), lambda i: (0, 0))],
        out_shape=[jax.ShapeDtypeStruct((rows_p, D), F32),
                   jax.ShapeDtypeStruct((rows_s, D), F32)],
        compiler_params=_params(("arbitrary",)),
        name=name,
    )(xp, xs, mod, mod, g, w1, w3, w2, gf)


def _softplus(v):
    return jnp.maximum(v, 0.0) + jnp.log1p(jnp.exp(-jnp.abs(v)))


def _mixer_kernel(x_ref, mod_ref, conv0_ref, h0_ref, sre0_ref, sim0_ref,
                  convp_ref, hp_ref, srep_ref, simp_ref,
                  gn_ref, win_ref, convw_ref, convb_ref, wg_ref, bg_ref, lam_ref,
                  abr_ref, abi_ref, bmat_ref, cmat_ref, d_ref, wglu_ref, bglu_ref, wout_ref,
                  xo_ref, convo_ref, ho_ref, sreo_ref, simo_ref,
                  xp_scr, a_scr, b_scr, s_scr, *, nb):
    del convp_ref, hp_ref, srep_ref, simp_ref
    rows, D = x_ref.shape
    tt = rows // nb
    DL = ho_ref.shape[-1]
    HS = d_ref.shape[-1] // 2
    HN = sreo_ref.shape[-1] // 2

    @pl.when(pl.program_id(0) == 0)
    def _():
        convo_ref[...] = conv0_ref[...]
        ho_ref[...] = h0_ref[...]
        sreo_ref[...] = sre0_ref[...]
        simo_ref[...] = sim0_ref[...]

    x = x_ref[...]
    y = _rmsnorm(x, gn_ref[...])
    h = (_rows3(y, nb) * (1.0 + mod_ref[1][None]) + mod_ref[0][None]).reshape(rows, D)
    z = jnp.dot(h.astype(BF16), win_ref[...], preferred_element_type=F32)
    xb = z[:, :DL]
    yb = z[:, DL:2 * DL]
    u = z[:, 2 * DL:]

    halo = (CONV_W - 1) * nb
    xp_scr[0:halo, :] = convo_ref[...]
    xp_scr[halo:halo + rows, :] = xb
    cw = convw_ref[...]
    xc = convb_ref[...] + sum(xp_scr[k * nb:k * nb + rows, :] * cw[k:k + 1, :]
                              for k in range(CONV_W))
    convo_ref[...] = xp_scr[rows:rows + halo, :]

    gpre = jnp.dot(xc.astype(BF16), wg_ref[...], preferred_element_type=F32) + bg_ref[...]
    r = jax.nn.sigmoid(gpre[:, :DL])
    ig = jax.nn.sigmoid(gpre[:, DL:])
    log_a = -C_GATE * r * _softplus(-lam_ref[...])
    a = jnp.exp(log_a)
    mult = jnp.sqrt(-jnp.tanh(log_a) * (a * a + 1.0))
    a_scr[...] = a
    b_scr[...] = mult * ig * xc

    for r0 in range(0, nb, SCAN_ROWS):
        hcur = ho_ref[r0:r0 + SCAN_ROWS, :]
        for t in range(tt):
            sl = slice(t * nb + r0, t * nb + r0 + SCAN_ROWS)
            hcur = a_scr[sl, :] * hcur + b_scr[sl, :]
            b_scr[sl, :] = hcur
        ho_ref[r0:r0 + SCAN_ROWS, :] = hcur
    y_lru = jax.nn.gelu(yb) * b_scr[...]

    ub = u.astype(BF16)
    for hf in range(2):
        for part in range(2):
            s_scr[hf, :, part * HN:(part + 1) * HN] = jnp.dot(
                ub[:, hf * HS:(hf + 1) * HS], bmat_ref[part, hf], preferred_element_type=F32)

    for hf in range(2):
        for c0 in range(0, HN, SCAN_LANES):
            st_l = slice(hf * HN + c0, hf * HN + c0 + SCAN_LANES)
            re_l = slice(c0, c0 + SCAN_LANES)
            im_l = slice(HN + c0, HN + c0 + SCAN_LANES)
            a_r = jnp.broadcast_to(abr_ref[:, st_l], (SCAN_ROWS, SCAN_LANES))
            a_i = jnp.broadcast_to(abi_ref[:, st_l], (SCAN_ROWS, SCAN_LANES))
            for r0 in range(0, nb, SCAN_ROWS):
                s_re = sreo_ref[r0:r0 + SCAN_ROWS, st_l]
                s_im = simo_ref[r0:r0 + SCAN_ROWS, st_l]
                for t in range(tt):
                    sl = slice(t * nb + r0, t * nb + r0 + SCAN_ROWS)
                    n_re = a_r * s_re - a_i * s_im + s_scr[hf, sl, re_l]
                    n_im = a_r * s_im + a_i * s_re + s_scr[hf, sl, im_l]
                    s_re, s_im = n_re, n_im
                    s_scr[hf, sl, re_l] = s_re
                    s_scr[hf, sl, im_l] = s_im
                sreo_ref[r0:r0 + SCAN_ROWS, st_l] = s_re
                simo_ref[r0:r0 + SCAN_ROWS, st_l] = s_im

    ys = []
    for hf in range(2):
        sb = s_scr[hf].astype(BF16)
        ys.append(jnp.dot(sb[:, :HN], cmat_ref[hf, 0], preferred_element_type=F32)
                  - jnp.dot(sb[:, HN:], cmat_ref[hf, 1], preferred_element_type=F32))
    ys = jnp.concatenate(ys, axis=-1) + d_ref[...] * u
    g = jax.nn.gelu(ys)
    y_s5 = g * jax.nn.sigmoid(
        jnp.dot(g.astype(BF16), wglu_ref[...], preferred_element_type=F32) + bglu_ref[...])

    ycat = jnp.concatenate([y_lru, y_s5], axis=-1).astype(BF16)
    out = jnp.dot(ycat, wout_ref[...], preferred_element_type=F32)
    xo_ref[...] = (_rows3(x, nb) + mod_ref[2][None] * _rows3(out, nb)).reshape(rows, D)


def _mixer_call(x, mod, layer, mod_rowblk, nb, init_states, prev_states, wts, name):
    rows, D = x.shape
    tm = min(MIXER_ROWS, rows)
    assert rows % tm == 0 and tm % nb == 0
    L, _, DL = init_states[1].shape
    GN = init_states[2].shape[-1]
    halo = (CONV_W - 1) * nb
    assert tm // nb >= CONV_W - 1

    n_lead = 2 + len(init_states)
    state_out_specs = [pl.BlockSpec((None,) + s.shape[1:], lambda i: (layer, 0, 0))
                       for s in prev_states]
    outs = pl.pallas_call(
        functools.partial(_mixer_kernel, nb=nb),
        grid=(rows // tm,),
        in_specs=[pl.BlockSpec((tm, D), lambda i: (i, 0)),
                  _const_spec((None, 3, nb, D), (layer, 1, mod_rowblk, 0))]
                 + [_layer_spec(s, layer) for s in init_states]
                 + [pl.BlockSpec(memory_space=pl.ANY)] * len(prev_states)
                 + [_layer_spec(w, layer) for w in wts],
        out_specs=[pl.BlockSpec((tm, D), lambda i: (i, 0))] + state_out_specs,
        out_shape=[jax.ShapeDtypeStruct((rows, D), F32)]
                  + [jax.ShapeDtypeStruct(s.shape, F32) for s in prev_states],
        input_output_aliases={n_lead + k: 1 + k for k in range(len(prev_states))},
        scratch_shapes=[pltpu.VMEM((tm + halo, DL), F32),
                        pltpu.VMEM((tm, DL), F32),
                        pltpu.VMEM((tm, DL), F32),
                        pltpu.VMEM((2, tm, GN), F32)],
        compiler_params=_params(("arbitrary",)),
        name=name,
    )(x, mod, *init_states, *prev_states, *wts)
    return outs[0], tuple(outs[1:])


def kernel(x_prompt, x_sample, c_prompt, c_sample, state_lru_conv, state_lru_h, state_s5_re, state_s5_im, w_ada, b_ada, norm_ffn1, w1_ffn1, w3_ffn1, w2_ffn1, norm_mix, w_in, conv_w, conv_b, w_rg, b_rg, w_ig, b_ig, lru_lambda, s5_a_re, s5_a_im, s5_log_dt, s5_b_re, s5_b_im, s5_c_re, s5_c_im, s5_d, w_glu, b_glu, w_out, norm_ffn2, w1_ffn2, w3_ffn2, w2_ffn2, norm_final):
    L, D, _ = w_ada.shape
    Bp, Tp, _ = x_prompt.shape
    Bs, Ts, _ = x_sample.shape
    _, G, N = s5_a_re.shape
    DL = lru_lambda.shape[-1]
    GN = G * N
    halo = CONV_W - 1
    assert Bs % Bp == 0 and Bp % SCAN_ROWS == 0 and G % 2 == 0

    mod = _ada_call(jnp.concatenate([c_sample, c_prompt], axis=0), w_ada, b_ada)
    abr, abi, bmat, cmat, wgate = _prep_call(s5_a_re, s5_a_im, s5_log_dt, s5_b_re, s5_b_im,
                                             s5_c_re, s5_c_im, w_rg, w_ig)

    r3 = lambda v: v.reshape(L, 1, v.shape[-1])
    bgate = jnp.concatenate([b_rg, b_ig], axis=-1).reshape(L, 1, 2 * DL)
    mixer_wts = [r3(norm_mix), w_in.astype(BF16), conv_w, r3(conv_b), wgate, bgate,
                 r3(lru_lambda), abr, abi, bmat, cmat, r3(s5_d), w_glu.astype(BF16), r3(b_glu),
                 w_out.astype(BF16)]
    ffn1_wts = (r3(norm_ffn1), w1_ffn1.astype(BF16), w3_ffn1.astype(BF16), w2_ffn1.astype(BF16))
    ffn2_wts = (r3(norm_ffn2), w1_ffn2.astype(BF16), w3_ffn2.astype(BF16), w2_ffn2.astype(BF16))
    gfin = norm_final.reshape(1, D)

    def time_major(x):
        return jnp.swapaxes(x, 0, 1).reshape(x.shape[0] * x.shape[1], D)

    def state_shapes(nb):
        return ((L, halo * nb, DL), (L, nb, DL), (L, nb, GN), (L, nb, GN))

    init_p = tuple(jnp.zeros(s, F32) for s in state_shapes(Bp))
    init_s = (jnp.swapaxes(state_lru_conv, 1, 2).reshape(L, halo * Bs, DL), state_lru_h,
              state_s5_re.reshape(L, Bs, GN), state_s5_im.reshape(L, Bs, GN))
    st_p = tuple(jnp.zeros(s, F32) for s in state_shapes(Bp))
    st_s = tuple(jnp.zeros(s, F32) for s in state_shapes(Bs))
    xp, xs = time_major(x_prompt), time_major(x_sample)
    for l in range(L):
        xp, xs = _ffn_call(xp, xs, Bp, Bs, mod, l, 0, *ffn1_wts, gfin, False, f"ffn1_{l}")
        xp, st_p = _mixer_call(xp, mod, l, Bs // Bp, Bp, init_p, st_p, mixer_wts, f"mixer_p{l}")
        xs, st_s = _mixer_call(xs, mod, l, 0, Bs, init_s, st_s, mixer_wts, f"mixer_s{l}")
        xp, xs = _ffn_call(xp, xs, Bp, Bs, mod, l, 2, *ffn2_wts, gfin, l == L - 1, f"ffn2_{l}")

    def finish(x, T, nb, st):
        conv, hh, sre, sim = st
        return (jnp.swapaxes(x.reshape(T, nb, D), 0, 1),
                jnp.swapaxes(conv.reshape(L, halo, nb, DL), 1, 2), hh,
                sre.reshape(L, nb, G, N), sim.reshape(L, nb, G, N))

    p_out = finish(xp, Tp, Bp, st_p)
    s_out = finish(xs, Ts, Bs, st_s)
    return (p_out[0], s_out[0]) + p_out[1:] + s_out[1:]
```

```python
import functools

import jax
import jax.numpy as jnp
from jax import lax
from jax.experimental import pallas as pl
from jax.experimental.pallas import tpu as pltpu

EPS = 1e-6
C_GATE = 8.0
FFN_RES = 0.5
CONV_W = 4
N_MOD = 9
S5_A_RE_MAX = -1e-4
BF16 = jnp.bfloat16
F32 = jnp.float32

VMEM_LIMIT_BYTES = 56 * 1024 * 1024
FFN_ROWS = 1024
FFN_SUB_ROWS = 512
MIXER_ROWS = 512
SCAN_ROWS = 8
SCAN_LANES = 512
LANES = 128


def _const_spec(shape, index):
    return pl.BlockSpec(shape, lambda i: index, pipeline_mode=pl.Buffered(1))


def _layer_spec(w, layer):
    return _const_spec((None,) + w.shape[1:], (layer,) + (0,) * (w.ndim - 1))


def _params(semantics):
    return pltpu.CompilerParams(dimension_semantics=semantics,
                                vmem_limit_bytes=VMEM_LIMIT_BYTES)


def _rmsnorm(x, g):
    return x * lax.rsqrt(jnp.mean(x * x, axis=-1, keepdims=True) + EPS) * g


def _rows3(v, nb):
    return v.reshape(v.shape[0] // nb, nb, v.shape[1])


def _ada_kernel(c_ref, w_ref, b_ref, o_ref):
    c = c_ref[...]
    s = (c * jax.nn.sigmoid(c)).astype(BF16)
    o_ref[...] = jnp.dot(s, w_ref[...].astype(BF16), preferred_element_type=F32) + b_ref[...]


def _ada_call(c_all, w_ada, b_ada):
    L, D, _ = w_ada.shape
    nseq = c_all.shape[0]
    return pl.pallas_call(
        _ada_kernel,
        grid=(L, N_MOD),
        in_specs=[
            pl.BlockSpec((nseq, D), lambda l, k: (0, 0)),
            pl.BlockSpec((None, D, D), lambda l, k: (l, 0, k)),
            pl.BlockSpec((None, None, 1, D), lambda l, k: (l, k, 0, 0)),
        ],
        out_specs=pl.BlockSpec((None, None, nseq, D), lambda l, k: (l, k, 0, 0)),
        out_shape=jax.ShapeDtypeStruct((L, N_MOD, nseq, D), F32),
        compiler_params=_params(("arbitrary", "arbitrary")),
        name="adaln_mod",
    )(c_all, w_ada, b_ada.reshape(L, N_MOD, 1, D))


def _block_diag_rows(t, nblk):
    r, width = t.shape
    c = width // nblk
    tiled = jnp.concatenate([t] * nblk, axis=0)
    row_blk = lax.broadcasted_iota(jnp.int32, tiled.shape, 0) // r
    col_blk = lax.broadcasted_iota(jnp.int32, tiled.shape, 1) // c
    return jnp.where(row_blk == col_blk, tiled, 0.0)


def _prep_kernel(are_ref, aim_ref, ldt_ref, bt_ref, ct_ref, wt_ref,
                 abr_ref, abi_ref, bmat_ref, cmat_ref, wg_ref):
    ar = jnp.minimum(are_ref[...], S5_A_RE_MAX)
    ai = aim_ref[...]
    dt = jnp.exp(ldt_ref[...])
    mag = jnp.exp(ar * dt)
    abr = mag * jnp.cos(ai * dt)
    abi = mag * jnp.sin(ai * dt)
    abr_ref[...] = abr
    abi_ref[...] = abi
    den = ar * ar + ai * ai
    f_r = ((abr - 1.0) * ar + abi * ai) / den
    f_i = (abi * ar - (abr - 1.0) * ai) / den
    b_r = bt_ref[0]
    b_i = bt_ref[1]
    bb = (f_r * b_r - f_i * b_i, f_r * b_i + f_i * b_r)
    hn = bb[0].shape[-1] // 2
    nblk = bmat_ref.shape[-2] // bb[0].shape[0]
    for part in range(2):
        for hf in range(2):
            cols = slice(hf * hn, (hf + 1) * hn)
            bmat_ref[part, hf] = _block_diag_rows(bb[part][:, cols], nblk).astype(BF16)
            cmat_ref[hf, part] = _block_diag_rows(ct_ref[part][:, cols], nblk).T.astype(BF16)
    heads = wg_ref.shape[0] // wt_ref.shape[1]
    dl = wg_ref.shape[0]
    for gate in range(2):
        wg_ref[:, gate * dl:(gate + 1) * dl] = _block_diag_rows(wt_ref[gate], heads).astype(BF16)


def _prep_call(a_re, a_im, log_dt, b_re, b_im, c_re, c_im, w_rg, w_ig):
    L, G, N = a_re.shape
    J = b_re.shape[-1]
    H, HD, _ = w_rg.shape[1:]
    GN, HN, DL = G * N, G * N // 2, H * HD
    row = lambda v: v.reshape(L, 1, GN)
    ldt = jnp.broadcast_to(log_dt[:, :, None], (L, G, N))
    bt = jnp.stack([b_re, b_im], axis=1).transpose(0, 1, 4, 2, 3).reshape(L, 2, J, GN)
    ct = jnp.stack([c_re, c_im], axis=1).transpose(0, 1, 3, 2, 4).reshape(L, 2, J, GN)
    wt = jnp.stack([w_rg, w_ig], axis=1).transpose(0, 1, 3, 2, 4).reshape(L, 2, HD, DL)
    rspec = pl.BlockSpec((None, 1, GN), lambda l: (l, 0, 0))
    jspec = pl.BlockSpec((None, 2, J, GN), lambda l: (l, 0, 0, 0))
    return pl.pallas_call(
        _prep_kernel,
        grid=(L,),
        in_specs=[rspec, rspec, rspec, jspec, jspec,
                  pl.BlockSpec((None, 2, HD, DL), lambda l: (l, 0, 0, 0))],
        out_specs=[rspec, rspec,
                   pl.BlockSpec((None, 2, 2, G // 2 * J, HN), lambda l: (l, 0, 0, 0, 0)),
                   pl.BlockSpec((None, 2, 2, HN, G // 2 * J), lambda l: (l, 0, 0, 0, 0)),
                   pl.BlockSpec((None, DL, 2 * DL), lambda l: (l, 0, 0))],
        out_shape=[jax.ShapeDtypeStruct((L, 1, GN), F32), jax.ShapeDtypeStruct((L, 1, GN), F32),
                   jax.ShapeDtypeStruct((L, 2, 2, G // 2 * J, HN), BF16),
                   jax.ShapeDtypeStruct((L, 2, 2, HN, G // 2 * J), BF16),
                   jax.ShapeDtypeStruct((L, DL, 2 * DL), BF16)],
        compiler_params=_params(("arbitrary",)),
        name="param_prep",
    )(row(a_re), row(a_im), row(ldt), bt, ct, wt)


def _ffn_tile(x, inner, shift, scale, gate, g_ref, w1_ref, w3_ref, w2_ref, gf_ref, final_norm):
    rows3 = lambda v: v.reshape(v.shape[0] // inner, inner, v.shape[1])
    y = _rmsnorm(x, g_ref[...])
    h = (rows3(y) * (1.0 + scale) + shift).reshape(x.shape).astype(BF16)
    a = jnp.dot(h, w1_ref[...], preferred_element_type=F32)
    b = jnp.dot(h, w3_ref[...], preferred_element_type=F32)
    act = (a * jax.nn.sigmoid(a) * b).astype(BF16)
    f = jnp.dot(act, w2_ref[...], preferred_element_type=F32)
    out = (rows3(x) + (FFN_RES * gate) * rows3(f)).reshape(x.shape)
    if final_norm:
        out = _rmsnorm(out, gf_ref[...])
    return out


def _ffn_rows(x_ref, mod_ref, o_ref, slab_scr, wts, nb, final_norm, layout):
    D = x_ref.shape[-1]
    rows = o_ref.shape[0] * o_ref.shape[1] if layout == "ts" else o_ref.shape[0]
    steps = rows // nb
    sub = min(FFN_SUB_ROWS, rows)
    n_slab = D // LANES
    for r0 in range(0, rows, sub):
        if layout == "st":
            b0, nseq = r0 // steps, sub // steps
            x = x_ref[b0:b0 + nseq].reshape(sub, D)
            m = [mod_ref[k, b0:b0 + nseq][:, None, :] for k in range(3)]
            inner = steps
        else:
            x = x_ref[r0:r0 + sub, :]
            m = [mod_ref[k][None] for k in range(3)]
            inner = nb
        out = _ffn_tile(x, inner, *m, *wts, final_norm)
        if layout == "tt":
            o_ref[r0:r0 + sub, :] = out
        elif layout == "st":
            for b in range(b0, b0 + nseq):
                seq = out[(b - b0) * steps:(b - b0 + 1) * steps]
                for s in range(n_slab):
                    slab_scr[s, pl.ds(b, steps, stride=nb), :] = seq[:, s * LANES:(s + 1) * LANES]
        else:
            for s in range(n_slab):
                slab_scr[s, r0:r0 + sub, :] = out[:, s * LANES:(s + 1) * LANES]
            t0, nt = r0 // nb, sub // nb
            for b in range(nb):
                for s in range(n_slab):
                    o_ref[b, t0:t0 + nt, s * LANES:(s + 1) * LANES] = (
                        slab_scr[s, pl.ds(r0 + b, nt, stride=nb), :])
    if layout == "st":
        for s in range(n_slab):
            o_ref[:, s * LANES:(s + 1) * LANES] = slab_scr[s]


def _ffn_kernel(xp_ref, xs_ref, modp_ref, mods_ref, g_ref, w1_ref, w3_ref, w2_ref, gf_ref,
                op_ref, os_ref, slab_scr, *, nbp, nbs, n_prompt, final_norm, prompt_layout):
    wts = (g_ref, w1_ref, w3_ref, w2_ref, gf_ref)

    @pl.when(pl.program_id(0) < n_prompt)
    def _():
        _ffn_rows(xp_ref, modp_ref, op_ref, slab_scr, wts, nbp, final_norm, prompt_layout)

    @pl.when(pl.program_id(0) == n_prompt)
    def _():
        _ffn_rows(xs_ref, mods_ref, os_ref, slab_scr, wts, nbs, final_norm, "tt")


def _ffn_call(xp, xs, nbp, nbs, mod, layer, mod_group, g, w1, w3, w2, gf, final_norm,
              prompt_layout, name):
    D = xp.shape[-1]
    rows_p = xp.size // D
    rows_s = xs.shape[0]
    tm = min(FFN_ROWS, rows_p)
    assert rows_p % tm == 0 and tm % nbp == 0 and rows_s % nbs == 0 and nbs % nbp == 0
    assert D % LANES == 0 and FFN_SUB_ROWS % (tm // nbp) == 0
    n_prompt = rows_p // tm
    tile_tm = pl.BlockSpec((tm, D), lambda i: (jnp.minimum(i, n_prompt - 1), 0))
    tile_sm = pl.BlockSpec((nbp, tm // nbp, D), lambda i: (0, jnp.minimum(i, n_prompt - 1), 0))
    shape_tm = jax.ShapeDtypeStruct((rows_p, D), F32)
    shape_sm = jax.ShapeDtypeStruct((nbp, rows_p // nbp, D), F32)
    return pl.pallas_call(
        functools.partial(_ffn_kernel, nbp=nbp, nbs=nbs, n_prompt=n_prompt,
                          final_norm=final_norm, prompt_layout=prompt_layout),
        grid=(n_prompt + 1,),
        in_specs=[
            tile_sm if prompt_layout == "st" else tile_tm,
            _const_spec((rows_s, D), (0, 0)),
            _const_spec((None, 3, nbp, D), (layer, mod_group, nbs // nbp, 0)),
            _const_spec((None, 3, nbs, D), (layer, mod_group, 0, 0)),
            _layer_spec(g, layer), _layer_spec(w1, layer), _layer_spec(w3, layer),
            _layer_spec(w2, layer),
            _const_spec((1, D), (0, 0)),
        ],
        out_specs=[tile_sm if prompt_layout == "ts" else tile_tm,
                   pl.BlockSpec((rows_s, D), lambda i: (0, 0))],
        out_shape=[shape_sm if prompt_layout == "ts" else shape_tm,
                   jax.ShapeDtypeStruct((rows_s, D), F32)],
        scratch_shapes=[pltpu.VMEM((D // LANES, tm, LANES), F32)],
        compiler_params=_params(("arbitrary",)),
        name=name,
    )(xp, xs, mod, mod, g, w1, w3, w2, gf)


def _softplus(v):
    return jnp.maximum(v, 0.0) + jnp.log1p(jnp.exp(-jnp.abs(v)))


def _mixer_kernel(x_ref, mod_ref, conv0_ref, h0_ref, sre0_ref, sim0_ref,
                  convp_ref, hp_ref, srep_ref, simp_ref,
                  gn_ref, win_ref, convw_ref, convb_ref, wg_ref, bg_ref, lam_ref,
                  abr_ref, abi_ref, bmat_ref, cmat_ref, d_ref, wglu_ref, bglu_ref, wout_ref,
                  xo_ref, convo_ref, ho_ref, sreo_ref, simo_ref,
                  xp_scr, a_scr, b_scr, s_scr, *, nb):
    del convp_ref, hp_ref, srep_ref, simp_ref
    rows, D = x_ref.shape
    tt = rows // nb
    DL = ho_ref.shape[-1]
    HS = d_ref.shape[-1] // 2
    HN = sreo_ref.shape[-1] // 2

    @pl.when(pl.program_id(0) == 0)
    def _():
        convo_ref[...] = conv0_ref[...]
        ho_ref[...] = h0_ref[...]
        sreo_ref[...] = sre0_ref[...]
        simo_ref[...] = sim0_ref[...]

    x = x_ref[...]
    y = _rmsnorm(x, gn_ref[...])
    h = (_rows3(y, nb) * (1.0 + mod_ref[1][None]) + mod_ref[0][None]).reshape(rows, D)
    z = jnp.dot(h.astype(BF16), win_ref[...], preferred_element_type=F32)
    xb = z[:, :DL]
    yb = z[:, DL:2 * DL]
    u = z[:, 2 * DL:]

    halo = (CONV_W - 1) * nb
    xp_scr[0:halo, :] = convo_ref[...]
    xp_scr[halo:halo + rows, :] = xb
    cw = convw_ref[...]
    xc = convb_ref[...] + sum(xp_scr[k * nb:k * nb + rows, :] * cw[k:k + 1, :]
                              for k in range(CONV_W))
    convo_ref[...] = xp_scr[rows:rows + halo, :]

    gpre = jnp.dot(xc.astype(BF16), wg_ref[...], preferred_element_type=F32) + bg_ref[...]
    r = jax.nn.sigmoid(gpre[:, :DL])
    ig = jax.nn.sigmoid(gpre[:, DL:])
    log_a = -C_GATE * r * _softplus(-lam_ref[...])
    a = jnp.exp(log_a)
    mult = jnp.sqrt(-jnp.tanh(log_a) * (a * a + 1.0))
    a_scr[...] = a
    b_scr[...] = mult * ig * xc

    for r0 in range(0, nb, SCAN_ROWS):
        hcur = ho_ref[r0:r0 + SCAN_ROWS, :]
        for t in range(tt):
            sl = slice(t * nb + r0, t * nb + r0 + SCAN_ROWS)
            hcur = a_scr[sl, :] * hcur + b_scr[sl, :]
            b_scr[sl, :] = hcur
        ho_ref[r0:r0 + SCAN_ROWS, :] = hcur
    y_lru = jax.nn.gelu(yb) * b_scr[...]

    ub = u.astype(BF16)
    for hf in range(2):
        for part in range(2):
            s_scr[hf, :, part * HN:(part + 1) * HN] = jnp.dot(
                ub[:, hf * HS:(hf + 1) * HS], bmat_ref[part, hf], preferred_element_type=F32)

    for hf in range(2):
        for c0 in range(0, HN, SCAN_LANES):
            st_l = slice(hf * HN + c0, hf * HN + c0 + SCAN_LANES)
            re_l = slice(c0, c0 + SCAN_LANES)
            im_l = slice(HN + c0, HN + c0 + SCAN_LANES)
            a_r = jnp.broadcast_to(abr_ref[:, st_l], (SCAN_ROWS, SCAN_LANES))
            a_i = jnp.broadcast_to(abi_ref[:, st_l], (SCAN_ROWS, SCAN_LANES))
            for r0 in range(0, nb, SCAN_ROWS):
                s_re = sreo_ref[r0:r0 + SCAN_ROWS, st_l]
                s_im = simo_ref[r0:r0 + SCAN_ROWS, st_l]
                for t in range(tt):
                    sl = slice(t * nb + r0, t * nb + r0 + SCAN_ROWS)
                    n_re = a_r * s_re - a_i * s_im + s_scr[hf, sl, re_l]
                    n_im = a_r * s_im + a_i * s_re + s_scr[hf, sl, im_l]
                    s_re, s_im = n_re, n_im
                    s_scr[hf, sl, re_l] = s_re
                    s_scr[hf, sl, im_l] = s_im
                sreo_ref[r0:r0 + SCAN_ROWS, st_l] = s_re
                simo_ref[r0:r0 + SCAN_ROWS, st_l] = s_im

    ys = []
    for hf in range(2):
        sb = s_scr[hf].astype(BF16)
        ys.append(jnp.dot(sb[:, :HN], cmat_ref[hf, 0], preferred_element_type=F32)
                  - jnp.dot(sb[:, HN:], cmat_ref[hf, 1], preferred_element_type=F32))
    ys = jnp.concatenate(ys, axis=-1) + d_ref[...] * u
    g = jax.nn.gelu(ys)
    y_s5 = g * jax.nn.sigmoid(
        jnp.dot(g.astype(BF16), wglu_ref[...], preferred_element_type=F32) + bglu_ref[...])

    ycat = jnp.concatenate([y_lru, y_s5], axis=-1).astype(BF16)
    out = jnp.dot(ycat, wout_ref[...], preferred_element_type=F32)
    xo_ref[...] = (_rows3(x, nb) + mod_ref[2][None] * _rows3(out, nb)).reshape(rows, D)


def _mixer_call(x, mod, layer, mod_rowblk, nb, init_states, prev_states, wts, name):
    rows, D = x.shape
    tm = min(MIXER_ROWS, rows)
    assert rows % tm == 0 and tm % nb == 0
    L, _, DL = init_states[1].shape
    GN = init_states[2].shape[-1]
    halo = (CONV_W - 1) * nb
    assert tm // nb >= CONV_W - 1

    n_lead = 2 + len(init_states)
    state_out_specs = [pl.BlockSpec((None,) + s.shape[1:], lambda i: (layer, 0, 0))
                       for s in prev_states]
    outs = pl.pallas_call(
        functools.partial(_mixer_kernel, nb=nb),
        grid=(rows // tm,),
        in_specs=[pl.BlockSpec((tm, D), lambda i: (i, 0)),
                  _const_spec((None, 3, nb, D), (layer, 1, mod_rowblk, 0))]
                 + [_layer_spec(s, layer) for s in init_states]
                 + [pl.BlockSpec(memory_space=pl.ANY)] * len(prev_states)
                 + [_layer_spec(w, layer) for w in wts],
        out_specs=[pl.BlockSpec((tm, D), lambda i: (i, 0))] + state_out_specs,
        out_shape=[jax.ShapeDtypeStruct((rows, D), F32)]
                  + [jax.ShapeDtypeStruct(s.shape, F32) for s in prev_states],
        input_output_aliases={n_lead + k: 1 + k for k in range(len(prev_states))},
        scratch_shapes=[pltpu.VMEM((tm + halo, DL), F32),
                        pltpu.VMEM((tm, DL), F32),
                        pltpu.VMEM((tm, DL), F32),
                        pltpu.VMEM((2, tm, GN), F32)],
        compiler_params=_params(("arbitrary",)),
        name=name,
    )(x, mod, *init_states, *prev_states, *wts)
    return outs[0], tuple(outs[1:])


def kernel(x_prompt, x_sample, c_prompt, c_sample, state_lru_conv, state_lru_h, state_s5_re, state_s5_im, w_ada, b_ada, norm_ffn1, w1_ffn1, w3_ffn1, w2_ffn1, norm_mix, w_in, conv_w, conv_b, w_rg, b_rg, w_ig, b_ig, lru_lambda, s5_a_re, s5_a_im, s5_log_dt, s5_b_re, s5_b_im, s5_c_re, s5_c_im, s5_d, w_glu, b_glu, w_out, norm_ffn2, w1_ffn2, w3_ffn2, w2_ffn2, norm_final):
    L, D, _ = w_ada.shape
    Bp, Tp, _ = x_prompt.shape
    Bs, Ts, _ = x_sample.shape
    _, G, N = s5_a_re.shape
    DL = lru_lambda.shape[-1]
    GN = G * N
    halo = CONV_W - 1
    assert Bs % Bp == 0 and Bp % SCAN_ROWS == 0 and G % 2 == 0

    mod = _ada_call(jnp.concatenate([c_sample, c_prompt], axis=0), w_ada, b_ada)
    abr, abi, bmat, cmat, wgate = _prep_call(s5_a_re, s5_a_im, s5_log_dt, s5_b_re, s5_b_im,
                                             s5_c_re, s5_c_im, w_rg, w_ig)

    r3 = lambda v: v.reshape(L, 1, v.shape[-1])
    bgate = jnp.concatenate([b_rg, b_ig], axis=-1).reshape(L, 1, 2 * DL)
    mixer_wts = [r3(norm_mix), w_in.astype(BF16), conv_w, r3(conv_b), wgate, bgate,
                 r3(lru_lambda), abr, abi, bmat, cmat, r3(s5_d), w_glu.astype(BF16), r3(b_glu),
                 w_out.astype(BF16)]
    ffn1_wts = (r3(norm_ffn1), w1_ffn1.astype(BF16), w3_ffn1.astype(BF16), w2_ffn1.astype(BF16))
    ffn2_wts = (r3(norm_ffn2), w1_ffn2.astype(BF16), w3_ffn2.astype(BF16), w2_ffn2.astype(BF16))
    gfin = norm_final.reshape(1, D)

    def time_major(x):
        return jnp.swapaxes(x, 0, 1).reshape(x.shape[0] * x.shape[1], D)

    def state_shapes(nb):
        return ((L, halo * nb, DL), (L, nb, DL), (L, nb, GN), (L, nb, GN))

    init_p = tuple(jnp.zeros(s, F32) for s in state_shapes(Bp))
    init_s = (jnp.swapaxes(state_lru_conv, 1, 2).reshape(L, halo * Bs, DL), state_lru_h,
              state_s5_re.reshape(L, Bs, GN), state_s5_im.reshape(L, Bs, GN))
    st_p = tuple(jnp.zeros(s, F32) for s in state_shapes(Bp))
    st_s = tuple(jnp.zeros(s, F32) for s in state_shapes(Bs))
    xp, xs = x_prompt, time_major(x_sample)
    for l in range(L):
        xp, xs = _ffn_call(xp, xs, Bp, Bs, mod, l, 0, *ffn1_wts, gfin, False,
                           "st" if l == 0 else "tt", f"ffn1_{l}")
        xp, st_p = _mixer_call(xp, mod, l, Bs // Bp, Bp, init_p, st_p, mixer_wts, f"mixer_p{l}")
        xs, st_s = _mixer_call(xs, mod, l, 0, Bs, init_s, st_s, mixer_wts, f"mixer_s{l}")
        xp, xs = _ffn_call(xp, xs, Bp, Bs, mod, l, 2, *ffn2_wts, gfin, l == L - 1,
                           "ts" if l == L - 1 else "tt", f"ffn2_{l}")

    def finish(y, st, nb):
        conv, hh, sre, sim = st
        return (y, jnp.swapaxes(conv.reshape(L, halo, nb, DL), 1, 2), hh,
                sre.reshape(L, nb, G, N), sim.reshape(L, nb, G, N))

    p_out = finish(xp, st_p, Bp)
    s_out = finish(jnp.swapaxes(xs.reshape(Ts, Bs, D), 0, 1), st_s, Bs)
    return (p_out[0], s_out[0]) + p_out[1:] + s_out[1:]
```

```python
import functools

import jax
import jax.numpy as jnp
from jax import lax
from jax.experimental import pallas as pl
from jax.experimental.pallas import tpu as pltpu

EPS = 1e-6
C_GATE = 8.0
FFN_RES = 0.5
CONV_W = 4
N_MOD = 9
S5_A_RE_MAX = -1e-4
BF16 = jnp.bfloat16
F32 = jnp.float32

VMEM_LIMIT_BYTES = 56 * 1024 * 1024
FFN_ROWS = 1024
FFN_SUB_ROWS = 512
MIXER_ROWS = 512
SCAN_ROWS = 8
SCAN_LANES = 512
LANES = 128
BF16_ROWS = 16


def _const_spec(shape, index):
    return pl.BlockSpec(shape, lambda i: index, pipeline_mode=pl.Buffered(1))


def _layer_spec(w, layer):
    return _const_spec((None,) + w.shape[1:], (layer,) + (0,) * (w.ndim - 1))


def _params(semantics):
    return pltpu.CompilerParams(dimension_semantics=semantics,
                                vmem_limit_bytes=VMEM_LIMIT_BYTES)


def _cast_steps(jobs, grid):
    n = grid
    while n > 1 and (grid % n or any(w.shape[1] % (BF16_ROWS * n) for w, _ in jobs)):
        n -= 1
    return n


def _cast_specs(jobs, n_steps, step_of):
    ins, outs, shapes = [], [], []
    for w, layer in jobs:
        _, R, C = w.shape
        assert R % (BF16_ROWS * n_steps) == 0
        ins.append(pl.BlockSpec((None, R // n_steps, C),
                                lambda i, layer=layer: (layer, step_of(i), 0)))
        outs.append(pl.BlockSpec((R // n_steps, C), lambda i: (step_of(i), 0)))
        shapes.append(jax.ShapeDtypeStruct((R, C), BF16))
    return ins, outs, shapes


def _cast_blocks(in_refs, out_refs):
    for src_ref, dst_ref in zip(in_refs, out_refs):
        dst_ref[...] = src_ref[...].astype(BF16)


def _rmsnorm(x, g):
    return x * lax.rsqrt(jnp.mean(x * x, axis=-1, keepdims=True) + EPS) * g


def _rows3(v, nb):
    return v.reshape(v.shape[0] // nb, nb, v.shape[1])


def _ada_kernel(c_ref, w_ref, b_ref, o_ref):
    c = c_ref[...]
    s = (c * jax.nn.sigmoid(c)).astype(BF16)
    D = c.shape[-1]
    for k in range(o_ref.shape[0]):
        w = w_ref[:, k * D:(k + 1) * D].astype(BF16)
        o_ref[k] = jnp.dot(s, w, preferred_element_type=F32) + b_ref[k]


def _ada_call(c_all, w_ada, b_ada):
    L, D, _ = w_ada.shape
    nseq = c_all.shape[0]
    n_sub = N_MOD // 3
    return pl.pallas_call(
        _ada_kernel,
        grid=(L, n_sub),
        in_specs=[
            pl.BlockSpec((nseq, D), lambda l, k: (0, 0)),
            pl.BlockSpec((None, D, 3 * D), lambda l, k: (l, 0, k)),
            pl.BlockSpec((None, 3, 1, D), lambda l, k: (l, k, 0, 0)),
        ],
        out_specs=pl.BlockSpec((None, 3, nseq, D), lambda l, k: (l, k, 0, 0)),
        out_shape=jax.ShapeDtypeStruct((L, N_MOD, nseq, D), F32),
        compiler_params=_params(("arbitrary", "arbitrary")),
        name="adaln_mod",
    )(c_all, w_ada, b_ada.reshape(L, N_MOD, 1, D))


def _block_diag_rows(t, nblk):
    r, width = t.shape
    c = width // nblk
    tiled = jnp.concatenate([t] * nblk, axis=0)
    row_blk = lax.broadcasted_iota(jnp.int32, tiled.shape, 0) // r
    col_blk = lax.broadcasted_iota(jnp.int32, tiled.shape, 1) // c
    return jnp.where(row_blk == col_blk, tiled, 0.0)


def _prep_kernel(are_ref, aim_ref, ldt_ref, bt_ref, ct_ref, wt_ref,
                 abr_ref, abi_ref, bmat_ref, cmat_ref, wg_ref):
    ar = jnp.minimum(are_ref[...], S5_A_RE_MAX)
    ai = aim_ref[...]
    dt = jnp.exp(ldt_ref[...])
    mag = jnp.exp(ar * dt)
    abr = mag * jnp.cos(ai * dt)
    abi = mag * jnp.sin(ai * dt)
    abr_ref[...] = abr
    abi_ref[...] = abi
    den = ar * ar + ai * ai
    f_r = ((abr - 1.0) * ar + abi * ai) / den
    f_i = (abi * ar - (abr - 1.0) * ai) / den
    b_r = bt_ref[0]
    b_i = bt_ref[1]
    bb = (f_r * b_r - f_i * b_i, f_r * b_i + f_i * b_r)
    hn = bb[0].shape[-1] // 2
    nblk = bmat_ref.shape[-2] // bb[0].shape[0]
    for part in range(2):
        for hf in range(2):
            cols = slice(hf * hn, (hf + 1) * hn)
            bmat_ref[part, hf] = _block_diag_rows(bb[part][:, cols], nblk).astype(BF16)
            cmat_ref[hf, part] = _block_diag_rows(ct_ref[part][:, cols], nblk).T.astype(BF16)
    heads = wg_ref.shape[0] // wt_ref.shape[1]
    dl = wg_ref.shape[0]
    for gate in range(2):
        wg_ref[:, gate * dl:(gate + 1) * dl] = _block_diag_rows(wt_ref[gate], heads).astype(BF16)


def _prep_call(a_re, a_im, log_dt, b_re, b_im, c_re, c_im, w_rg, w_ig):
    L, G, N = a_re.shape
    J = b_re.shape[-1]
    H, HD, _ = w_rg.shape[1:]
    GN, HN, DL = G * N, G * N // 2, H * HD
    row = lambda v: v.reshape(L, 1, GN)
    ldt = jnp.broadcast_to(log_dt[:, :, None], (L, G, N))
    bt = jnp.stack([b_re, b_im], axis=1).transpose(0, 1, 4, 2, 3).reshape(L, 2, J, GN)
    ct = jnp.stack([c_re, c_im], axis=1).transpose(0, 1, 3, 2, 4).reshape(L, 2, J, GN)
    wt = jnp.stack([w_rg, w_ig], axis=1).transpose(0, 1, 3, 2, 4).reshape(L, 2, HD, DL)
    rspec = pl.BlockSpec((None, 1, GN), lambda l: (l, 0, 0))
    jspec = pl.BlockSpec((None, 2, J, GN), lambda l: (l, 0, 0, 0))
    return pl.pallas_call(
        _prep_kernel,
        grid=(L,),
        in_specs=[rspec, rspec, rspec, jspec, jspec,
                  pl.BlockSpec((None, 2, HD, DL), lambda l: (l, 0, 0, 0))],
        out_specs=[rspec, rspec,
                   pl.BlockSpec((None, 2, 2, G // 2 * J, HN), lambda l: (l, 0, 0, 0, 0)),
                   pl.BlockSpec((None, 2, 2, HN, G // 2 * J), lambda l: (l, 0, 0, 0, 0)),
                   pl.BlockSpec((None, DL, 2 * DL), lambda l: (l, 0, 0))],
        out_shape=[jax.ShapeDtypeStruct((L, 1, GN), F32), jax.ShapeDtypeStruct((L, 1, GN), F32),
                   jax.ShapeDtypeStruct((L, 2, 2, G // 2 * J, HN), BF16),
                   jax.ShapeDtypeStruct((L, 2, 2, HN, G // 2 * J), BF16),
                   jax.ShapeDtypeStruct((L, DL, 2 * DL), BF16)],
        compiler_params=_params(("arbitrary",)),
        name="param_prep",
    )(row(a_re), row(a_im), row(ldt), bt, ct, wt)


def _ffn_tile(x, inner, shift, scale, gate, g_ref, w1_ref, w3_ref, w2_ref, gf_ref, final_norm):
    rows3 = lambda v: v.reshape(v.shape[0] // inner, inner, v.shape[1])
    y = _rmsnorm(x, g_ref[...])
    h = (rows3(y) * (1.0 + scale) + shift).reshape(x.shape).astype(BF16)
    a = jnp.dot(h, w1_ref[...], preferred_element_type=F32)
    b = jnp.dot(h, w3_ref[...], preferred_element_type=F32)
    act = (a * jax.nn.sigmoid(a) * b).astype(BF16)
    f = jnp.dot(act, w2_ref[...], preferred_element_type=F32)
    out = (rows3(x) + (FFN_RES * gate) * rows3(f)).reshape(x.shape)
    if final_norm:
        out = _rmsnorm(out, gf_ref[...])
    return out


def _ffn_rows(x_ref, mod_ref, o_ref, slab_scr, wts, nb, final_norm, layout):
    D = x_ref.shape[-1]
    rows = o_ref.shape[0] * o_ref.shape[1] if layout == "ts" else o_ref.shape[0]
    steps = rows // nb
    sub = min(FFN_SUB_ROWS, rows)
    n_slab = D // LANES
    for r0 in range(0, rows, sub):
        if layout == "st":
            b0, nseq = r0 // steps, sub // steps
            x = x_ref[b0:b0 + nseq].reshape(sub, D)
            m = [mod_ref[k, b0:b0 + nseq][:, None, :] for k in range(3)]
            inner = steps
        else:
            x = x_ref[r0:r0 + sub, :]
            m = [mod_ref[k][None] for k in range(3)]
            inner = nb
        out = _ffn_tile(x, inner, *m, *wts, final_norm)
        if layout == "tt":
            o_ref[r0:r0 + sub, :] = out
        elif layout == "st":
            for b in range(b0, b0 + nseq):
                seq = out[(b - b0) * steps:(b - b0 + 1) * steps]
                for s in range(n_slab):
                    slab_scr[s, pl.ds(b, steps, stride=nb), :] = seq[:, s * LANES:(s + 1) * LANES]
        else:
            for s in range(n_slab):
                slab_scr[s, r0:r0 + sub, :] = out[:, s * LANES:(s + 1) * LANES]
            t0, nt = r0 // nb, sub // nb
            for b in range(nb):
                for s in range(n_slab):
                    o_ref[b, t0:t0 + nt, s * LANES:(s + 1) * LANES] = (
                        slab_scr[s, pl.ds(r0 + b, nt, stride=nb), :])
    if layout == "st":
        for s in range(n_slab):
            o_ref[:, s * LANES:(s + 1) * LANES] = slab_scr[s]


def _ffn_kernel(*refs, n_cast, nbp, nbs, n_prompt, final_norm, prompt_layout):
    (xp_ref, xs_ref, modp_ref, mods_ref, g_ref, w1_ref, w3_ref, w2_ref, gf_ref), refs = (
        refs[:9], refs[9:])
    cast_in, (op_ref, os_ref), cast_out, slab_scr = (
        refs[:n_cast], refs[n_cast:n_cast + 2], refs[n_cast + 2:2 * n_cast + 2], refs[-1])
    wts = (g_ref, w1_ref, w3_ref, w2_ref, gf_ref)

    @pl.when(pl.program_id(0) < n_prompt)
    def _():
        _cast_blocks(cast_in, cast_out)
        _ffn_rows(xp_ref, modp_ref, op_ref, slab_scr, wts, nbp, final_norm, prompt_layout)

    @pl.when(pl.program_id(0) == n_prompt)
    def _():
        _ffn_rows(xs_ref, mods_ref, os_ref, slab_scr, wts, nbs, final_norm, "tt")


def _ffn_call(xp, xs, nbp, nbs, mod, layer, mod_group, g, w1, w3, w2, gf, final_norm,
              prompt_layout, cast_jobs, name):
    D = xp.shape[-1]
    rows_p = xp.size // D
    rows_s = xs.shape[0]
    tm = min(FFN_ROWS, rows_p)
    assert rows_p % tm == 0 and tm % nbp == 0 and rows_s % nbs == 0 and nbs % nbp == 0
    assert D % LANES == 0 and FFN_SUB_ROWS % (tm // nbp) == 0
    n_prompt = rows_p // tm
    tile_tm = pl.BlockSpec((tm, D), lambda i: (jnp.minimum(i, n_prompt - 1), 0))
    tile_sm = pl.BlockSpec((nbp, tm // nbp, D), lambda i: (0, jnp.minimum(i, n_prompt - 1), 0))
    shape_tm = jax.ShapeDtypeStruct((rows_p, D), F32)
    shape_sm = jax.ShapeDtypeStruct((nbp, rows_p // nbp, D), F32)
    n_cast_steps = _cast_steps(cast_jobs, n_prompt)
    cast_in, cast_out, cast_shapes = _cast_specs(
        cast_jobs, n_cast_steps,
        lambda i: jnp.minimum(i, n_prompt - 1) // (n_prompt // n_cast_steps))
    return pl.pallas_call(
        functools.partial(_ffn_kernel, n_cast=len(cast_jobs), nbp=nbp, nbs=nbs,
                          n_prompt=n_prompt, final_norm=final_norm, prompt_layout=prompt_layout),
        grid=(n_prompt + 1,),
        in_specs=[
            tile_sm if prompt_layout == "st" else tile_tm,
            _const_spec((rows_s, D), (0, 0)),
            _const_spec((None, 3, nbp, D), (layer, mod_group, nbs // nbp, 0)),
            _const_spec((None, 3, nbs, D), (layer, mod_group, 0, 0)),
            _layer_spec(g, layer), _const_spec(w1.shape, (0, 0)), _const_spec(w3.shape, (0, 0)),
            _const_spec(w2.shape, (0, 0)),
            _const_spec((1, D), (0, 0)),
        ] + cast_in,
        out_specs=[tile_sm if prompt_layout == "ts" else tile_tm,
                   pl.BlockSpec((rows_s, D), lambda i: (0, 0))] + cast_out,
        out_shape=[shape_sm if prompt_layout == "ts" else shape_tm,
                   jax.ShapeDtypeStruct((rows_s, D), F32)] + cast_shapes,
        scratch_shapes=[pltpu.VMEM((D // LANES, tm, LANES), F32)],
        compiler_params=_params(("arbitrary",)),
        name=name,
    )(xp, xs, mod, mod, g, w1, w3, w2, gf, *[w for w, _ in cast_jobs])


def _softplus(v):
    return jnp.maximum(v, 0.0) + jnp.log1p(jnp.exp(-jnp.abs(v)))


def _mixer_kernel(*refs, n_cast, nb):
    (x_ref, mod_ref, conv0_ref, h0_ref, sre0_ref, sim0_ref,
     _, _, _, _,
     gn_ref, win_ref, convw_ref, convb_ref, wg_ref, bg_ref, lam_ref,
     abr_ref, abi_ref, bmat_ref, cmat_ref, d_ref, wglu_ref, bglu_ref, wout_ref), refs = (
        refs[:25], refs[25:])
    cast_in, refs = refs[:n_cast], refs[n_cast:]
    (xo_ref, convo_ref, ho_ref, sreo_ref, simo_ref), refs = refs[:5], refs[5:]
    cast_out, (xp_scr, a_scr, b_scr, s_scr) = refs[:n_cast], refs[n_cast:]
    rows, D = x_ref.shape
    tt = rows // nb
    DL = ho_ref.shape[-1]
    HS = d_ref.shape[-1] // 2
    HN = sreo_ref.shape[-1] // 2

    @pl.when(pl.program_id(0) == 0)
    def _():
        convo_ref[...] = conv0_ref[...]
        ho_ref[...] = h0_ref[...]
        sreo_ref[...] = sre0_ref[...]
        simo_ref[...] = sim0_ref[...]

    _cast_blocks(cast_in, cast_out)
    x = x_ref[...]
    y = _rmsnorm(x, gn_ref[...])
    h = (_rows3(y, nb) * (1.0 + mod_ref[1][None]) + mod_ref[0][None]).reshape(rows, D)
    z = jnp.dot(h.astype(BF16), win_ref[...], preferred_element_type=F32)
    xb = z[:, :DL]
    yb = z[:, DL:2 * DL]
    u = z[:, 2 * DL:]

    halo = (CONV_W - 1) * nb
    xp_scr[0:halo, :] = convo_ref[...]
    xp_scr[halo:halo + rows, :] = xb
    cw = convw_ref[...]
    xc = convb_ref[...] + sum(xp_scr[k * nb:k * nb + rows, :] * cw[k:k + 1, :]
                              for k in range(CONV_W))
    convo_ref[...] = xp_scr[rows:rows + halo, :]

    gpre = jnp.dot(xc.astype(BF16), wg_ref[...], preferred_element_type=F32) + bg_ref[...]
    r = jax.nn.sigmoid(gpre[:, :DL])
    ig = jax.nn.sigmoid(gpre[:, DL:])
    log_a = -C_GATE * r * _softplus(-lam_ref[...])
    a = jnp.exp(log_a)
    mult = jnp.sqrt(-jnp.tanh(log_a) * (a * a + 1.0))
    a_scr[...] = a
    b_scr[...] = mult * ig * xc

    for r0 in range(0, nb, SCAN_ROWS):
        hcur = ho_ref[r0:r0 + SCAN_ROWS, :]
        for t in range(tt):
            sl = slice(t * nb + r0, t * nb + r0 + SCAN_ROWS)
            hcur = a_scr[sl, :] * hcur + b_scr[sl, :]
            b_scr[sl, :] = hcur
        ho_ref[r0:r0 + SCAN_ROWS, :] = hcur
    y_lru = jax.nn.gelu(yb) * b_scr[...]

    ub = u.astype(BF16)
    for hf in range(2):
        for part in range(2):
            s_scr[hf, :, part * HN:(part + 1) * HN] = jnp.dot(
                ub[:, hf * HS:(hf + 1) * HS], bmat_ref[part, hf], preferred_element_type=F32)

    for hf in range(2):
        for c0 in range(0, HN, SCAN_LANES):
            st_l = slice(hf * HN + c0, hf * HN + c0 + SCAN_LANES)
            re_l = slice(c0, c0 + SCAN_LANES)
            im_l = slice(HN + c0, HN + c0 + SCAN_LANES)
            a_r = jnp.broadcast_to(abr_ref[:, st_l], (SCAN_ROWS, SCAN_LANES))
            a_i = jnp.broadcast_to(abi_ref[:, st_l], (SCAN_ROWS, SCAN_LANES))
            for r0 in range(0, nb, SCAN_ROWS):
                s_re = sreo_ref[r0:r0 + SCAN_ROWS, st_l]
                s_im = simo_ref[r0:r0 + SCAN_ROWS, st_l]
                for t in range(tt):
                    sl = slice(t * nb + r0, t * nb + r0 + SCAN_ROWS)
                    n_re = a_r * s_re - a_i * s_im + s_scr[hf, sl, re_l]
                    n_im = a_r * s_im + a_i * s_re + s_scr[hf, sl, im_l]
                    s_re, s_im = n_re, n_im
                    s_scr[hf, sl, re_l] = s_re
                    s_scr[hf, sl, im_l] = s_im
                sreo_ref[r0:r0 + SCAN_ROWS, st_l] = s_re
                simo_ref[r0:r0 + SCAN_ROWS, st_l] = s_im

    ys = []
    for hf in range(2):
        sb = s_scr[hf].astype(BF16)
        ys.append(jnp.dot(sb[:, :HN], cmat_ref[hf, 0], preferred_element_type=F32)
                  - jnp.dot(sb[:, HN:], cmat_ref[hf, 1], preferred_element_type=F32))
    ys = jnp.concatenate(ys, axis=-1) + d_ref[...] * u
    g = jax.nn.gelu(ys)
    y_s5 = g * jax.nn.sigmoid(
        jnp.dot(g.astype(BF16), wglu_ref[...], preferred_element_type=F32) + bglu_ref[...])

    ycat = jnp.concatenate([y_lru, y_s5], axis=-1).astype(BF16)
    out = jnp.dot(ycat, wout_ref[...], preferred_element_type=F32)
    xo_ref[...] = (_rows3(x, nb) + mod_ref[2][None] * _rows3(out, nb)).reshape(rows, D)


def _mixer_call(x, mod, layer, mod_rowblk, nb, init_states, prev_states, wts, cast_jobs, name):
    rows, D = x.shape
    tm = min(MIXER_ROWS, rows)
    assert rows % tm == 0 and tm % nb == 0
    L, _, DL = init_states[1].shape
    GN = init_states[2].shape[-1]
    halo = (CONV_W - 1) * nb
    assert tm // nb >= CONV_W - 1

    n_lead = 2 + len(init_states)
    state_out_specs = [pl.BlockSpec((None,) + s.shape[1:], lambda i: (layer, 0, 0))
                       for s in prev_states]
    grid = rows // tm
    n_cast_steps = _cast_steps(cast_jobs, grid)
    cast_in, cast_out, cast_shapes = _cast_specs(cast_jobs, n_cast_steps,
                                                 lambda i: i // (grid // n_cast_steps))
    wt_spec = lambda w: _const_spec(w.shape, (0, 0)) if w.ndim == 2 else _layer_spec(w, layer)
    outs = pl.pallas_call(
        functools.partial(_mixer_kernel, n_cast=len(cast_jobs), nb=nb),
        grid=(grid,),
        in_specs=[pl.BlockSpec((tm, D), lambda i: (i, 0)),
                  _const_spec((None, 3, nb, D), (layer, 1, mod_rowblk, 0))]
                 + [_layer_spec(s, layer) for s in init_states]
                 + [pl.BlockSpec(memory_space=pl.ANY)] * len(prev_states)
                 + [wt_spec(w) for w in wts] + cast_in,
        out_specs=[pl.BlockSpec((tm, D), lambda i: (i, 0))] + state_out_specs + cast_out,
        out_shape=[jax.ShapeDtypeStruct((rows, D), F32)]
                  + [jax.ShapeDtypeStruct(s.shape, F32) for s in prev_states] + cast_shapes,
        input_output_aliases={n_lead + k: 1 + k for k in range(len(prev_states))},
        scratch_shapes=[pltpu.VMEM((tm + halo, DL), F32),
                        pltpu.VMEM((tm, DL), F32),
                        pltpu.VMEM((tm, DL), F32),
                        pltpu.VMEM((2, tm, GN), F32)],
        compiler_params=_params(("arbitrary",)),
        name=name,
    )(x, mod, *init_states, *prev_states, *wts, *[w for w, _ in cast_jobs])
    n_st = len(prev_states)
    return outs[0], tuple(outs[1:1 + n_st]), tuple(outs[1 + n_st:])


def kernel(x_prompt, x_sample, c_prompt, c_sample, state_lru_conv, state_lru_h, state_s5_re, state_s5_im, w_ada, b_ada, norm_ffn1, w1_ffn1, w3_ffn1, w2_ffn1, norm_mix, w_in, conv_w, conv_b, w_rg, b_rg, w_ig, b_ig, lru_lambda, s5_a_re, s5_a_im, s5_log_dt, s5_b_re, s5_b_im, s5_c_re, s5_c_im, s5_d, w_glu, b_glu, w_out, norm_ffn2, w1_ffn2, w3_ffn2, w2_ffn2, norm_final):
    L, D, _ = w_ada.shape
    Bp, Tp, _ = x_prompt.shape
    Bs, Ts, _ = x_sample.shape
    _, G, N = s5_a_re.shape
    DL = lru_lambda.shape[-1]
    GN = G * N
    halo = CONV_W - 1
    assert Bs % Bp == 0 and Bp % SCAN_ROWS == 0 and G % 2 == 0

    mod = _ada_call(jnp.concatenate([c_sample, c_prompt], axis=0), w_ada, b_ada)
    abr, abi, bmat, cmat, wgate = _prep_call(s5_a_re, s5_a_im, s5_log_dt, s5_b_re, s5_b_im,
                                             s5_c_re, s5_c_im, w_rg, w_ig)

    r3 = lambda v: v.reshape(L, 1, v.shape[-1])
    bgate = jnp.concatenate([b_rg, b_ig], axis=-1).reshape(L, 1, 2 * DL)
    gfin = norm_final.reshape(1, D)
    ffn1_f32 = (w1_ffn1, w3_ffn1, w2_ffn1)
    ffn2_f32 = (w1_ffn2, w3_ffn2, w2_ffn2)
    mix_f32 = (w_in, w_glu, w_out)

    def mixer_wts(w_in_b, w_glu_b, w_out_b):
        return [r3(norm_mix), w_in_b, conv_w, r3(conv_b), wgate, bgate, r3(lru_lambda), abr, abi,
                bmat, cmat, r3(s5_d), w_glu_b, r3(b_glu), w_out_b]

    def time_major(x):
        return jnp.swapaxes(x, 0, 1).reshape(x.shape[0] * x.shape[1], D)

    def state_shapes(nb):
        return ((L, halo * nb, DL), (L, nb, DL), (L, nb, GN), (L, nb, GN))

    init_p = tuple(jnp.zeros(s, F32) for s in state_shapes(Bp))
    init_s = (jnp.swapaxes(state_lru_conv, 1, 2).reshape(L, halo * Bs, DL), state_lru_h,
              state_s5_re.reshape(L, Bs, GN), state_s5_im.reshape(L, Bs, GN))
    st_p = tuple(jnp.zeros(s, F32) for s in state_shapes(Bp))
    st_s = tuple(jnp.zeros(s, F32) for s in state_shapes(Bs))
    xp, xs = x_prompt, time_major(x_sample)
    ffn1_b = tuple(w[0].astype(BF16) for w in ffn1_f32)
    for l in range(L):
        xp, xs, *mix_b = _ffn_call(xp, xs, Bp, Bs, mod, l, 0, r3(norm_ffn1), *ffn1_b, gfin,
                                   False, "st" if l == 0 else "tt", [(w, l) for w in mix_f32],
                                   f"ffn1_{l}")
        xp, st_p, ffn2_b = _mixer_call(xp, mod, l, Bs // Bp, Bp, init_p, st_p, mixer_wts(*mix_b),
                                       [(w, l) for w in ffn2_f32], f"mixer_p{l}")
        xs, st_s, _ = _mixer_call(xs, mod, l, 0, Bs, init_s, st_s, mixer_wts(*mix_b), [],
                                  f"mixer_s{l}")
        jobs = [(w, l + 1) for w in ffn1_f32] if l + 1 < L else []
        xp, xs, *ffn1_b = _ffn_call(xp, xs, Bp, Bs, mod, l, 2, r3(norm_ffn2), *ffn2_b, gfin,
                                    l == L - 1, "ts" if l == L - 1 else "tt", jobs, f"ffn2_{l}")

    def finish(y, st, nb):
        conv, hh, sre, sim = st
        return (y, jnp.swapaxes(conv.reshape(L, halo, nb, DL), 1, 2), hh,
                sre.reshape(L, nb, G, N), sim.reshape(L, nb, G, N))

    p_out = finish(xp, st_p, Bp)
    s_out = finish(jnp.swapaxes(xs.reshape(Ts, Bs, D), 0, 1), st_s, Bs)
    return (p_out[0], s_out[0]) + p_out[1:] + s_out[1:]
```

```python
import functools

import jax
import jax.numpy as jnp
from jax import lax
from jax.experimental import pallas as pl
from jax.experimental.pallas import tpu as pltpu

EPS = 1e-6
C_GATE = 8.0
FFN_RES = 0.5
CONV_W = 4
N_MOD = 9
S5_A_RE_MAX = -1e-4
BF16 = jnp.bfloat16
F32 = jnp.float32

VMEM_LIMIT_BYTES = 60 * 1024 * 1024
FFN_ROWS = 1024
FFN_SUB_ROWS = 512
MIXER_ROWS = 1024
MIXER_SUB_ROWS = 512
SCAN_ROWS = 8
SCAN_LANES = 512
LANES = 128
BF16_ROWS = 16


def _const_spec(shape, index):
    return pl.BlockSpec(shape, lambda i: index, pipeline_mode=pl.Buffered(1))


def _layer_spec(w, layer):
    return _const_spec((None,) + w.shape[1:], (layer,) + (0,) * (w.ndim - 1))


def _params(semantics):
    return pltpu.CompilerParams(dimension_semantics=semantics,
                                vmem_limit_bytes=VMEM_LIMIT_BYTES)


def _cast_steps(jobs, grid):
    n = grid
    while n > 1 and (grid % n or any(w.shape[1] % (BF16_ROWS * n) for w, _ in jobs)):
        n -= 1
    return n


def _cast_specs(jobs, n_steps, step_of):
    ins, outs, shapes = [], [], []
    for w, layer in jobs:
        _, R, C = w.shape
        assert R % (BF16_ROWS * n_steps) == 0
        ins.append(pl.BlockSpec((None, R // n_steps, C),
                                lambda i, layer=layer: (layer, step_of(i), 0)))
        outs.append(pl.BlockSpec((R // n_steps, C), lambda i: (step_of(i), 0)))
        shapes.append(jax.ShapeDtypeStruct((R, C), BF16))
    return ins, outs, shapes


def _cast_blocks(in_refs, out_refs):
    for src_ref, dst_ref in zip(in_refs, out_refs):
        dst_ref[...] = src_ref[...].astype(BF16)


def _rmsnorm(x, g):
    return x * lax.rsqrt(jnp.mean(x * x, axis=-1, keepdims=True) + EPS) * g


def _norm_modulate(x, inner, g, shift, scale):
    xn = x * lax.rsqrt(jnp.mean(x * x, axis=-1, keepdims=True) + EPS)
    xn = xn.reshape(x.shape[0] // inner, inner, x.shape[1])
    return (xn * (g * (1.0 + scale)) + shift).reshape(x.shape)


def _sqrt_nonneg(v):
    return jnp.where(v > 0.0, v * lax.rsqrt(v), v)


def _rows3(v, nb):
    return v.reshape(v.shape[0] // nb, nb, v.shape[1])


def _ada_kernel(c_ref, w_ref, b_ref, o_ref):
    c = c_ref[...]
    s = (c * jax.nn.sigmoid(c)).astype(BF16)
    D = c.shape[-1]
    for k in range(o_ref.shape[0]):
        w = w_ref[:, k * D:(k + 1) * D].astype(BF16)
        o_ref[k] = jnp.dot(s, w, preferred_element_type=F32) + b_ref[k]


def _ada_call(c_all, w_ada, b_ada):
    L, D, _ = w_ada.shape
    nseq = c_all.shape[0]
    n_sub = N_MOD // 3
    return pl.pallas_call(
        _ada_kernel,
        grid=(L, n_sub),
        in_specs=[
            pl.BlockSpec((nseq, D), lambda l, k: (0, 0)),
            pl.BlockSpec((None, D, 3 * D), lambda l, k: (l, 0, k)),
            pl.BlockSpec((None, 3, 1, D), lambda l, k: (l, k, 0, 0)),
        ],
        out_specs=pl.BlockSpec((None, 3, nseq, D), lambda l, k: (l, k, 0, 0)),
        out_shape=jax.ShapeDtypeStruct((L, N_MOD, nseq, D), F32),
        compiler_params=_params(("arbitrary", "arbitrary")),
        name="adaln_mod",
    )(c_all, w_ada, b_ada.reshape(L, N_MOD, 1, D))


def _block_diag_rows(t, nblk):
    r, width = t.shape
    c = width // nblk
    tiled = jnp.concatenate([t] * nblk, axis=0)
    row_blk = lax.broadcasted_iota(jnp.int32, tiled.shape, 0) // r
    col_blk = lax.broadcasted_iota(jnp.int32, tiled.shape, 1) // c
    return jnp.where(row_blk == col_blk, tiled, 0.0)


def _prep_kernel(are_ref, aim_ref, ldt_ref, bt_ref, ct_ref, wt_ref,
                 abr_ref, abi_ref, bmat_ref, cmat_ref, wg_ref):
    ar = jnp.minimum(are_ref[...], S5_A_RE_MAX)
    ai = aim_ref[...]
    dt = jnp.exp(ldt_ref[...])
    mag = jnp.exp(ar * dt)
    abr = mag * jnp.cos(ai * dt)
    abi = mag * jnp.sin(ai * dt)
    abr_ref[...] = abr
    abi_ref[...] = abi
    den = ar * ar + ai * ai
    f_r = ((abr - 1.0) * ar + abi * ai) / den
    f_i = (abi * ar - (abr - 1.0) * ai) / den
    b_r = bt_ref[0]
    b_i = bt_ref[1]
    bb = (f_r * b_r - f_i * b_i, f_r * b_i + f_i * b_r)
    hn = bb[0].shape[-1] // 2
    nblk = bmat_ref.shape[-2] // bb[0].shape[0]
    for part in range(2):
        for hf in range(2):
            cols = slice(hf * hn, (hf + 1) * hn)
            bmat_ref[part, hf] = _block_diag_rows(bb[part][:, cols], nblk).astype(BF16)
            cmat_ref[hf, part] = _block_diag_rows(ct_ref[part][:, cols], nblk).T.astype(BF16)
    heads = wg_ref.shape[0] // wt_ref.shape[1]
    dl = wg_ref.shape[0]
    for gate in range(2):
        wg_ref[:, gate * dl:(gate + 1) * dl] = _block_diag_rows(wt_ref[gate], heads).astype(BF16)


def _prep_call(a_re, a_im, log_dt, b_re, b_im, c_re, c_im, w_rg, w_ig):
    L, G, N = a_re.shape
    J = b_re.shape[-1]
    H, HD, _ = w_rg.shape[1:]
    GN, HN, DL = G * N, G * N // 2, H * HD
    row = lambda v: v.reshape(L, 1, GN)
    ldt = jnp.broadcast_to(log_dt[:, :, None], (L, G, N))
    bt = jnp.stack([b_re, b_im], axis=1).transpose(0, 1, 4, 2, 3).reshape(L, 2, J, GN)
    ct = jnp.stack([c_re, c_im], axis=1).transpose(0, 1, 3, 2, 4).reshape(L, 2, J, GN)
    wt = jnp.stack([w_rg, w_ig], axis=1).transpose(0, 1, 3, 2, 4).reshape(L, 2, HD, DL)
    rspec = pl.BlockSpec((None, 1, GN), lambda l: (l, 0, 0))
    jspec = pl.BlockSpec((None, 2, J, GN), lambda l: (l, 0, 0, 0))
    return pl.pallas_call(
        _prep_kernel,
        grid=(L,),
        in_specs=[rspec, rspec, rspec, jspec, jspec,
                  pl.BlockSpec((None, 2, HD, DL), lambda l: (l, 0, 0, 0))],
        out_specs=[rspec, rspec,
                   pl.BlockSpec((None, 2, 2, G // 2 * J, HN), lambda l: (l, 0, 0, 0, 0)),
                   pl.BlockSpec((None, 2, 2, HN, G // 2 * J), lambda l: (l, 0, 0, 0, 0)),
                   pl.BlockSpec((None, DL, 2 * DL), lambda l: (l, 0, 0))],
        out_shape=[jax.ShapeDtypeStruct((L, 1, GN), F32), jax.ShapeDtypeStruct((L, 1, GN), F32),
                   jax.ShapeDtypeStruct((L, 2, 2, G // 2 * J, HN), BF16),
                   jax.ShapeDtypeStruct((L, 2, 2, HN, G // 2 * J), BF16),
                   jax.ShapeDtypeStruct((L, DL, 2 * DL), BF16)],
        compiler_params=_params(("arbitrary",)),
        name="param_prep",
    )(row(a_re), row(a_im), row(ldt), bt, ct, wt)


def _ffn_tile(x, inner, shift, scale, gate, g_ref, w1_ref, w3_ref, w2_ref, gf_ref, final_norm):
    rows3 = lambda v: v.reshape(v.shape[0] // inner, inner, v.shape[1])
    h = _norm_modulate(x, inner, g_ref[...], shift, scale).astype(BF16)
    a = jnp.dot(h, w1_ref[...], preferred_element_type=F32)
    b = jnp.dot(h, w3_ref[...], preferred_element_type=F32)
    act = (a * jax.nn.sigmoid(a) * b).astype(BF16)
    f = jnp.dot(act, w2_ref[...], preferred_element_type=F32)
    out = (rows3(x) + (FFN_RES * gate) * rows3(f)).reshape(x.shape)
    if final_norm:
        out = _rmsnorm(out, gf_ref[...])
    return out


def _ffn_rows(x_ref, mod_ref, o_ref, slab_scr, wts, nb, final_norm, layout):
    D = x_ref.shape[-1]
    rows = o_ref.shape[0] * o_ref.shape[1] if layout == "ts" else o_ref.shape[0]
    steps = rows // nb
    sub = min(FFN_SUB_ROWS, rows)
    n_slab = D // LANES
    for r0 in range(0, rows, sub):
        if layout == "st":
            b0, nseq = r0 // steps, sub // steps
            x = x_ref[b0:b0 + nseq].reshape(sub, D)
            m = [mod_ref[k, b0:b0 + nseq][:, None, :] for k in range(3)]
            inner = steps
        else:
            x = x_ref[r0:r0 + sub, :]
            m = [mod_ref[k][None] for k in range(3)]
            inner = nb
        out = _ffn_tile(x, inner, *m, *wts, final_norm)
        if layout == "tt":
            o_ref[r0:r0 + sub, :] = out
        elif layout == "st":
            for b in range(b0, b0 + nseq):
                seq = out[(b - b0) * steps:(b - b0 + 1) * steps]
                for s in range(n_slab):
                    slab_scr[s, pl.ds(b, steps, stride=nb), :] = seq[:, s * LANES:(s + 1) * LANES]
        else:
            for s in range(n_slab):
                slab_scr[s, r0:r0 + sub, :] = out[:, s * LANES:(s + 1) * LANES]
            t0, nt = r0 // nb, sub // nb
            for b in range(nb):
                for s in range(n_slab):
                    o_ref[b, t0:t0 + nt, s * LANES:(s + 1) * LANES] = (
                        slab_scr[s, pl.ds(r0 + b, nt, stride=nb), :])
    if layout == "st":
        for s in range(n_slab):
            o_ref[:, s * LANES:(s + 1) * LANES] = slab_scr[s]


def _ffn_kernel(*refs, n_cast, nbp, nbs, n_prompt, final_norm, prompt_layout):
    (xp_ref, xs_ref, modp_ref, mods_ref, g_ref, w1_ref, w3_ref, w2_ref, gf_ref), refs = (
        refs[:9], refs[9:])
    cast_in, (op_ref, os_ref), cast_out, slab_scr = (
        refs[:n_cast], refs[n_cast:n_cast + 2], refs[n_cast + 2:2 * n_cast + 2], refs[-1])
    wts = (g_ref, w1_ref, w3_ref, w2_ref, gf_ref)

    @pl.when(pl.program_id(0) < n_prompt)
    def _():
        _cast_blocks(cast_in, cast_out)
        _ffn_rows(xp_ref, modp_ref, op_ref, slab_scr, wts, nbp, final_norm, prompt_layout)

    @pl.when(pl.program_id(0) == n_prompt)
    def _():
        _ffn_rows(xs_ref, mods_ref, os_ref, slab_scr, wts, nbs, final_norm, "tt")


def _ffn_call(xp, xs, nbp, nbs, mod, layer, mod_group, g, w1, w3, w2, gf, final_norm,
              prompt_layout, cast_jobs, name):
    D = xp.shape[-1]
    rows_p = xp.size // D
    rows_s = xs.shape[0]
    tm = min(FFN_ROWS, rows_p)
    assert rows_p % tm == 0 and tm % nbp == 0 and rows_s % nbs == 0 and nbs % nbp == 0
    assert D % LANES == 0 and FFN_SUB_ROWS % (tm // nbp) == 0
    n_prompt = rows_p // tm
    tile_tm = pl.BlockSpec((tm, D), lambda i: (jnp.minimum(i, n_prompt - 1), 0))
    tile_sm = pl.BlockSpec((nbp, tm // nbp, D), lambda i: (0, jnp.minimum(i, n_prompt - 1), 0))
    shape_tm = jax.ShapeDtypeStruct((rows_p, D), F32)
    shape_sm = jax.ShapeDtypeStruct((nbp, rows_p // nbp, D), F32)
    n_cast_steps = _cast_steps(cast_jobs, n_prompt)
    cast_in, cast_out, cast_shapes = _cast_specs(
        cast_jobs, n_cast_steps,
        lambda i: jnp.minimum(i, n_prompt - 1) // (n_prompt // n_cast_steps))
    return pl.pallas_call(
        functools.partial(_ffn_kernel, n_cast=len(cast_jobs), nbp=nbp, nbs=nbs,
                          n_prompt=n_prompt, final_norm=final_norm, prompt_layout=prompt_layout),
        grid=(n_prompt + 1,),
        in_specs=[
            tile_sm if prompt_layout == "st" else tile_tm,
            _const_spec((rows_s, D), (0, 0)),
            _const_spec((None, 3, nbp, D), (layer, mod_group, nbs // nbp, 0)),
            _const_spec((None, 3, nbs, D), (layer, mod_group, 0, 0)),
            _layer_spec(g, layer), _const_spec(w1.shape, (0, 0)), _const_spec(w3.shape, (0, 0)),
            _const_spec(w2.shape, (0, 0)),
            _const_spec((1, D), (0, 0)),
        ] + cast_in,
        out_specs=[tile_sm if prompt_layout == "ts" else tile_tm,
                   pl.BlockSpec((rows_s, D), lambda i: (0, 0))] + cast_out,
        out_shape=[shape_sm if prompt_layout == "ts" else shape_tm,
                   jax.ShapeDtypeStruct((rows_s, D), F32)] + cast_shapes,
        scratch_shapes=[pltpu.VMEM((D // LANES, tm, LANES), F32)],
        compiler_params=_params(("arbitrary",)),
        name=name,
    )(xp, xs, mod, mod, g, w1, w3, w2, gf, *[w for w, _ in cast_jobs])


def _softplus(v):
    return jnp.maximum(v, 0.0) + jnp.log1p(jnp.exp(-jnp.abs(v)))


def _mixer_kernel(*refs, n_cast, nb):
    (x_ref, mod_ref, conv0_ref, h0_ref, sre0_ref, sim0_ref,
     _, _, _, _,
     gn_ref, win_ref, convw_ref, convb_ref, wg_ref, bg_ref, lam_ref,
     abr_ref, abi_ref, bmat_ref, cmat_ref, d_ref, wglu_ref, bglu_ref, wout_ref), refs = (
        refs[:25], refs[25:])
    cast_in, refs = refs[:n_cast], refs[n_cast:]
    (xo_ref, convo_ref, ho_ref, sreo_ref, simo_ref), refs = refs[:5], refs[5:]
    cast_out, (xp_scr, a_scr, b_scr, s_scr) = refs[:n_cast], refs[n_cast:]
    rows, D = x_ref.shape
    sub = min(MIXER_SUB_ROWS, rows)
    tts = sub // nb
    DL = ho_ref.shape[-1]
    HS = d_ref.shape[-1] // 2
    HN = sreo_ref.shape[-1] // 2
    halo = (CONV_W - 1) * nb

    @pl.when(pl.program_id(0) == 0)
    def _():
        convo_ref[...] = conv0_ref[...]
        ho_ref[...] = h0_ref[...]
        sreo_ref[...] = sre0_ref[...]
        simo_ref[...] = sim0_ref[...]

    _cast_blocks(cast_in, cast_out)
    xp_scr[0:halo, :] = convo_ref[...]

    def in_proj(j):
        r = slice(j * sub, (j + 1) * sub)
        h = _norm_modulate(x_ref[r, :], nb, gn_ref[...], mod_ref[0][None], mod_ref[1][None])
        z = jnp.dot(h.astype(BF16), win_ref[...], preferred_element_type=F32)
        xp_scr[halo + j * sub:halo + (j + 1) * sub, :] = z[:, :DL]
        return z[:, DL:2 * DL], z[:, 2 * DL:]

    def s5_in(j, u):
        r = slice(j * sub, (j + 1) * sub)
        ub = u.astype(BF16)
        for hf in range(2):
            for part in range(2):
                s_scr[hf, r, part * HN:(part + 1) * HN] = jnp.dot(
                    ub[:, hf * HS:(hf + 1) * HS], bmat_ref[part, hf],
                    preferred_element_type=F32)

    def lru_in(j):
        r = slice(j * sub, (j + 1) * sub)
        cw = convw_ref[...]
        xc = convb_ref[...] + sum(
            xp_scr[j * sub + k * nb:(j + 1) * sub + k * nb, :] * cw[k:k + 1, :]
            for k in range(CONV_W))
        gpre = jnp.dot(xc.astype(BF16), wg_ref[...], preferred_element_type=F32) + bg_ref[...]
        rg = jax.nn.sigmoid(gpre[:, :DL])
        ig = jax.nn.sigmoid(gpre[:, DL:])
        log_a = -C_GATE * rg * _softplus(-lam_ref[...])
        a = jnp.exp(log_a)
        mult = _sqrt_nonneg(-jnp.tanh(log_a) * (a * a + 1.0))
        a_scr[r, :] = a
        b_scr[r, :] = mult * ig * xc

    def scans(j):
        row0 = j * sub
        for r0 in range(0, nb, SCAN_ROWS):
            hcur = ho_ref[r0:r0 + SCAN_ROWS, :]
            for t in range(tts):
                sl = slice(row0 + t * nb + r0, row0 + t * nb + r0 + SCAN_ROWS)
                hcur = a_scr[sl, :] * hcur + b_scr[sl, :]
                b_scr[sl, :] = hcur
            ho_ref[r0:r0 + SCAN_ROWS, :] = hcur
        for hf in range(2):
            for c0 in range(0, HN, SCAN_LANES):
                st_l = slice(hf * HN + c0, hf * HN + c0 + SCAN_LANES)
                re_l = slice(c0, c0 + SCAN_LANES)
                im_l = slice(HN + c0, HN + c0 + SCAN_LANES)
                a_r = jnp.broadcast_to(abr_ref[:, st_l], (SCAN_ROWS, SCAN_LANES))
                a_i = jnp.broadcast_to(abi_ref[:, st_l], (SCAN_ROWS, SCAN_LANES))
                for r0 in range(0, nb, SCAN_ROWS):
                    s_re = sreo_ref[r0:r0 + SCAN_ROWS, st_l]
                    s_im = simo_ref[r0:r0 + SCAN_ROWS, st_l]
                    for t in range(tts):
                        sl = slice(row0 + t * nb + r0, row0 + t * nb + r0 + SCAN_ROWS)
                        n_re = a_r * s_re - a_i * s_im + s_scr[hf, sl, re_l]
                        n_im = a_r * s_im + a_i * s_re + s_scr[hf, sl, im_l]
                        s_re, s_im = n_re, n_im
                        s_scr[hf, sl, re_l] = s_re
                        s_scr[hf, sl, im_l] = s_im
                    sreo_ref[r0:r0 + SCAN_ROWS, st_l] = s_re
                    simo_ref[r0:r0 + SCAN_ROWS, st_l] = s_im

    def s5_out(j, u):
        r = slice(j * sub, (j + 1) * sub)
        ys = []
        for hf in range(2):
            sb = s_scr[hf, r, :].astype(BF16)
            ys.append(jnp.dot(sb[:, :HN], cmat_ref[hf, 0], preferred_element_type=F32)
                      - jnp.dot(sb[:, HN:], cmat_ref[hf, 1], preferred_element_type=F32))
        return jax.nn.gelu(jnp.concatenate(ys, axis=-1) + d_ref[...] * u)

    def glu(g):
        return g * jax.nn.sigmoid(
            jnp.dot(g.astype(BF16), wglu_ref[...], preferred_element_type=F32) + bglu_ref[...])

    def out_proj(j, yb, y_s5):
        r = slice(j * sub, (j + 1) * sub)
        y_lru = jax.nn.gelu(yb) * b_scr[r, :]
        ycat = jnp.concatenate([y_lru, y_s5], axis=-1).astype(BF16)
        out = jnp.dot(ycat, wout_ref[...], preferred_element_type=F32)
        xo_ref[r, :] = (_rows3(x_ref[r, :], nb)
                        + mod_ref[2][None] * _rows3(out, nb)).reshape(sub, D)

    subs = range(rows // sub)
    yb, u = zip(*[in_proj(j) for j in subs])
    for j in subs:
        s5_in(j, u[j])
    for j in subs:
        lru_in(j)
    for j in subs:
        scans(j)
    g = [s5_out(j, u[j]) for j in subs]
    y_s5 = [glu(g[j]) for j in subs]
    for j in subs:
        out_proj(j, yb[j], y_s5[j])
    convo_ref[...] = xp_scr[rows:rows + halo, :]


def _mixer_call(x, mod, layer, mod_rowblk, nb, init_states, prev_states, wts, cast_jobs, name):
    rows, D = x.shape
    tm = min(MIXER_ROWS, rows)
    assert rows % tm == 0 and tm % nb == 0
    L, _, DL = init_states[1].shape
    GN = init_states[2].shape[-1]
    halo = (CONV_W - 1) * nb
    assert min(MIXER_SUB_ROWS, tm) // nb >= CONV_W - 1 and tm % min(MIXER_SUB_ROWS, tm) == 0

    n_lead = 2 + len(init_states)
    state_out_specs = [pl.BlockSpec((None,) + s.shape[1:], lambda i: (layer, 0, 0))
                       for s in prev_states]
    grid = rows // tm
    n_cast_steps = _cast_steps(cast_jobs, grid)
    cast_in, cast_out, cast_shapes = _cast_specs(cast_jobs, n_cast_steps,
                                                 lambda i: i // (grid // n_cast_steps))
    wt_spec = lambda w: _const_spec(w.shape, (0, 0)) if w.ndim == 2 else _layer_spec(w, layer)
    outs = pl.pallas_call(
        functools.partial(_mixer_kernel, n_cast=len(cast_jobs), nb=nb),
        grid=(grid,),
        in_specs=[pl.BlockSpec((tm, D), lambda i: (i, 0)),
                  _const_spec((None, 3, nb, D), (layer, 1, mod_rowblk, 0))]
                 + [_layer_spec(s, layer) for s in init_states]
                 + [pl.BlockSpec(memory_space=pl.ANY)] * len(prev_states)
                 + [wt_spec(w) for w in wts] + cast_in,
        out_specs=[pl.BlockSpec((tm, D), lambda i: (i, 0))] + state_out_specs + cast_out,
        out_shape=[jax.ShapeDtypeStruct((rows, D), F32)]
                  + [jax.ShapeDtypeStruct(s.shape, F32) for s in prev_states] + cast_shapes,
        input_output_aliases={n_lead + k: 1 + k for k in range(len(prev_states))},
        scratch_shapes=[pltpu.VMEM((tm + halo, DL), F32),
                        pltpu.VMEM((tm, DL), F32),
                        pltpu.VMEM((tm, DL), F32),
                        pltpu.VMEM((2, tm, GN), F32)],
        compiler_params=_params(("arbitrary",)),
        name=name,
    )(x, mod, *init_states, *prev_states, *wts, *[w for w, _ in cast_jobs])
    n_st = len(prev_states)
    return outs[0], tuple(outs[1:1 + n_st]), tuple(outs[1 + n_st:])


def kernel(x_prompt, x_sample, c_prompt, c_sample, state_lru_conv, state_lru_h, state_s5_re, state_s5_im, w_ada, b_ada, norm_ffn1, w1_ffn1, w3_ffn1, w2_ffn1, norm_mix, w_in, conv_w, conv_b, w_rg, b_rg, w_ig, b_ig, lru_lambda, s5_a_re, s5_a_im, s5_log_dt, s5_b_re, s5_b_im, s5_c_re, s5_c_im, s5_d, w_glu, b_glu, w_out, norm_ffn2, w1_ffn2, w3_ffn2, w2_ffn2, norm_final):
    L, D, _ = w_ada.shape
    Bp, Tp, _ = x_prompt.shape
    Bs, Ts, _ = x_sample.shape
    _, G, N = s5_a_re.shape
    DL = lru_lambda.shape[-1]
    GN = G * N
    halo = CONV_W - 1
    assert Bs % Bp == 0 and Bp % SCAN_ROWS == 0 and G % 2 == 0

    mod = _ada_call(jnp.concatenate([c_sample, c_prompt], axis=0), w_ada, b_ada)
    abr, abi, bmat, cmat, wgate = _prep_call(s5_a_re, s5_a_im, s5_log_dt, s5_b_re, s5_b_im,
                                             s5_c_re, s5_c_im, w_rg, w_ig)

    r3 = lambda v: v.reshape(L, 1, v.shape[-1])
    bgate = jnp.concatenate([b_rg, b_ig], axis=-1).reshape(L, 1, 2 * DL)
    gfin = norm_final.reshape(1, D)
    ffn1_f32 = (w1_ffn1, w3_ffn1, w2_ffn1)
    ffn2_f32 = (w1_ffn2, w3_ffn2, w2_ffn2)
    mix_f32 = (w_in, w_glu, w_out)

    def mixer_wts(w_in_b, w_glu_b, w_out_b):
        return [r3(norm_mix), w_in_b, conv_w, r3(conv_b), wgate, bgate, r3(lru_lambda), abr, abi,
                bmat, cmat, r3(s5_d), w_glu_b, r3(b_glu), w_out_b]

    def time_major(x):
        return jnp.swapaxes(x, 0, 1).reshape(x.shape[0] * x.shape[1], D)

    def state_shapes(nb):
        return ((L, halo * nb, DL), (L, nb, DL), (L, nb, GN), (L, nb, GN))

    init_p = tuple(jnp.zeros(s, F32) for s in state_shapes(Bp))
    init_s = (jnp.swapaxes(state_lru_conv, 1, 2).reshape(L, halo * Bs, DL), state_lru_h,
              state_s5_re.reshape(L, Bs, GN), state_s5_im.reshape(L, Bs, GN))
    st_p = tuple(jnp.zeros(s, F32) for s in state_shapes(Bp))
    st_s = tuple(jnp.zeros(s, F32) for s in state_shapes(Bs))
    xp, xs = x_prompt, time_major(x_sample)
    ffn1_b = tuple(w[0].astype(BF16) for w in ffn1_f32)
    for l in range(L):
        xp, xs, *mix_b = _ffn_call(xp, xs, Bp, Bs, mod, l, 0, r3(norm_ffn1), *ffn1_b, gfin,
                                   False, "st" if l == 0 else "tt", [(w, l) for w in mix_f32],
                                   f"ffn1_{l}")
        xp, st_p, ffn2_b = _mixer_call(xp, mod, l, Bs // Bp, Bp, init_p, st_p, mixer_wts(*mix_b),
                                       [(w, l) for w in ffn2_f32], f"mixer_p{l}")
        xs, st_s, _ = _mixer_call(xs, mod, l, 0, Bs, init_s, st_s, mixer_wts(*mix_b), [],
                                  f"mixer_s{l}")
        jobs = [(w, l + 1) for w in ffn1_f32] if l + 1 < L else []
        xp, xs, *ffn1_b = _ffn_call(xp, xs, Bp, Bs, mod, l, 2, r3(norm_ffn2), *ffn2_b, gfin,
                                    l == L - 1, "ts" if l == L - 1 else "tt", jobs, f"ffn2_{l}")

    def finish(y, st, nb):
        conv, hh, sre, sim = st
        return (y, jnp.swapaxes(conv.reshape(L, halo, nb, DL), 1, 2), hh,
                sre.reshape(L, nb, G, N), sim.reshape(L, nb, G, N))

    p_out = finish(xp, st_p, Bp)
    s_out = finish(jnp.swapaxes(xs.reshape(Ts, Bs, D), 0, 1), st_s, Bs)
    return (p_out[0], s_out[0]) + p_out[1:] + s_out[1:]
```

```python
import functools

import jax
import jax.numpy as jnp
from jax import lax
from jax.experimental import pallas as pl
from jax.experimental.pallas import tpu as pltpu

EPS = 1e-6
C_GATE = 8.0
FFN_RES = 0.5
CONV_W = 4
N_MOD = 9
S5_A_RE_MAX = -1e-4
BF16 = jnp.bfloat16
F32 = jnp.float32

VMEM_LIMIT_BYTES = 56 * 1024 * 1024
FFN_ROWS = 1024
FFN_SUB_ROWS = 512
MIXER_ROWS = 512
SCAN_ROWS = 8
SCAN_LANES = 512
LANES = 128
BF16_ROWS = 16


def _const_spec(shape, index):
    return pl.BlockSpec(shape, lambda i: index, pipeline_mode=pl.Buffered(1))


def _layer_spec(w, layer):
    return _const_spec((None,) + w.shape[1:], (layer,) + (0,) * (w.ndim - 1))


def _params(semantics):
    return pltpu.CompilerParams(dimension_semantics=semantics,
                                vmem_limit_bytes=VMEM_LIMIT_BYTES)


def _cast_steps(jobs, grid):
    n = grid
    while n > 1 and (grid % n or any(w.shape[1] % (BF16_ROWS * n) for w, _ in jobs)):
        n -= 1
    return n


def _cast_specs(jobs, n_steps, step_of):
    ins, outs, shapes = [], [], []
    for w, layer in jobs:
        _, R, C = w.shape
        assert R % (BF16_ROWS * n_steps) == 0
        ins.append(pl.BlockSpec((None, R // n_steps, C),
                                lambda i, layer=layer: (layer, step_of(i), 0)))
        outs.append(pl.BlockSpec((R // n_steps, C), lambda i: (step_of(i), 0)))
        shapes.append(jax.ShapeDtypeStruct((R, C), BF16))
    return ins, outs, shapes


def _cast_blocks(in_refs, out_refs):
    for src_ref, dst_ref in zip(in_refs, out_refs):
        dst_ref[...] = src_ref[...].astype(BF16)


def _rmsnorm(x, g):
    return x * lax.rsqrt(jnp.mean(x * x, axis=-1, keepdims=True) + EPS) * g


def _norm_modulate(x, inner, g, shift, scale):
    xn = x * lax.rsqrt(jnp.mean(x * x, axis=-1, keepdims=True) + EPS)
    xn = xn.reshape(x.shape[0] // inner, inner, x.shape[1])
    return (xn * (g * (1.0 + scale)) + shift).reshape(x.shape)


def _sqrt_nonneg(v):
    return jnp.where(v > 0.0, v * lax.rsqrt(v), v)


def _rows3(v, nb):
    return v.reshape(v.shape[0] // nb, nb, v.shape[1])


def _ada_kernel(c_ref, w_ref, b_ref, o_ref):
    c = c_ref[...]
    s = (c * jax.nn.sigmoid(c)).astype(BF16)
    D = c.shape[-1]
    for k in range(o_ref.shape[0]):
        w = w_ref[:, k * D:(k + 1) * D].astype(BF16)
        o_ref[k] = jnp.dot(s, w, preferred_element_type=F32) + b_ref[k]


def _ada_call(c_all, w_ada, b_ada):
    L, D, _ = w_ada.shape
    nseq = c_all.shape[0]
    n_sub = N_MOD // 3
    return pl.pallas_call(
        _ada_kernel,
        grid=(L, n_sub),
        in_specs=[
            pl.BlockSpec((nseq, D), lambda l, k: (0, 0)),
            pl.BlockSpec((None, D, 3 * D), lambda l, k: (l, 0, k)),
            pl.BlockSpec((None, 3, 1, D), lambda l, k: (l, k, 0, 0)),
        ],
        out_specs=pl.BlockSpec((None, 3, nseq, D), lambda l, k: (l, k, 0, 0)),
        out_shape=jax.ShapeDtypeStruct((L, N_MOD, nseq, D), F32),
        compiler_params=_params(("arbitrary", "arbitrary")),
        name="adaln_mod",
    )(c_all, w_ada, b_ada.reshape(L, N_MOD, 1, D))


def _block_diag_rows(t, nblk):
    r, width = t.shape
    c = width // nblk
    tiled = jnp.concatenate([t] * nblk, axis=0)
    row_blk = lax.broadcasted_iota(jnp.int32, tiled.shape, 0) // r
    col_blk = lax.broadcasted_iota(jnp.int32, tiled.shape, 1) // c
    return jnp.where(row_blk == col_blk, tiled, 0.0)


def _prep_kernel(are_ref, aim_ref, ldt_ref, bt_ref, ct_ref, wt_ref,
                 abr_ref, abi_ref, bmat_ref, cmat_ref, wg_ref):
    ar = jnp.minimum(are_ref[...], S5_A_RE_MAX)
    ai = aim_ref[...]
    dt = jnp.exp(ldt_ref[...])
    mag = jnp.exp(ar * dt)
    abr = mag * jnp.cos(ai * dt)
    abi = mag * jnp.sin(ai * dt)
    abr_ref[...] = abr
    abi_ref[...] = abi
    den = ar * ar + ai * ai
    f_r = ((abr - 1.0) * ar + abi * ai) / den
    f_i = (abi * ar - (abr - 1.0) * ai) / den
    b_r = bt_ref[0]
    b_i = bt_ref[1]
    bb = (f_r * b_r - f_i * b_i, f_r * b_i + f_i * b_r)
    hn = bb[0].shape[-1] // 2
    nblk = bmat_ref.shape[-2] // bb[0].shape[0]
    for part in range(2):
        for hf in range(2):
            cols = slice(hf * hn, (hf + 1) * hn)
            bmat_ref[part, hf] = _block_diag_rows(bb[part][:, cols], nblk).astype(BF16)
            cmat_ref[hf, part] = _block_diag_rows(ct_ref[part][:, cols], nblk).T.astype(BF16)
    heads = wg_ref.shape[0] // wt_ref.shape[1]
    dl = wg_ref.shape[0]
    for gate in range(2):
        wg_ref[:, gate * dl:(gate + 1) * dl] = _block_diag_rows(wt_ref[gate], heads).astype(BF16)


def _prep_call(a_re, a_im, log_dt, b_re, b_im, c_re, c_im, w_rg, w_ig):
    L, G, N = a_re.shape
    J = b_re.shape[-1]
    H, HD, _ = w_rg.shape[1:]
    GN, HN, DL = G * N, G * N // 2, H * HD
    row = lambda v: v.reshape(L, 1, GN)
    ldt = jnp.broadcast_to(log_dt[:, :, None], (L, G, N))
    bt = jnp.stack([b_re, b_im], axis=1).transpose(0, 1, 4, 2, 3).reshape(L, 2, J, GN)
    ct = jnp.stack([c_re, c_im], axis=1).transpose(0, 1, 3, 2, 4).reshape(L, 2, J, GN)
    wt = jnp.stack([w_rg, w_ig], axis=1).transpose(0, 1, 3, 2, 4).reshape(L, 2, HD, DL)
    rspec = pl.BlockSpec((None, 1, GN), lambda l: (l, 0, 0))
    jspec = pl.BlockSpec((None, 2, J, GN), lambda l: (l, 0, 0, 0))
    return pl.pallas_call(
        _prep_kernel,
        grid=(L,),
        in_specs=[rspec, rspec, rspec, jspec, jspec,
                  pl.BlockSpec((None, 2, HD, DL), lambda l: (l, 0, 0, 0))],
        out_specs=[rspec, rspec,
                   pl.BlockSpec((None, 2, 2, G // 2 * J, HN), lambda l: (l, 0, 0, 0, 0)),
                   pl.BlockSpec((None, 2, 2, HN, G // 2 * J), lambda l: (l, 0, 0, 0, 0)),
                   pl.BlockSpec((None, DL, 2 * DL), lambda l: (l, 0, 0))],
        out_shape=[jax.ShapeDtypeStruct((L, 1, GN), F32), jax.ShapeDtypeStruct((L, 1, GN), F32),
                   jax.ShapeDtypeStruct((L, 2, 2, G // 2 * J, HN), BF16),
                   jax.ShapeDtypeStruct((L, 2, 2, HN, G // 2 * J), BF16),
                   jax.ShapeDtypeStruct((L, DL, 2 * DL), BF16)],
        compiler_params=_params(("arbitrary",)),
        name="param_prep",
    )(row(a_re), row(a_im), row(ldt), bt, ct, wt)


def _ffn_matmuls(x, h, inner, gate, w1_ref, w3_ref, w2_ref, gf_ref, final_norm):
    rows3 = lambda v: v.reshape(v.shape[0] // inner, inner, v.shape[1])
    a = jnp.dot(h, w1_ref[...], preferred_element_type=F32)
    b = jnp.dot(h, w3_ref[...], preferred_element_type=F32)
    act = (a * jax.nn.sigmoid(a) * b).astype(BF16)
    f = jnp.dot(act, w2_ref[...], preferred_element_type=F32)
    out = (rows3(x) + (FFN_RES * gate) * rows3(f)).reshape(x.shape)
    if final_norm:
        out = _rmsnorm(out, gf_ref[...])
    return out


def _ffn_rows(x_ref, mod_ref, o_ref, slab_scr, h_scr, wts, nb, final_norm, layout):
    D = x_ref.shape[-1]
    rows = o_ref.shape[0] * o_ref.shape[1] if layout == "ts" else o_ref.shape[0]
    steps = rows // nb
    sub = min(FFN_SUB_ROWS, rows)
    n_slab = D // LANES
    g_ref, w1_ref, w3_ref, w2_ref, gf_ref = wts

    def tile(r0):
        if layout == "st":
            b0, nseq = r0 // steps, sub // steps
            x = x_ref[b0:b0 + nseq].reshape(sub, D)
            return x, steps, [mod_ref[k, b0:b0 + nseq][:, None, :] for k in range(3)]
        return x_ref[r0:r0 + sub, :], nb, [mod_ref[k][None] for k in range(3)]

    for r0 in range(0, rows, sub):
        x, inner, (shift, scale, _) = tile(r0)
        h_scr[r0:r0 + sub, :] = _norm_modulate(x, inner, g_ref[...], shift, scale).astype(BF16)
    for r0 in range(0, rows, sub):
        x, inner, (_, _, gate) = tile(r0)
        out = _ffn_matmuls(x, h_scr[r0:r0 + sub, :], inner, gate, w1_ref, w3_ref, w2_ref,
                           gf_ref, final_norm)
        if layout == "st":
            b0, nseq = r0 // steps, sub // steps
        if layout == "tt":
            o_ref[r0:r0 + sub, :] = out
        elif layout == "st":
            for b in range(b0, b0 + nseq):
                seq = out[(b - b0) * steps:(b - b0 + 1) * steps]
                for s in range(n_slab):
                    slab_scr[s, pl.ds(b, steps, stride=nb), :] = seq[:, s * LANES:(s + 1) * LANES]
        else:
            for s in range(n_slab):
                slab_scr[s, r0:r0 + sub, :] = out[:, s * LANES:(s + 1) * LANES]
            t0, nt = r0 // nb, sub // nb
            for b in range(nb):
                for s in range(n_slab):
                    o_ref[b, t0:t0 + nt, s * LANES:(s + 1) * LANES] = (
                        slab_scr[s, pl.ds(r0 + b, nt, stride=nb), :])
    if layout == "st":
        for s in range(n_slab):
            o_ref[:, s * LANES:(s + 1) * LANES] = slab_scr[s]


def _ffn_kernel(*refs, n_cast, nbp, nbs, n_prompt, final_norm, prompt_layout):
    (xp_ref, xs_ref, modp_ref, mods_ref, g_ref, w1_ref, w3_ref, w2_ref, gf_ref), refs = (
        refs[:9], refs[9:])
    cast_in, (op_ref, os_ref), cast_out, (slab_scr, h_scr) = (
        refs[:n_cast], refs[n_cast:n_cast + 2], refs[n_cast + 2:2 * n_cast + 2], refs[-2:])
    wts = (g_ref, w1_ref, w3_ref, w2_ref, gf_ref)

    @pl.when(pl.program_id(0) < n_prompt)
    def _():
        _cast_blocks(cast_in, cast_out)
        _ffn_rows(xp_ref, modp_ref, op_ref, slab_scr, h_scr, wts, nbp, final_norm, prompt_layout)

    @pl.when(pl.program_id(0) == n_prompt)
    def _():
        _ffn_rows(xs_ref, mods_ref, os_ref, slab_scr, h_scr, wts, nbs, final_norm, "tt")


def _ffn_call(xp, xs, nbp, nbs, mod, layer, mod_group, g, w1, w3, w2, gf, final_norm,
              prompt_layout, cast_jobs, name):
    D = xp.shape[-1]
    rows_p = xp.size // D
    rows_s = xs.shape[0]
    tm = min(FFN_ROWS, rows_p)
    assert rows_p % tm == 0 and tm % nbp == 0 and rows_s % nbs == 0 and nbs % nbp == 0
    assert D % LANES == 0 and FFN_SUB_ROWS % (tm // nbp) == 0
    n_prompt = rows_p // tm
    tile_tm = pl.BlockSpec((tm, D), lambda i: (jnp.minimum(i, n_prompt - 1), 0))
    tile_sm = pl.BlockSpec((nbp, tm // nbp, D), lambda i: (0, jnp.minimum(i, n_prompt - 1), 0))
    shape_tm = jax.ShapeDtypeStruct((rows_p, D), F32)
    shape_sm = jax.ShapeDtypeStruct((nbp, rows_p // nbp, D), F32)
    n_cast_steps = _cast_steps(cast_jobs, n_prompt)
    cast_in, cast_out, cast_shapes = _cast_specs(
        cast_jobs, n_cast_steps,
        lambda i: jnp.minimum(i, n_prompt - 1) // (n_prompt // n_cast_steps))
    return pl.pallas_call(
        functools.partial(_ffn_kernel, n_cast=len(cast_jobs), nbp=nbp, nbs=nbs,
                          n_prompt=n_prompt, final_norm=final_norm, prompt_layout=prompt_layout),
        grid=(n_prompt + 1,),
        in_specs=[
            tile_sm if prompt_layout == "st" else tile_tm,
            _const_spec((rows_s, D), (0, 0)),
            _const_spec((None, 3, nbp, D), (layer, mod_group, nbs // nbp, 0)),
            _const_spec((None, 3, nbs, D), (layer, mod_group, 0, 0)),
            _layer_spec(g, layer), _const_spec(w1.shape, (0, 0)), _const_spec(w3.shape, (0, 0)),
            _const_spec(w2.shape, (0, 0)),
            _const_spec((1, D), (0, 0)),
        ] + cast_in,
        out_specs=[tile_sm if prompt_layout == "ts" else tile_tm,
                   pl.BlockSpec((rows_s, D), lambda i: (0, 0))] + cast_out,
        out_shape=[shape_sm if prompt_layout == "ts" else shape_tm,
                   jax.ShapeDtypeStruct((rows_s, D), F32)] + cast_shapes,
        scratch_shapes=[pltpu.VMEM((D // LANES, tm, LANES), F32), pltpu.VMEM((tm, D), BF16)],
        compiler_params=_params(("arbitrary",)),
        name=name,
    )(xp, xs, mod, mod, g, w1, w3, w2, gf, *[w for w, _ in cast_jobs])


def _softplus(v):
    return jnp.maximum(v, 0.0) + jnp.log1p(jnp.exp(-jnp.abs(v)))


def _mixer_kernel(*refs, n_cast, nb):
    (x_ref, mod_ref, conv0_ref, h0_ref, sre0_ref, sim0_ref,
     _, _, _, _,
     gn_ref, win_ref, convw_ref, convb_ref, wg_ref, bg_ref, lam_ref,
     abr_ref, abi_ref, bmat_ref, cmat_ref, d_ref, wglu_ref, bglu_ref, wout_ref), refs = (
        refs[:25], refs[25:])
    cast_in, refs = refs[:n_cast], refs[n_cast:]
    (xo_ref, convo_ref, ho_ref, sreo_ref, simo_ref), refs = refs[:5], refs[5:]
    cast_out, (xp_scr, a_scr, b_scr, s_scr) = refs[:n_cast], refs[n_cast:]
    rows, D = x_ref.shape
    tt = rows // nb
    DL = ho_ref.shape[-1]
    HS = d_ref.shape[-1] // 2
    HN = sreo_ref.shape[-1] // 2

    @pl.when(pl.program_id(0) == 0)
    def _():
        convo_ref[...] = conv0_ref[...]
        ho_ref[...] = h0_ref[...]
        sreo_ref[...] = sre0_ref[...]
        simo_ref[...] = sim0_ref[...]

    _cast_blocks(cast_in, cast_out)
    x = x_ref[...]
    h = _norm_modulate(x, nb, gn_ref[...], mod_ref[0][None], mod_ref[1][None])
    z = jnp.dot(h.astype(BF16), win_ref[...], preferred_element_type=F32)
    xb = z[:, :DL]
    yb = z[:, DL:2 * DL]
    u = z[:, 2 * DL:]

    halo = (CONV_W - 1) * nb
    xp_scr[0:halo, :] = convo_ref[...]
    xp_scr[halo:halo + rows, :] = xb
    cw = convw_ref[...]
    xc = convb_ref[...] + sum(xp_scr[k * nb:k * nb + rows, :] * cw[k:k + 1, :]
                              for k in range(CONV_W))
    convo_ref[...] = xp_scr[rows:rows + halo, :]

    gpre = jnp.dot(xc.astype(BF16), wg_ref[...], preferred_element_type=F32) + bg_ref[...]
    r = jax.nn.sigmoid(gpre[:, :DL])
    ig = jax.nn.sigmoid(gpre[:, DL:])
    log_a = -C_GATE * r * _softplus(-lam_ref[...])
    a = jnp.exp(log_a)
    mult = _sqrt_nonneg(-jnp.tanh(log_a) * (a * a + 1.0))
    a_scr[...] = a
    b_scr[...] = mult * ig * xc

    for r0 in range(0, nb, SCAN_ROWS):
        hcur = ho_ref[r0:r0 + SCAN_ROWS, :]
        for t in range(tt):
            sl = slice(t * nb + r0, t * nb + r0 + SCAN_ROWS)
            hcur = a_scr[sl, :] * hcur + b_scr[sl, :]
            b_scr[sl, :] = hcur
        ho_ref[r0:r0 + SCAN_ROWS, :] = hcur
    y_lru = jax.nn.gelu(yb) * b_scr[...]

    ub = u.astype(BF16)
    for hf in range(2):
        for part in range(2):
            s_scr[hf, :, part * HN:(part + 1) * HN] = jnp.dot(
                ub[:, hf * HS:(hf + 1) * HS], bmat_ref[part, hf], preferred_element_type=F32)

    for hf in range(2):
        for c0 in range(0, HN, SCAN_LANES):
            st_l = slice(hf * HN + c0, hf * HN + c0 + SCAN_LANES)
            re_l = slice(c0, c0 + SCAN_LANES)
            im_l = slice(HN + c0, HN + c0 + SCAN_LANES)
            a_r = jnp.broadcast_to(abr_ref[:, st_l], (SCAN_ROWS, SCAN_LANES))
            a_i = jnp.broadcast_to(abi_ref[:, st_l], (SCAN_ROWS, SCAN_LANES))
            for r0 in range(0, nb, SCAN_ROWS):
                s_re = sreo_ref[r0:r0 + SCAN_ROWS, st_l]
                s_im = simo_ref[r0:r0 + SCAN_ROWS, st_l]
                for t in range(tt):
                    sl = slice(t * nb + r0, t * nb + r0 + SCAN_ROWS)
                    n_re = a_r * s_re - a_i * s_im + s_scr[hf, sl, re_l]
                    n_im = a_r * s_im + a_i * s_re + s_scr[hf, sl, im_l]
                    s_re, s_im = n_re, n_im
                    s_scr[hf, sl, re_l] = s_re
                    s_scr[hf, sl, im_l] = s_im
                sreo_ref[r0:r0 + SCAN_ROWS, st_l] = s_re
                simo_ref[r0:r0 + SCAN_ROWS, st_l] = s_im

    ys = []
    for hf in range(2):
        sb = s_scr[hf].astype(BF16)
        ys.append(jnp.dot(sb[:, :HN], cmat_ref[hf, 0], preferred_element_type=F32)
                  - jnp.dot(sb[:, HN:], cmat_ref[hf, 1], preferred_element_type=F32))
    ys = jnp.concatenate(ys, axis=-1) + d_ref[...] * u
    g = jax.nn.gelu(ys)
    y_s5 = g * jax.nn.sigmoid(
        jnp.dot(g.astype(BF16), wglu_ref[...], preferred_element_type=F32) + bglu_ref[...])

    ycat = jnp.concatenate([y_lru, y_s5], axis=-1).astype(BF16)
    out = jnp.dot(ycat, wout_ref[...], preferred_element_type=F32)
    xo_ref[...] = (_rows3(x, nb) + mod_ref[2][None] * _rows3(out, nb)).reshape(rows, D)


def _mixer_call(x, mod, layer, mod_rowblk, nb, init_states, prev_states, wts, cast_jobs, name):
    rows, D = x.shape
    tm = min(MIXER_ROWS, rows)
    assert rows % tm == 0 and tm % nb == 0
    L, _, DL = init_states[1].shape
    GN = init_states[2].shape[-1]
    halo = (CONV_W - 1) * nb
    assert tm // nb >= CONV_W - 1

    n_lead = 2 + len(init_states)
    state_out_specs = [pl.BlockSpec((None,) + s.shape[1:], lambda i: (layer, 0, 0))
                       for s in prev_states]
    grid = rows // tm
    n_cast_steps = _cast_steps(cast_jobs, grid)
    cast_in, cast_out, cast_shapes = _cast_specs(cast_jobs, n_cast_steps,
                                                 lambda i: i // (grid // n_cast_steps))
    wt_spec = lambda w: _const_spec(w.shape, (0, 0)) if w.ndim == 2 else _layer_spec(w, layer)
    outs = pl.pallas_call(
        functools.partial(_mixer_kernel, n_cast=len(cast_jobs), nb=nb),
        grid=(grid,),
        in_specs=[pl.BlockSpec((tm, D), lambda i: (i, 0)),
                  _const_spec((None, 3, nb, D), (layer, 1, mod_rowblk, 0))]
                 + [_layer_spec(s, layer) for s in init_states]
                 + [pl.BlockSpec(memory_space=pl.ANY)] * len(prev_states)
                 + [wt_spec(w) for w in wts] + cast_in,
        out_specs=[pl.BlockSpec((tm, D), lambda i: (i, 0))] + state_out_specs + cast_out,
        out_shape=[jax.ShapeDtypeStruct((rows, D), F32)]
                  + [jax.ShapeDtypeStruct(s.shape, F32) for s in prev_states] + cast_shapes,
        input_output_aliases={n_lead + k: 1 + k for k in range(len(prev_states))},
        scratch_shapes=[pltpu.VMEM((tm + halo, DL), F32),
                        pltpu.VMEM((tm, DL), F32),
                        pltpu.VMEM((tm, DL), F32),
                        pltpu.VMEM((2, tm, GN), F32)],
        compiler_params=_params(("arbitrary",)),
        name=name,
    )(x, mod, *init_states, *prev_states, *wts, *[w for w, _ in cast_jobs])
    n_st = len(prev_states)
    return outs[0], tuple(outs[1:1 + n_st]), tuple(outs[1 + n_st:])


def kernel(x_prompt, x_sample, c_prompt, c_sample, state_lru_conv, state_lru_h, state_s5_re, state_s5_im, w_ada, b_ada, norm_ffn1, w1_ffn1, w3_ffn1, w2_ffn1, norm_mix, w_in, conv_w, conv_b, w_rg, b_rg, w_ig, b_ig, lru_lambda, s5_a_re, s5_a_im, s5_log_dt, s5_b_re, s5_b_im, s5_c_re, s5_c_im, s5_d, w_glu, b_glu, w_out, norm_ffn2, w1_ffn2, w3_ffn2, w2_ffn2, norm_final):
    L, D, _ = w_ada.shape
    Bp, Tp, _ = x_prompt.shape
    Bs, Ts, _ = x_sample.shape
    _, G, N = s5_a_re.shape
    DL = lru_lambda.shape[-1]
    GN = G * N
    halo = CONV_W - 1
    assert Bs % Bp == 0 and Bp % SCAN_ROWS == 0 and G % 2 == 0

    mod = _ada_call(jnp.concatenate([c_sample, c_prompt], axis=0), w_ada, b_ada)
    abr, abi, bmat, cmat, wgate = _prep_call(s5_a_re, s5_a_im, s5_log_dt, s5_b_re, s5_b_im,
                                             s5_c_re, s5_c_im, w_rg, w_ig)

    r3 = lambda v: v.reshape(L, 1, v.shape[-1])
    bgate = jnp.concatenate([b_rg, b_ig], axis=-1).reshape(L, 1, 2 * DL)
    gfin = norm_final.reshape(1, D)
    ffn1_f32 = (w1_ffn1, w3_ffn1, w2_ffn1)
    ffn2_f32 = (w1_ffn2, w3_ffn2, w2_ffn2)
    mix_f32 = (w_in, w_glu, w_out)

    def mixer_wts(w_in_b, w_glu_b, w_out_b):
        return [r3(norm_mix), w_in_b, conv_w, r3(conv_b), wgate, bgate, r3(lru_lambda), abr, abi,
                bmat, cmat, r3(s5_d), w_glu_b, r3(b_glu), w_out_b]

    def time_major(x):
        return jnp.swapaxes(x, 0, 1).reshape(x.shape[0] * x.shape[1], D)

    def state_shapes(nb):
        return ((L, halo * nb, DL), (L, nb, DL), (L, nb, GN), (L, nb, GN))

    init_p = tuple(jnp.zeros(s, F32) for s in state_shapes(Bp))
    init_s = (jnp.swapaxes(state_lru_conv, 1, 2).reshape(L, halo * Bs, DL), state_lru_h,
              state_s5_re.reshape(L, Bs, GN), state_s5_im.reshape(L, Bs, GN))
    st_p = tuple(jnp.zeros(s, F32) for s in state_shapes(Bp))
    st_s = tuple(jnp.zeros(s, F32) for s in state_shapes(Bs))
    xp, xs = x_prompt, time_major(x_sample)
    ffn1_b = tuple(w[0].astype(BF16) for w in ffn1_f32)
    for l in range(L):
        xp, xs, *mix_b = _ffn_call(xp, xs, Bp, Bs, mod, l, 0, r3(norm_ffn1), *ffn1_b, gfin,
                                   False, "st" if l == 0 else "tt", [(w, l) for w in mix_f32],
                                   f"ffn1_{l}")
        xp, st_p, ffn2_b = _mixer_call(xp, mod, l, Bs // Bp, Bp, init_p, st_p, mixer_wts(*mix_b),
                                       [(w, l) for w in ffn2_f32], f"mixer_p{l}")
        xs, st_s, _ = _mixer_call(xs, mod, l, 0, Bs, init_s, st_s, mixer_wts(*mix_b), [],
                                  f"mixer_s{l}")
        jobs = [(w, l + 1) for w in ffn1_f32] if l + 1 < L else []
        xp, xs, *ffn1_b = _ffn_call(xp, xs, Bp, Bs, mod, l, 2, r3(norm_ffn2), *ffn2_b, gfin,
                                    l == L - 1, "ts" if l == L - 1 else "tt", jobs, f"ffn2_{l}")

    def finish(y, st, nb):
        conv, hh, sre, sim = st
        return (y, jnp.swapaxes(conv.reshape(L, halo, nb, DL), 1, 2), hh,
                sre.reshape(L, nb, G, N), sim.reshape(L, nb, G, N))

    p_out = finish(xp, st_p, Bp)
    s_out = finish(jnp.swapaxes(xs.reshape(Ts, Bs, D), 0, 1), st_s, Bs)
    return (p_out[0], s_out[0]) + p_out[1:] + s_out[1:]
```

```python
import functools

import jax
import jax.numpy as jnp
from jax import lax
from jax.experimental import pallas as pl
from jax.experimental.pallas import tpu as pltpu

EPS = 1e-6
C_GATE = 8.0
FFN_RES = 0.5
CONV_W = 4
N_MOD = 9
S5_A_RE_MAX = -1e-4
BF16 = jnp.bfloat16
F32 = jnp.float32

VMEM_LIMIT_BYTES = 56 * 1024 * 1024
FFN_ROWS = 1024
FFN_SUB_ROWS = 512
MIXER_ROWS = 512
SCAN_ROWS = 8
SCAN_LANES = 512
LANES = 128
BF16_ROWS = 16


def _const_spec(shape, index):
    return pl.BlockSpec(shape, lambda i: index, pipeline_mode=pl.Buffered(1))


def _layer_spec(w, layer):
    return _const_spec((None,) + w.shape[1:], (layer,) + (0,) * (w.ndim - 1))


def _params(semantics):
    return pltpu.CompilerParams(dimension_semantics=semantics,
                                vmem_limit_bytes=VMEM_LIMIT_BYTES)


def _cast_steps(jobs, grid):
    n = grid
    while n > 1 and (grid % n or any(w.shape[1] % (BF16_ROWS * n) for w, _ in jobs)):
        n -= 1
    return n


def _cast_specs(jobs, n_steps, step_of):
    ins, outs, shapes = [], [], []
    for w, layer in jobs:
        _, R, C = w.shape
        assert R % (BF16_ROWS * n_steps) == 0
        ins.append(pl.BlockSpec((None, R // n_steps, C),
                                lambda i, layer=layer: (layer, step_of(i), 0)))
        outs.append(pl.BlockSpec((R // n_steps, C), lambda i: (step_of(i), 0)))
        shapes.append(jax.ShapeDtypeStruct((R, C), BF16))
    return ins, outs, shapes


def _cast_blocks(in_refs, out_refs):
    for src_ref, dst_ref in zip(in_refs, out_refs):
        dst_ref[...] = src_ref[...].astype(BF16)


def _rmsnorm(x, g):
    return x * lax.rsqrt(jnp.mean(x * x, axis=-1, keepdims=True) + EPS) * g


def _norm_modulate(x, inner, g, shift, scale):
    xn = x * lax.rsqrt(jnp.mean(x * x, axis=-1, keepdims=True) + EPS)
    xn = xn.reshape(x.shape[0] // inner, inner, x.shape[1])
    return (xn * (g * (1.0 + scale)) + shift).reshape(x.shape)


def _sqrt_nonneg(v):
    return jnp.where(v > 0.0, v * lax.rsqrt(v), v)


def _rows3(v, nb):
    return v.reshape(v.shape[0] // nb, nb, v.shape[1])


def _ada_kernel(c_ref, w_ref, b_ref, o_ref):
    c = c_ref[...]
    s = (c * jax.nn.sigmoid(c)).astype(BF16)
    D = c.shape[-1]
    for k in range(o_ref.shape[0]):
        w = w_ref[:, k * D:(k + 1) * D].astype(BF16)
        o_ref[k] = jnp.dot(s, w, preferred_element_type=F32) + b_ref[k]


def _ada_call(c_all, w_ada, b_ada):
    L, D, _ = w_ada.shape
    nseq = c_all.shape[0]
    n_sub = N_MOD // 3
    return pl.pallas_call(
        _ada_kernel,
        grid=(L, n_sub),
        in_specs=[
            pl.BlockSpec((nseq, D), lambda l, k: (0, 0)),
            pl.BlockSpec((None, D, 3 * D), lambda l, k: (l, 0, k)),
            pl.BlockSpec((None, 3, 1, D), lambda l, k: (l, k, 0, 0)),
        ],
        out_specs=pl.BlockSpec((None, 3, nseq, D), lambda l, k: (l, k, 0, 0)),
        out_shape=jax.ShapeDtypeStruct((L, N_MOD, nseq, D), F32),
        compiler_params=_params(("arbitrary", "arbitrary")),
        name="adaln_mod",
    )(c_all, w_ada, b_ada.reshape(L, N_MOD, 1, D))


def _block_diag_rows(t, nblk):
    r, width = t.shape
    c = width // nblk
    tiled = jnp.concatenate([t] * nblk, axis=0)
    row_blk = lax.broadcasted_iota(jnp.int32, tiled.shape, 0) // r
    col_blk = lax.broadcasted_iota(jnp.int32, tiled.shape, 1) // c
    return jnp.where(row_blk == col_blk, tiled, 0.0)


def _prep_kernel(are_ref, aim_ref, ldt_ref, bt_ref, ct_ref, wt_ref,
                 abr_ref, abi_ref, bmat_ref, cmat_ref, wg_ref):
    ar = jnp.minimum(are_ref[...], S5_A_RE_MAX)
    ai = aim_ref[...]
    dt = jnp.exp(ldt_ref[...])
    mag = jnp.exp(ar * dt)
    abr = mag * jnp.cos(ai * dt)
    abi = mag * jnp.sin(ai * dt)
    abr_ref[...] = abr
    abi_ref[...] = abi
    den = ar * ar + ai * ai
    f_r = ((abr - 1.0) * ar + abi * ai) / den
    f_i = (abi * ar - (abr - 1.0) * ai) / den
    b_r = bt_ref[0]
    b_i = bt_ref[1]
    bb = (f_r * b_r - f_i * b_i, f_r * b_i + f_i * b_r)
    hn = bb[0].shape[-1] // 2
    nblk = bmat_ref.shape[-2] // bb[0].shape[0]
    for part in range(2):
        for hf in range(2):
            cols = slice(hf * hn, (hf + 1) * hn)
            bmat_ref[part, hf] = _block_diag_rows(bb[part][:, cols], nblk).astype(BF16)
            cmat_ref[hf, part] = _block_diag_rows(ct_ref[part][:, cols], nblk).T.astype(BF16)
    hd = wt_ref.shape[1]
    hw = wg_ref.shape[1]
    for q in range(2):
        for gate in range(2):
            wg_ref[q, :, gate * hw:(gate + 1) * hw] = _block_diag_rows(
                wt_ref[gate][:, q * hw:(q + 1) * hw], hw // hd).astype(BF16)


def _prep_call(a_re, a_im, log_dt, b_re, b_im, c_re, c_im, w_rg, w_ig):
    L, G, N = a_re.shape
    J = b_re.shape[-1]
    H, HD, _ = w_rg.shape[1:]
    GN, HN, DL = G * N, G * N // 2, H * HD
    row = lambda v: v.reshape(L, 1, GN)
    ldt = jnp.broadcast_to(log_dt[:, :, None], (L, G, N))
    bt = jnp.stack([b_re, b_im], axis=1).transpose(0, 1, 4, 2, 3).reshape(L, 2, J, GN)
    ct = jnp.stack([c_re, c_im], axis=1).transpose(0, 1, 3, 2, 4).reshape(L, 2, J, GN)
    wt = jnp.stack([w_rg, w_ig], axis=1).transpose(0, 1, 3, 2, 4).reshape(L, 2, HD, DL)
    rspec = pl.BlockSpec((None, 1, GN), lambda l: (l, 0, 0))
    jspec = pl.BlockSpec((None, 2, J, GN), lambda l: (l, 0, 0, 0))
    return pl.pallas_call(
        _prep_kernel,
        grid=(L,),
        in_specs=[rspec, rspec, rspec, jspec, jspec,
                  pl.BlockSpec((None, 2, HD, DL), lambda l: (l, 0, 0, 0))],
        out_specs=[rspec, rspec,
                   pl.BlockSpec((None, 2, 2, G // 2 * J, HN), lambda l: (l, 0, 0, 0, 0)),
                   pl.BlockSpec((None, 2, 2, HN, G // 2 * J), lambda l: (l, 0, 0, 0, 0)),
                   pl.BlockSpec((None, 2, DL // 2, DL), lambda l: (l, 0, 0, 0))],
        out_shape=[jax.ShapeDtypeStruct((L, 1, GN), F32), jax.ShapeDtypeStruct((L, 1, GN), F32),
                   jax.ShapeDtypeStruct((L, 2, 2, G // 2 * J, HN), BF16),
                   jax.ShapeDtypeStruct((L, 2, 2, HN, G // 2 * J), BF16),
                   jax.ShapeDtypeStruct((L, 2, DL // 2, DL), BF16)],
        compiler_params=_params(("arbitrary",)),
        name="param_prep",
    )(row(a_re), row(a_im), row(ldt), bt, ct, wt)


def _ffn_matmuls(x, h, inner, gate, w1_ref, w3_ref, w2_ref, gf_ref, final_norm):
    rows3 = lambda v: v.reshape(v.shape[0] // inner, inner, v.shape[1])
    a = jnp.dot(h, w1_ref[...], preferred_element_type=F32)
    b = jnp.dot(h, w3_ref[...], preferred_element_type=F32)
    act = (a * jax.nn.sigmoid(a) * b).astype(BF16)
    f = jnp.dot(act, w2_ref[...], preferred_element_type=F32)
    out = (rows3(x) + (FFN_RES * gate) * rows3(f)).reshape(x.shape)
    if final_norm:
        out = _rmsnorm(out, gf_ref[...])
    return out


def _slabs(v):
    return [v[:, s * LANES:(s + 1) * LANES] for s in range(v.shape[1] // LANES)]


def _ffn_rows(x_ref, mod_ref, o_ref, h_scr, wts, nb, final_norm, layout):
    rows, D = h_scr.shape[0] if layout != "tt" else x_ref.shape[0], h_scr.shape[1]
    steps = rows // nb
    sub = min(FFN_SUB_ROWS, rows)
    nseq = sub // steps
    g_ref, w1_ref, w3_ref, w2_ref, gf_ref = wts

    def tile(r0):
        if layout == "tt":
            return x_ref[r0:r0 + sub, :], nb, [mod_ref[k][None] for k in range(3)]
        b0 = r0 // steps
        if layout == "st":
            x = x_ref[b0:b0 + nseq].reshape(sub, D)
        else:
            x = jnp.concatenate(
                [jnp.concatenate([x_ref[s, pl.ds(b, steps, stride=nb), :]
                                  for s in range(D // LANES)], axis=-1)
                 for b in range(b0, b0 + nseq)], axis=0)
        return x, steps, [mod_ref[k, b0:b0 + nseq][:, None, :] for k in range(3)]

    for r0 in range(0, rows, sub):
        x, inner, (shift, scale, _) = tile(r0)
        h_scr[r0:r0 + sub, :] = _norm_modulate(x, inner, g_ref[...], shift, scale).astype(BF16)
    for r0 in range(0, rows, sub):
        x, inner, (_, _, gate) = tile(r0)
        out = _ffn_matmuls(x, h_scr[r0:r0 + sub, :], inner, gate, w1_ref, w3_ref, w2_ref,
                           gf_ref, final_norm)
        b0 = r0 // steps
        if layout == "tt":
            o_ref[r0:r0 + sub, :] = out
        elif layout == "ts":
            o_ref[b0:b0 + nseq] = out.reshape(nseq, steps, D)
        else:
            for b in range(b0, b0 + nseq):
                seq = out[(b - b0) * steps:(b - b0 + 1) * steps]
                for s, slab in enumerate(_slabs(seq)):
                    o_ref[s, pl.ds(b, steps, stride=nb), :] = slab


def _ffn_kernel(*refs, n_cast, nbp, nbs, n_prompt, final_norm, prompt_layout):
    (xp_ref, xs_ref, modp_ref, mods_ref, g_ref, w1_ref, w3_ref, w2_ref, gf_ref), refs = (
        refs[:9], refs[9:])
    cast_in, (op_ref, os_ref), cast_out, h_scr = (
        refs[:n_cast], refs[n_cast:n_cast + 2], refs[n_cast + 2:2 * n_cast + 2], refs[-1])
    wts = (g_ref, w1_ref, w3_ref, w2_ref, gf_ref)

    @pl.when(pl.program_id(0) < n_prompt)
    def _():
        _ffn_rows(xp_ref, modp_ref, op_ref, h_scr, wts, nbp, final_norm, prompt_layout)
        _cast_blocks(cast_in, cast_out)

    @pl.when(pl.program_id(0) == n_prompt)
    def _():
        _ffn_rows(xs_ref, mods_ref, os_ref, h_scr, wts, nbs, final_norm, "tt")


def _ffn_call(xp, xs, nbp, nbs, mod, layer, mod_group, g, w1, w3, w2, gf, final_norm,
              prompt_layout, cast_jobs, name):
    D = xs.shape[-1]
    rows_p = xp.size // D
    rows_s = xs.shape[0]
    tm = min(FFN_ROWS, rows_p)
    assert rows_p % tm == 0 and tm % nbp == 0 and rows_s % nbs == 0 and nbs % nbp == 0
    assert D % LANES == 0 and FFN_SUB_ROWS % (tm // nbp) == 0
    n_prompt = rows_p // tm
    tile_tm = pl.BlockSpec((tm, D), lambda i: (jnp.minimum(i, n_prompt - 1), 0))
    tile_sm = pl.BlockSpec((nbp, tm // nbp, D), lambda i: (0, jnp.minimum(i, n_prompt - 1), 0))
    tile_slab = pl.BlockSpec((D // LANES, tm, LANES),
                             lambda i: (0, jnp.minimum(i, n_prompt - 1), 0))
    in_tile = {"tt": tile_tm, "st": tile_sm, "ts": tile_slab}[prompt_layout]
    out_tile = {"tt": tile_tm, "st": tile_slab, "ts": tile_sm}[prompt_layout]
    out_shape = {"tt": (rows_p, D), "st": (D // LANES, rows_p, LANES),
                 "ts": (nbp, rows_p // nbp, D)}[prompt_layout]
    n_cast_steps = _cast_steps(cast_jobs, n_prompt)
    cast_in, cast_out, cast_shapes = _cast_specs(
        cast_jobs, n_cast_steps,
        lambda i: jnp.minimum(i, n_prompt - 1) // (n_prompt // n_cast_steps))
    return pl.pallas_call(
        functools.partial(_ffn_kernel, n_cast=len(cast_jobs), nbp=nbp, nbs=nbs,
                          n_prompt=n_prompt, final_norm=final_norm, prompt_layout=prompt_layout),
        grid=(n_prompt + 1,),
        in_specs=[
            in_tile,
            _const_spec((rows_s, D), (0, 0)),
            _const_spec((None, 3, nbp, D), (layer, mod_group, nbs // nbp, 0)),
            _const_spec((None, 3, nbs, D), (layer, mod_group, 0, 0)),
            _layer_spec(g, layer), _const_spec(w1.shape, (0, 0)), _const_spec(w3.shape, (0, 0)),
            _const_spec(w2.shape, (0, 0)),
            _const_spec((1, D), (0, 0)),
        ] + cast_in,
        out_specs=[out_tile, pl.BlockSpec((rows_s, D), lambda i: (0, 0))] + cast_out,
        out_shape=[jax.ShapeDtypeStruct(out_shape, F32),
                   jax.ShapeDtypeStruct((rows_s, D), F32)] + cast_shapes,
        scratch_shapes=[pltpu.VMEM((tm, D), BF16)],
        compiler_params=_params(("arbitrary",)),
        name=name,
    )(xp, xs, mod, mod, g, w1, w3, w2, gf, *[w for w, _ in cast_jobs])


def _softplus(v):
    return jnp.maximum(v, 0.0) + jnp.log1p(jnp.exp(-jnp.abs(v)))


def _mixer_kernel(*refs, n_cast, nb, x_slab, out_slab):
    (x_ref, mod_ref, conv0_ref, h0_ref, sre0_ref, sim0_ref,
     _, _, _, _,
     gn_ref, win_ref, convw_ref, convb_ref, wg_ref, bg_ref, lam_ref,
     abr_ref, abi_ref, bmat_ref, cmat_ref, d_ref, wglu_ref, bglu_ref, wout_ref), refs = (
        refs[:25], refs[25:])
    cast_in, refs = refs[:n_cast], refs[n_cast:]
    (xo_ref, convo_ref, ho_ref, sreo_ref, simo_ref), refs = refs[:5], refs[5:]
    cast_out, (xp_scr, a_scr, b_scr, s_scr) = refs[:n_cast], refs[n_cast:]
    rows = a_scr.shape[0]
    tt = rows // nb
    DL = ho_ref.shape[-1]
    HS = d_ref.shape[-1] // 2
    HN = sreo_ref.shape[-1] // 2

    @pl.when(pl.program_id(0) == 0)
    def _():
        convo_ref[...] = conv0_ref[...]
        ho_ref[...] = h0_ref[...]
        sreo_ref[...] = sre0_ref[...]
        simo_ref[...] = sim0_ref[...]

    _cast_blocks(cast_in, cast_out)
    x = (jnp.concatenate([x_ref[s] for s in range(x_ref.shape[0])], axis=-1) if x_slab
         else x_ref[...])
    D = x.shape[-1]
    h = _norm_modulate(x, nb, gn_ref[...], mod_ref[0][None], mod_ref[1][None])
    z = jnp.dot(h.astype(BF16), win_ref[...], preferred_element_type=F32)
    xb = z[:, :DL]
    yb = z[:, DL:2 * DL]
    u = z[:, 2 * DL:]

    halo = (CONV_W - 1) * nb
    xp_scr[0:halo, :] = convo_ref[...]
    xp_scr[halo:halo + rows, :] = xb
    cw = convw_ref[...]
    xc = convb_ref[...] + sum(xp_scr[k * nb:k * nb + rows, :] * cw[k:k + 1, :]
                              for k in range(CONV_W))
    convo_ref[...] = xp_scr[rows:rows + halo, :]

    xcb = xc.astype(BF16)
    gq = [jnp.dot(xcb[:, q * DL // 2:(q + 1) * DL // 2], wg_ref[q], preferred_element_type=F32)
          for q in range(2)]
    bg = bg_ref[...]
    r = jax.nn.sigmoid(jnp.concatenate([g[:, :DL // 2] for g in gq], axis=-1) + bg[:, :DL])
    ig = jax.nn.sigmoid(jnp.concatenate([g[:, DL // 2:] for g in gq], axis=-1) + bg[:, DL:])
    log_a = -C_GATE * r * _softplus(-lam_ref[...])
    a = jnp.exp(log_a)
    mult = _sqrt_nonneg(-jnp.tanh(log_a) * (a * a + 1.0))
    a_scr[...] = a
    b_scr[...] = mult * ig * xc

    for r0 in range(0, nb, SCAN_ROWS):
        hcur = ho_ref[r0:r0 + SCAN_ROWS, :]
        for t in range(tt):
            sl = slice(t * nb + r0, t * nb + r0 + SCAN_ROWS)
            hcur = a_scr[sl, :] * hcur + b_scr[sl, :]
            b_scr[sl, :] = hcur
        ho_ref[r0:r0 + SCAN_ROWS, :] = hcur
    y_lru = jax.nn.gelu(yb) * b_scr[...]

    ub = u.astype(BF16)
    for hf in range(2):
        for part in range(2):
            s_scr[hf, :, part * HN:(part + 1) * HN] = jnp.dot(
                ub[:, hf * HS:(hf + 1) * HS], bmat_ref[part, hf], preferred_element_type=F32)

    for hf in range(2):
        for c0 in range(0, HN, SCAN_LANES):
            st_l = slice(hf * HN + c0, hf * HN + c0 + SCAN_LANES)
            re_l = slice(c0, c0 + SCAN_LANES)
            im_l = slice(HN + c0, HN + c0 + SCAN_LANES)
            a_r = jnp.broadcast_to(abr_ref[:, st_l], (SCAN_ROWS, SCAN_LANES))
            a_i = jnp.broadcast_to(abi_ref[:, st_l], (SCAN_ROWS, SCAN_LANES))
            for r0 in range(0, nb, SCAN_ROWS):
                s_re = sreo_ref[r0:r0 + SCAN_ROWS, st_l]
                s_im = simo_ref[r0:r0 + SCAN_ROWS, st_l]
                for t in range(tt):
                    sl = slice(t * nb + r0, t * nb + r0 + SCAN_ROWS)
                    n_re = a_r * s_re - a_i * s_im + s_scr[hf, sl, re_l]
                    n_im = a_r * s_im + a_i * s_re + s_scr[hf, sl, im_l]
                    s_re, s_im = n_re, n_im
                    s_scr[hf, sl, re_l] = s_re
                    s_scr[hf, sl, im_l] = s_im
                sreo_ref[r0:r0 + SCAN_ROWS, st_l] = s_re
                simo_ref[r0:r0 + SCAN_ROWS, st_l] = s_im

    ys = []
    for hf in range(2):
        sb = s_scr[hf].astype(BF16)
        ys.append(jnp.dot(sb[:, :HN], cmat_ref[hf, 0], preferred_element_type=F32)
                  - jnp.dot(sb[:, HN:], cmat_ref[hf, 1], preferred_element_type=F32))
    ys = jnp.concatenate(ys, axis=-1) + d_ref[...] * u
    g = jax.nn.gelu(ys)
    y_s5 = g * jax.nn.sigmoid(
        jnp.dot(g.astype(BF16), wglu_ref[...], preferred_element_type=F32) + bglu_ref[...])

    ycat = jnp.concatenate([y_lru, y_s5], axis=-1).astype(BF16)
    out = jnp.dot(ycat, wout_ref[...], preferred_element_type=F32)
    xo = (_rows3(x, nb) + mod_ref[2][None] * _rows3(out, nb)).reshape(rows, D)
    if out_slab:
        for s, slab in enumerate(_slabs(xo)):
            xo_ref[s] = slab
    else:
        xo_ref[...] = xo


def _mixer_call(x, mod, layer, mod_rowblk, nb, init_states, prev_states, wts, cast_jobs, out_slab,
                name):
    x_slab = x.ndim == 3
    rows, D = (x.shape[1], x.shape[0] * x.shape[2]) if x_slab else x.shape
    tm = min(MIXER_ROWS, rows)
    assert rows % tm == 0 and tm % nb == 0
    slab_tile = pl.BlockSpec((D // LANES, tm, LANES), lambda i: (0, i, 0))
    flat_tile = pl.BlockSpec((tm, D), lambda i: (i, 0))
    L, _, DL = init_states[1].shape
    GN = init_states[2].shape[-1]
    halo = (CONV_W - 1) * nb
    assert tm // nb >= CONV_W - 1

    n_lead = 2 + len(init_states)
    state_out_specs = [pl.BlockSpec((None,) + s.shape[1:], lambda i: (layer, 0, 0))
                       for s in prev_states]
    grid = rows // tm
    n_cast_steps = _cast_steps(cast_jobs, grid)
    cast_in, cast_out, cast_shapes = _cast_specs(cast_jobs, n_cast_steps,
                                                 lambda i: i // (grid // n_cast_steps))
    wt_spec = lambda w: _const_spec(w.shape, (0, 0)) if w.ndim == 2 else _layer_spec(w, layer)
    outs = pl.pallas_call(
        functools.partial(_mixer_kernel, n_cast=len(cast_jobs), nb=nb, x_slab=x_slab,
                          out_slab=out_slab),
        grid=(grid,),
        in_specs=[slab_tile if x_slab else flat_tile,
                  _const_spec((None, 3, nb, D), (layer, 1, mod_rowblk, 0))]
                 + [_layer_spec(s, layer) for s in init_states]
                 + [pl.BlockSpec(memory_space=pl.ANY)] * len(prev_states)
                 + [wt_spec(w) for w in wts] + cast_in,
        out_specs=[slab_tile if out_slab else flat_tile] + state_out_specs + cast_out,
        out_shape=[jax.ShapeDtypeStruct((D // LANES, rows, LANES) if out_slab else (rows, D),
                                        F32)]
                  + [jax.ShapeDtypeStruct(s.shape, F32) for s in prev_states] + cast_shapes,
        input_output_aliases={n_lead + k: 1 + k for k in range(len(prev_states))},
        scratch_shapes=[pltpu.VMEM((tm + halo, DL), F32),
                        pltpu.VMEM((tm, DL), F32),
                        pltpu.VMEM((tm, DL), F32),
                        pltpu.VMEM((2, tm, GN), F32)],
        compiler_params=_params(("arbitrary",)),
        name=name,
    )(x, mod, *init_states, *prev_states, *wts, *[w for w, _ in cast_jobs])
    n_st = len(prev_states)
    return outs[0], tuple(outs[1:1 + n_st]), tuple(outs[1 + n_st:])


def kernel(x_prompt, x_sample, c_prompt, c_sample, state_lru_conv, state_lru_h, state_s5_re, state_s5_im, w_ada, b_ada, norm_ffn1, w1_ffn1, w3_ffn1, w2_ffn1, norm_mix, w_in, conv_w, conv_b, w_rg, b_rg, w_ig, b_ig, lru_lambda, s5_a_re, s5_a_im, s5_log_dt, s5_b_re, s5_b_im, s5_c_re, s5_c_im, s5_d, w_glu, b_glu, w_out, norm_ffn2, w1_ffn2, w3_ffn2, w2_ffn2, norm_final):
    L, D, _ = w_ada.shape
    Bp, Tp, _ = x_prompt.shape
    Bs, Ts, _ = x_sample.shape
    _, G, N = s5_a_re.shape
    DL = lru_lambda.shape[-1]
    GN = G * N
    halo = CONV_W - 1
    assert Bs % Bp == 0 and Bp % SCAN_ROWS == 0 and G % 2 == 0

    mod = _ada_call(jnp.concatenate([c_sample, c_prompt], axis=0), w_ada, b_ada)
    abr, abi, bmat, cmat, wgate = _prep_call(s5_a_re, s5_a_im, s5_log_dt, s5_b_re, s5_b_im,
                                             s5_c_re, s5_c_im, w_rg, w_ig)

    r3 = lambda v: v.reshape(L, 1, v.shape[-1])
    bgate = jnp.concatenate([b_rg, b_ig], axis=-1).reshape(L, 1, 2 * DL)
    gfin = norm_final.reshape(1, D)
    ffn1_f32 = (w1_ffn1, w3_ffn1, w2_ffn1)
    ffn2_f32 = (w1_ffn2, w3_ffn2, w2_ffn2)
    mix_f32 = (w_in, w_glu, w_out)

    def mixer_wts(w_in_b, w_glu_b, w_out_b):
        return [r3(norm_mix), w_in_b, conv_w, r3(conv_b), wgate, bgate, r3(lru_lambda), abr, abi,
                bmat, cmat, r3(s5_d), w_glu_b, r3(b_glu), w_out_b]

    def time_major(x):
        return jnp.swapaxes(x, 0, 1).reshape(x.shape[0] * x.shape[1], D)

    def state_shapes(nb):
        return ((L, halo * nb, DL), (L, nb, DL), (L, nb, GN), (L, nb, GN))

    init_p = tuple(jnp.zeros(s, F32) for s in state_shapes(Bp))
    init_s = (jnp.swapaxes(state_lru_conv, 1, 2).reshape(L, halo * Bs, DL), state_lru_h,
              state_s5_re.reshape(L, Bs, GN), state_s5_im.reshape(L, Bs, GN))
    st_p = tuple(jnp.zeros(s, F32) for s in state_shapes(Bp))
    st_s = tuple(jnp.zeros(s, F32) for s in state_shapes(Bs))
    xp, xs = x_prompt, time_major(x_sample)
    ffn1_b = tuple(w[0].astype(BF16) for w in ffn1_f32)
    for l in range(L):
        xp, xs, *mix_b = _ffn_call(xp, xs, Bp, Bs, mod, l, 0, r3(norm_ffn1), *ffn1_b, gfin,
                                   False, "st" if l == 0 else "tt", [(w, l) for w in mix_f32],
                                   f"ffn1_{l}")
        xp, st_p, ffn2_b = _mixer_call(xp, mod, l, Bs // Bp, Bp, init_p, st_p, mixer_wts(*mix_b),
                                       [(w, l) for w in ffn2_f32], l == L - 1, f"mixer_p{l}")
        xs, st_s, _ = _mixer_call(xs, mod, l, 0, Bs, init_s, st_s, mixer_wts(*mix_b), [], False,
                                  f"mixer_s{l}")
        jobs = [(w, l + 1) for w in ffn1_f32] if l + 1 < L else []
        xp, xs, *ffn1_b = _ffn_call(xp, xs, Bp, Bs, mod, l, 2, r3(norm_ffn2), *ffn2_b, gfin,
                                    l == L - 1, "ts" if l == L - 1 else "tt", jobs, f"ffn2_{l}")

    def finish(y, st, nb):
        conv, hh, sre, sim = st
        return (y, jnp.swapaxes(conv.reshape(L, halo, nb, DL), 1, 2), hh,
                sre.reshape(L, nb, G, N), sim.reshape(L, nb, G, N))

    p_out = finish(xp, st_p, Bp)
    s_out = finish(jnp.swapaxes(xs.reshape(Ts, Bs, D), 0, 1), st_s, Bs)
    return (p_out[0], s_out[0]) + p_out[1:] + s_out[1:]
```

```python
import functools

import jax
import jax.numpy as jnp
from jax import lax
from jax.experimental import pallas as pl
from jax.experimental.pallas import tpu as pltpu

EPS = 1e-6
C_GATE = 8.0
FFN_RES = 0.5
CONV_W = 4
N_MOD = 9
S5_A_RE_MAX = -1e-4
BF16 = jnp.bfloat16
F32 = jnp.float32

VMEM_LIMIT_BYTES = 56 * 1024 * 1024
FFN_ROWS = 1024
FFN_SUB_ROWS = 512
MIXER_ROWS = 512
SCAN_ROWS = 8
SCAN_LANES = 512
LANES = 128
BF16_ROWS = 16


def _const_spec(shape, index):
    return pl.BlockSpec(shape, lambda i: index, pipeline_mode=pl.Buffered(1))


def _layer_spec(w, layer):
    return _const_spec((None,) + w.shape[1:], (layer,) + (0,) * (w.ndim - 1))


def _params(semantics):
    return pltpu.CompilerParams(dimension_semantics=semantics,
                                vmem_limit_bytes=VMEM_LIMIT_BYTES)


def _cast_steps(jobs, grid):
    n = grid
    while n > 1 and (grid % n or any(w.shape[1] % (BF16_ROWS * n) for w, _ in jobs)):
        n -= 1
    return n


def _cast_specs(jobs, n_steps, step_of):
    ins, outs, shapes = [], [], []
    for w, layer in jobs:
        _, R, C = w.shape
        assert R % (BF16_ROWS * n_steps) == 0
        ins.append(pl.BlockSpec((None, R // n_steps, C),
                                lambda i, layer=layer: (layer, step_of(i), 0)))
        outs.append(pl.BlockSpec((R // n_steps, C), lambda i: (step_of(i), 0)))
        shapes.append(jax.ShapeDtypeStruct((R, C), BF16))
    return ins, outs, shapes


def _cast_blocks(in_refs, out_refs):
    for src_ref, dst_ref in zip(in_refs, out_refs):
        dst_ref[...] = src_ref[...].astype(BF16)


def _rmsnorm(x, g):
    return x * lax.rsqrt(jnp.mean(x * x, axis=-1, keepdims=True) + EPS) * g


def _norm_modulate(x, inner, g, shift, scale):
    xn = x * lax.rsqrt(jnp.mean(x * x, axis=-1, keepdims=True) + EPS)
    xn = xn.reshape(x.shape[0] // inner, inner, x.shape[1])
    return (xn * (g * (1.0 + scale)) + shift).reshape(x.shape)


def _sqrt_nonneg(v):
    return jnp.where(v > 0.0, v * lax.rsqrt(v), v)


def _rows3(v, nb):
    return v.reshape(v.shape[0] // nb, nb, v.shape[1])


def _ada_kernel(c_ref, w_ref, b_ref, o_ref):
    c = c_ref[...]
    s = (c * jax.nn.sigmoid(c)).astype(BF16)
    D = c.shape[-1]
    for k in range(o_ref.shape[0]):
        w = w_ref[:, k * D:(k + 1) * D].astype(BF16)
        o_ref[k] = jnp.dot(s, w, preferred_element_type=F32) + b_ref[k]


def _ada_call(c_all, w_ada, b_ada):
    L, D, _ = w_ada.shape
    nseq = c_all.shape[0]
    n_sub = N_MOD // 3
    return pl.pallas_call(
        _ada_kernel,
        grid=(L, n_sub),
        in_specs=[
            pl.BlockSpec((nseq, D), lambda l, k: (0, 0)),
            pl.BlockSpec((None, D, 3 * D), lambda l, k: (l, 0, k)),
            pl.BlockSpec((None, 3, 1, D), lambda l, k: (l, k, 0, 0)),
        ],
        out_specs=pl.BlockSpec((None, 3, nseq, D), lambda l, k: (l, k, 0, 0)),
        out_shape=jax.ShapeDtypeStruct((L, N_MOD, nseq, D), F32),
        compiler_params=_params(("arbitrary", "arbitrary")),
        name="adaln_mod",
    )(c_all, w_ada, b_ada.reshape(L, N_MOD, 1, D))


def _block_diag_rows(t, nblk):
    r, width = t.shape
    c = width // nblk
    tiled = jnp.concatenate([t] * nblk, axis=0)
    row_blk = lax.broadcasted_iota(jnp.int32, tiled.shape, 0) // r
    col_blk = lax.broadcasted_iota(jnp.int32, tiled.shape, 1) // c
    return jnp.where(row_blk == col_blk, tiled, 0.0)


def _prep_kernel(are_ref, aim_ref, ldt_ref, bt_ref, ct_ref, wt_ref,
                 abr_ref, abi_ref, bmat_ref, cmat_ref, wg_ref):
    ar = jnp.minimum(are_ref[...], S5_A_RE_MAX)
    ai = aim_ref[...]
    dt = jnp.exp(ldt_ref[...])
    mag = jnp.exp(ar * dt)
    abr = mag * jnp.cos(ai * dt)
    abi = mag * jnp.sin(ai * dt)
    abr_ref[...] = abr
    abi_ref[...] = abi
    den = ar * ar + ai * ai
    f_r = ((abr - 1.0) * ar + abi * ai) / den
    f_i = (abi * ar - (abr - 1.0) * ai) / den
    b_r = bt_ref[0]
    b_i = bt_ref[1]
    bb = (f_r * b_r - f_i * b_i, f_r * b_i + f_i * b_r)
    hn = bb[0].shape[-1] // 2
    nblk = bmat_ref.shape[-2] // bb[0].shape[0]
    for part in range(2):
        for hf in range(2):
            cols = slice(hf * hn, (hf + 1) * hn)
            bmat_ref[part, hf] = _block_diag_rows(bb[part][:, cols], nblk).astype(BF16)
            cmat_ref[hf, part] = _block_diag_rows(ct_ref[part][:, cols], nblk).T.astype(BF16)
    heads = wg_ref.shape[0] // wt_ref.shape[1]
    dl = wg_ref.shape[0]
    for gate in range(2):
        wg_ref[:, gate * dl:(gate + 1) * dl] = _block_diag_rows(wt_ref[gate], heads).astype(BF16)


def _prep_call(a_re, a_im, log_dt, b_re, b_im, c_re, c_im, w_rg, w_ig):
    L, G, N = a_re.shape
    J = b_re.shape[-1]
    H, HD, _ = w_rg.shape[1:]
    GN, HN, DL = G * N, G * N // 2, H * HD
    row = lambda v: v.reshape(L, 1, GN)
    ldt = jnp.broadcast_to(log_dt[:, :, None], (L, G, N))
    bt = jnp.stack([b_re, b_im], axis=1).transpose(0, 1, 4, 2, 3).reshape(L, 2, J, GN)
    ct = jnp.stack([c_re, c_im], axis=1).transpose(0, 1, 3, 2, 4).reshape(L, 2, J, GN)
    wt = jnp.stack([w_rg, w_ig], axis=1).transpose(0, 1, 3, 2, 4).reshape(L, 2, HD, DL)
    rspec = pl.BlockSpec((None, 1, GN), lambda l: (l, 0, 0))
    jspec = pl.BlockSpec((None, 2, J, GN), lambda l: (l, 0, 0, 0))
    return pl.pallas_call(
        _prep_kernel,
        grid=(L,),
        in_specs=[rspec, rspec, rspec, jspec, jspec,
                  pl.BlockSpec((None, 2, HD, DL), lambda l: (l, 0, 0, 0))],
        out_specs=[rspec, rspec,
                   pl.BlockSpec((None, 2, 2, G // 2 * J, HN), lambda l: (l, 0, 0, 0, 0)),
                   pl.BlockSpec((None, 2, 2, HN, G // 2 * J), lambda l: (l, 0, 0, 0, 0)),
                   pl.BlockSpec((None, DL, 2 * DL), lambda l: (l, 0, 0))],
        out_shape=[jax.ShapeDtypeStruct((L, 1, GN), F32), jax.ShapeDtypeStruct((L, 1, GN), F32),
                   jax.ShapeDtypeStruct((L, 2, 2, G // 2 * J, HN), BF16),
                   jax.ShapeDtypeStruct((L, 2, 2, HN, G // 2 * J), BF16),
                   jax.ShapeDtypeStruct((L, DL, 2 * DL), BF16)],
        compiler_params=_params(("arbitrary",)),
        name="param_prep",
    )(row(a_re), row(a_im), row(ldt), bt, ct, wt)


def _ffn_matmuls(x, h, inner, gate, w1_ref, w3_ref, w2_ref, gf_ref, final_norm):
    rows3 = lambda v: v.reshape(v.shape[0] // inner, inner, v.shape[1])
    a = jnp.dot(h, w1_ref[...], preferred_element_type=F32)
    b = jnp.dot(h, w3_ref[...], preferred_element_type=F32)
    act = (a * jax.nn.sigmoid(a) * b).astype(BF16)
    f = jnp.dot(act, w2_ref[...], preferred_element_type=F32)
    out = (rows3(x) + (FFN_RES * gate) * rows3(f)).reshape(x.shape)
    if final_norm:
        out = _rmsnorm(out, gf_ref[...])
    return out


def _ffn_rows(x_ref, mod_ref, o_ref, slab_scr, h_scr, wts, nb, final_norm, layout):
    D = x_ref.shape[-1]
    rows = o_ref.shape[0] * o_ref.shape[1] if layout == "ts" else o_ref.shape[0]
    steps = rows // nb
    sub = min(FFN_SUB_ROWS, rows)
    n_slab = D // LANES
    g_ref, w1_ref, w3_ref, w2_ref, gf_ref = wts

    def tile(r0):
        if layout == "st":
            b0, nseq = r0 // steps, sub // steps
            x = x_ref[b0:b0 + nseq].reshape(sub, D)
            return x, steps, [mod_ref[k, b0:b0 + nseq][:, None, :] for k in range(3)]
        return x_ref[r0:r0 + sub, :], nb, [mod_ref[k][None] for k in range(3)]

    for r0 in range(0, rows, sub):
        x, inner, (shift, scale, _) = tile(r0)
        h_scr[r0:r0 + sub, :] = _norm_modulate(x, inner, g_ref[...], shift, scale).astype(BF16)
    for r0 in range(0, rows, sub):
        x, inner, (_, _, gate) = tile(r0)
        out = _ffn_matmuls(x, h_scr[r0:r0 + sub, :], inner, gate, w1_ref, w3_ref, w2_ref,
                           gf_ref, final_norm)
        if layout == "st":
            b0, nseq = r0 // steps, sub // steps
        if layout == "tt":
            o_ref[r0:r0 + sub, :] = out
        elif layout == "st":
            for b in range(b0, b0 + nseq):
                seq = out[(b - b0) * steps:(b - b0 + 1) * steps]
                for s in range(n_slab):
                    slab_scr[s, pl.ds(b, steps, stride=nb), :] = seq[:, s * LANES:(s + 1) * LANES]
        else:
            for s in range(n_slab):
                slab_scr[s, r0:r0 + sub, :] = out[:, s * LANES:(s + 1) * LANES]
            t0, nt = r0 // nb, sub // nb
            for b in range(nb):
                for s in range(n_slab):
                    o_ref[b, t0:t0 + nt, s * LANES:(s + 1) * LANES] = (
                        slab_scr[s, pl.ds(r0 + b, nt, stride=nb), :])
    if layout == "st":
        for s in range(n_slab):
            o_ref[:, s * LANES:(s + 1) * LANES] = slab_scr[s]


def _ffn_kernel(*refs, layer, n_cast, nbp, nbs, n_prompt, final_norm, prompt_layout):
    (xp_ref, xs_ref, modp_ref, mods_ref, g_ref, w1_ref, w3_ref, w2_ref, gf_ref), refs = (
        refs[:9], refs[9:])
    cast_in, (op_ref, os_ref), cast_out, (slab_scr, h_scr) = (
        refs[:n_cast], refs[n_cast:n_cast + 2], refs[n_cast + 2:2 * n_cast + 2], refs[-2:])
    wts = (g_ref.at[layer:layer + 1], w1_ref, w3_ref, w2_ref, gf_ref)

    @pl.when(pl.program_id(0) < n_prompt)
    def _():
        _cast_blocks(cast_in, cast_out)
        _ffn_rows(xp_ref, modp_ref, op_ref, slab_scr, h_scr, wts, nbp, final_norm, prompt_layout)

    @pl.when(pl.program_id(0) == n_prompt)
    def _():
        _ffn_rows(xs_ref, mods_ref, os_ref, slab_scr, h_scr, wts, nbs, final_norm, "tt")


def _ffn_call(xp, xs, nbp, nbs, mod, layer, mod_group, g, w1, w3, w2, gf, final_norm,
              prompt_layout, cast_jobs, name):
    D = xp.shape[-1]
    rows_p = xp.size // D
    rows_s = xs.shape[0]
    tm = min(FFN_ROWS, rows_p)
    assert rows_p % tm == 0 and tm % nbp == 0 and rows_s % nbs == 0 and nbs % nbp == 0
    assert D % LANES == 0 and FFN_SUB_ROWS % (tm // nbp) == 0
    n_prompt = rows_p // tm
    tile_tm = pl.BlockSpec((tm, D), lambda i: (jnp.minimum(i, n_prompt - 1), 0))
    tile_sm = pl.BlockSpec((nbp, tm // nbp, D), lambda i: (0, jnp.minimum(i, n_prompt - 1), 0))
    shape_tm = jax.ShapeDtypeStruct((rows_p, D), F32)
    shape_sm = jax.ShapeDtypeStruct((nbp, rows_p // nbp, D), F32)
    n_cast_steps = _cast_steps(cast_jobs, n_prompt)
    cast_in, cast_out, cast_shapes = _cast_specs(
        cast_jobs, n_cast_steps,
        lambda i: jnp.minimum(i, n_prompt - 1) // (n_prompt // n_cast_steps))
    return pl.pallas_call(
        functools.partial(_ffn_kernel, layer=layer, n_cast=len(cast_jobs), nbp=nbp, nbs=nbs,
                          n_prompt=n_prompt, final_norm=final_norm, prompt_layout=prompt_layout),
        grid=(n_prompt + 1,),
        in_specs=[
            tile_sm if prompt_layout == "st" else tile_tm,
            _const_spec((rows_s, D), (0, 0)),
            _const_spec((None, 3, nbp, D), (layer, mod_group, nbs // nbp, 0)),
            _const_spec((None, 3, nbs, D), (layer, mod_group, 0, 0)),
            _const_spec(g.shape, (0, 0)), _const_spec(w1.shape, (0, 0)),
            _const_spec(w3.shape, (0, 0)),
            _const_spec(w2.shape, (0, 0)),
            _const_spec((1, D), (0, 0)),
        ] + cast_in,
        out_specs=[tile_sm if prompt_layout == "ts" else tile_tm,
                   pl.BlockSpec((rows_s, D), lambda i: (0, 0))] + cast_out,
        out_shape=[shape_sm if prompt_layout == "ts" else shape_tm,
                   jax.ShapeDtypeStruct((rows_s, D), F32)] + cast_shapes,
        scratch_shapes=[pltpu.VMEM((D // LANES, tm, LANES), F32), pltpu.VMEM((tm, D), BF16)],
        compiler_params=_params(("arbitrary",)),
        name=name,
    )(xp, xs, mod, mod, g, w1, w3, w2, gf, *[w for w, _ in cast_jobs])


def _softplus(v):
    return jnp.maximum(v, 0.0) + jnp.log1p(jnp.exp(-jnp.abs(v)))


def _mixer_kernel(*refs, layer, n_init, n_prev, n_cast, nb):
    (x_ref, mod_ref), refs = refs[:2], refs[2:]
    init_refs, refs = refs[:n_init], refs[n_init + n_prev:]
    (gn_ref, win_ref, convw_ref, convb_ref, wg_ref, brg_ref, big_ref, lam_ref,
     abr_ref, abi_ref, bmat_ref, cmat_ref, d_ref, wglu_ref, bglu_ref, wout_ref), refs = (
        refs[:16], refs[16:])
    cast_in, refs = refs[:n_cast], refs[n_cast:]
    (xo_ref, *state_refs), refs = refs[:5], refs[5:]
    convo_ref, ho_ref, sreo_ref, simo_ref = state_refs
    cast_out, (xp_scr, a_scr, b_scr, s_scr) = refs[:n_cast], refs[n_cast:]
    vec = lambda ref: ref[layer:layer + 1, :]
    rows, D = x_ref.shape
    tt = rows // nb
    DL = ho_ref.shape[-1]
    HS = d_ref.shape[-1] // 2
    HN = sreo_ref.shape[-1] // 2

    @pl.when(pl.program_id(0) == 0)
    def _():
        for k, st_ref in enumerate(state_refs):
            st_ref[...] = init_refs[k][...] if init_refs else jnp.zeros(st_ref.shape, F32)

    _cast_blocks(cast_in, cast_out)
    x = x_ref[...]
    h = _norm_modulate(x, nb, vec(gn_ref), mod_ref[0][None], mod_ref[1][None])
    z = jnp.dot(h.astype(BF16), win_ref[...], preferred_element_type=F32)
    xb = z[:, :DL]
    yb = z[:, DL:2 * DL]
    u = z[:, 2 * DL:]

    halo = (CONV_W - 1) * nb
    xp_scr[0:halo, :] = convo_ref[...]
    xp_scr[halo:halo + rows, :] = xb
    cw = convw_ref[...]
    xc = vec(convb_ref) + sum(xp_scr[k * nb:k * nb + rows, :] * cw[k:k + 1, :]
                              for k in range(CONV_W))
    convo_ref[...] = xp_scr[rows:rows + halo, :]

    gpre = jnp.dot(xc.astype(BF16), wg_ref[...], preferred_element_type=F32)
    r = jax.nn.sigmoid(gpre[:, :DL] + vec(brg_ref))
    ig = jax.nn.sigmoid(gpre[:, DL:] + vec(big_ref))
    log_a = -C_GATE * r * _softplus(-vec(lam_ref))
    a = jnp.exp(log_a)
    mult = _sqrt_nonneg(-jnp.tanh(log_a) * (a * a + 1.0))
    a_scr[...] = a
    b_scr[...] = mult * ig * xc

    for r0 in range(0, nb, SCAN_ROWS):
        hcur = ho_ref[r0:r0 + SCAN_ROWS, :]
        for t in range(tt):
            sl = slice(t * nb + r0, t * nb + r0 + SCAN_ROWS)
            hcur = a_scr[sl, :] * hcur + b_scr[sl, :]
            b_scr[sl, :] = hcur
        ho_ref[r0:r0 + SCAN_ROWS, :] = hcur
    y_lru = jax.nn.gelu(yb) * b_scr[...]

    ub = u.astype(BF16)
    for hf in range(2):
        for part in range(2):
            s_scr[hf, :, part * HN:(part + 1) * HN] = jnp.dot(
                ub[:, hf * HS:(hf + 1) * HS], bmat_ref[part, hf], preferred_element_type=F32)

    for hf in range(2):
        for c0 in range(0, HN, SCAN_LANES):
            st_l = slice(hf * HN + c0, hf * HN + c0 + SCAN_LANES)
            re_l = slice(c0, c0 + SCAN_LANES)
            im_l = slice(HN + c0, HN + c0 + SCAN_LANES)
            a_r = jnp.broadcast_to(abr_ref[:, st_l], (SCAN_ROWS, SCAN_LANES))
            a_i = jnp.broadcast_to(abi_ref[:, st_l], (SCAN_ROWS, SCAN_LANES))
            for r0 in range(0, nb, SCAN_ROWS):
                s_re = sreo_ref[r0:r0 + SCAN_ROWS, st_l]
                s_im = simo_ref[r0:r0 + SCAN_ROWS, st_l]
                for t in range(tt):
                    sl = slice(t * nb + r0, t * nb + r0 + SCAN_ROWS)
                    n_re = a_r * s_re - a_i * s_im + s_scr[hf, sl, re_l]
                    n_im = a_r * s_im + a_i * s_re + s_scr[hf, sl, im_l]
                    s_re, s_im = n_re, n_im
                    s_scr[hf, sl, re_l] = s_re
                    s_scr[hf, sl, im_l] = s_im
                sreo_ref[r0:r0 + SCAN_ROWS, st_l] = s_re
                simo_ref[r0:r0 + SCAN_ROWS, st_l] = s_im

    ys = []
    for hf in range(2):
        sb = s_scr[hf].astype(BF16)
        ys.append(jnp.dot(sb[:, :HN], cmat_ref[hf, 0], preferred_element_type=F32)
                  - jnp.dot(sb[:, HN:], cmat_ref[hf, 1], preferred_element_type=F32))
    ys = jnp.concatenate(ys, axis=-1) + vec(d_ref) * u
    g = jax.nn.gelu(ys)
    y_s5 = g * jax.nn.sigmoid(
        jnp.dot(g.astype(BF16), wglu_ref[...], preferred_element_type=F32) + vec(bglu_ref))

    ycat = jnp.concatenate([y_lru, y_s5], axis=-1).astype(BF16)
    out = jnp.dot(ycat, wout_ref[...], preferred_element_type=F32)
    xo_ref[...] = (_rows3(x, nb) + mod_ref[2][None] * _rows3(out, nb)).reshape(rows, D)


def _mixer_call(x, mod, layer, mod_rowblk, nb, init_states, prev_states, state_shapes, wts,
                cast_jobs, name):
    rows, D = x.shape
    tm = min(MIXER_ROWS, rows)
    assert rows % tm == 0 and tm % nb == 0 and tm // nb >= CONV_W - 1
    DL, GN = state_shapes[1][-1], state_shapes[2][-1]
    halo = (CONV_W - 1) * nb

    def wt_spec(w, kind):
        return _layer_spec(w, layer) if kind == "stacked" else _const_spec(w.shape, (0,) * w.ndim)

    n_lead = 2 + len(init_states)
    state_out_specs = [pl.BlockSpec((None,) + s[1:], lambda i: (layer, 0, 0))
                       for s in state_shapes]
    grid = rows // tm
    n_cast_steps = _cast_steps(cast_jobs, grid)
    cast_in, cast_out, cast_shapes = _cast_specs(cast_jobs, n_cast_steps,
                                                 lambda i: i // (grid // n_cast_steps))
    outs = pl.pallas_call(
        functools.partial(_mixer_kernel, layer=layer, n_init=len(init_states),
                          n_prev=len(prev_states), n_cast=len(cast_jobs), nb=nb),
        grid=(grid,),
        in_specs=[pl.BlockSpec((tm, D), lambda i: (i, 0)),
                  _const_spec((None, 3, nb, D), (layer, 1, mod_rowblk, 0))]
                 + [_layer_spec(s, layer) for s in init_states]
                 + [pl.BlockSpec(memory_space=pl.ANY)] * len(prev_states)
                 + [wt_spec(w, kind) for w, kind in wts] + cast_in,
        out_specs=[pl.BlockSpec((tm, D), lambda i: (i, 0))] + state_out_specs + cast_out,
        out_shape=[jax.ShapeDtypeStruct((rows, D), F32)]
                  + [jax.ShapeDtypeStruct(s, F32) for s in state_shapes] + cast_shapes,
        input_output_aliases={n_lead + k: 1 + k for k in range(len(prev_states))},
        scratch_shapes=[pltpu.VMEM((tm + halo, DL), F32),
                        pltpu.VMEM((tm, DL), F32),
                        pltpu.VMEM((tm, DL), F32),
                        pltpu.VMEM((2, tm, GN), F32)],
        compiler_params=_params(("arbitrary",)),
        name=name,
    )(x, mod, *init_states, *prev_states, *[w for w, _ in wts], *[w for w, _ in cast_jobs])
    n_st = len(state_shapes)
    return outs[0], tuple(outs[1:1 + n_st]), tuple(outs[1 + n_st:])


def kernel(x_prompt, x_sample, c_prompt, c_sample, state_lru_conv, state_lru_h, state_s5_re, state_s5_im, w_ada, b_ada, norm_ffn1, w1_ffn1, w3_ffn1, w2_ffn1, norm_mix, w_in, conv_w, conv_b, w_rg, b_rg, w_ig, b_ig, lru_lambda, s5_a_re, s5_a_im, s5_log_dt, s5_b_re, s5_b_im, s5_c_re, s5_c_im, s5_d, w_glu, b_glu, w_out, norm_ffn2, w1_ffn2, w3_ffn2, w2_ffn2, norm_final):
    L, D, _ = w_ada.shape
    Bp, Tp, _ = x_prompt.shape
    Bs, Ts, _ = x_sample.shape
    _, G, N = s5_a_re.shape
    DL = lru_lambda.shape[-1]
    GN = G * N
    halo = CONV_W - 1
    assert Bs % Bp == 0 and Bp % SCAN_ROWS == 0 and G % 2 == 0

    mod = _ada_call(jnp.concatenate([c_sample, c_prompt], axis=0), w_ada, b_ada)
    abr, abi, bmat, cmat, wgate = _prep_call(s5_a_re, s5_a_im, s5_log_dt, s5_b_re, s5_b_im,
                                             s5_c_re, s5_c_im, w_rg, w_ig)

    gfin = norm_final.reshape(1, D)
    ffn1_f32 = (w1_ffn1, w3_ffn1, w2_ffn1)
    ffn2_f32 = (w1_ffn2, w3_ffn2, w2_ffn2)
    mix_f32 = (w_in, w_glu, w_out)

    def mixer_wts(w_in_b, w_glu_b, w_out_b):
        return [(norm_mix, "rows"), (w_in_b, "layer"), (conv_w, "stacked"), (conv_b, "rows"),
                (wgate, "stacked"), (b_rg, "rows"), (b_ig, "rows"), (lru_lambda, "rows"),
                (abr, "stacked"), (abi, "stacked"), (bmat, "stacked"), (cmat, "stacked"),
                (s5_d, "rows"), (w_glu_b, "layer"), (b_glu, "rows"), (w_out_b, "layer")]

    def time_major(x):
        return jnp.swapaxes(x, 0, 1).reshape(x.shape[0] * x.shape[1], D)

    def state_shapes(nb):
        return ((L, halo * nb, DL), (L, nb, DL), (L, nb, GN), (L, nb, GN))

    init_s = (jnp.swapaxes(state_lru_conv, 1, 2).reshape(L, halo * Bs, DL), state_lru_h,
              state_s5_re.reshape(L, Bs, GN), state_s5_im.reshape(L, Bs, GN))
    st_p, st_s = (), ()
    xp, xs = x_prompt, time_major(x_sample)
    ffn1_b = tuple(w[0].astype(BF16) for w in ffn1_f32)
    for l in range(L):
        xp, xs, *mix_b = _ffn_call(xp, xs, Bp, Bs, mod, l, 0, norm_ffn1, *ffn1_b, gfin,
                                   False, "st" if l == 0 else "tt", [(w, l) for w in mix_f32],
                                   f"ffn1_{l}")
        xp, st_p, ffn2_b = _mixer_call(xp, mod, l, Bs // Bp, Bp, (), st_p, state_shapes(Bp),
                                       mixer_wts(*mix_b), [(w, l) for w in ffn2_f32],
                                       f"mixer_p{l}")
        xs, st_s, _ = _mixer_call(xs, mod, l, 0, Bs, init_s, st_s, state_shapes(Bs),
                                  mixer_wts(*mix_b), [], f"mixer_s{l}")
        jobs = [(w, l + 1) for w in ffn1_f32] if l + 1 < L else []
        xp, xs, *ffn1_b = _ffn_call(xp, xs, Bp, Bs, mod, l, 2, norm_ffn2, *ffn2_b, gfin,
                                    l == L - 1, "ts" if l == L - 1 else "tt", jobs, f"ffn2_{l}")

    def finish(y, st, nb):
        conv, hh, sre, sim = st
        return (y, jnp.swapaxes(conv.reshape(L, halo, nb, DL), 1, 2), hh,
                sre.reshape(L, nb, G, N), sim.reshape(L, nb, G, N))

    p_out = finish(xp, st_p, Bp)
    s_out = finish(jnp.swapaxes(xs.reshape(Ts, Bs, D), 0, 1), st_s, Bs)
    return (p_out[0], s_out[0]) + p_out[1:] + s_out[1:]
```

```python
import functools

import jax
import jax.numpy as jnp
from jax import lax
from jax.experimental import pallas as pl
from jax.experimental.pallas import tpu as pltpu

EPS = 1e-6
C_GATE = 8.0
FFN_RES = 0.5
CONV_W = 4
N_MOD = 9
S5_A_RE_MAX = -1e-4
BF16 = jnp.bfloat16
F32 = jnp.float32

VMEM_LIMIT_BYTES = 60 * 1024 * 1024
FFN_ROWS = 1024
FFN_SUB_ROWS = 1024
MIXER_ROWS = 512
SCAN_ROWS = 8
SCAN_LANES = 512
LANES = 128
BF16_ROWS = 16


def _const_spec(shape, index):
    return pl.BlockSpec(shape, lambda i: index, pipeline_mode=pl.Buffered(1))


def _layer_spec(w, layer):
    return _const_spec((None,) + w.shape[1:], (layer,) + (0,) * (w.ndim - 1))


def _params(semantics):
    return pltpu.CompilerParams(dimension_semantics=semantics,
                                vmem_limit_bytes=VMEM_LIMIT_BYTES)


def _cast_steps(jobs, grid):
    n = grid
    while n > 1 and (grid % n or any(w.shape[1] % (BF16_ROWS * n) for w, _ in jobs)):
        n -= 1
    return n


def _cast_specs(jobs, n_steps, step_of):
    ins, outs, shapes = [], [], []
    for w, layer in jobs:
        _, R, C = w.shape
        assert R % (BF16_ROWS * n_steps) == 0
        ins.append(pl.BlockSpec((None, R // n_steps, C),
                                lambda i, layer=layer: (layer, step_of(i), 0)))
        outs.append(pl.BlockSpec((R // n_steps, C), lambda i: (step_of(i), 0)))
        shapes.append(jax.ShapeDtypeStruct((R, C), BF16))
    return ins, outs, shapes


def _cast_blocks(in_refs, out_refs):
    for src_ref, dst_ref in zip(in_refs, out_refs):
        dst_ref[...] = src_ref[...].astype(BF16)


def _rmsnorm(x, g):
    return x * lax.rsqrt(jnp.mean(x * x, axis=-1, keepdims=True) + EPS) * g


def _norm_modulate(x, inner, g, shift, scale):
    xn = x * lax.rsqrt(jnp.mean(x * x, axis=-1, keepdims=True) + EPS)
    xn = xn.reshape(x.shape[0] // inner, inner, x.shape[1])
    return (xn * (g * (1.0 + scale)) + shift).reshape(x.shape)


def _sqrt_nonneg(v):
    return jnp.where(v > 0.0, v * lax.rsqrt(v), v)


def _rows3(v, nb):
    return v.reshape(v.shape[0] // nb, nb, v.shape[1])


def _ada_kernel(c_ref, w_ref, b_ref, o_ref):
    c = c_ref[...]
    s = (c * jax.nn.sigmoid(c)).astype(BF16)
    D = c.shape[-1]
    for k in range(o_ref.shape[0]):
        w = w_ref[:, k * D:(k + 1) * D].astype(BF16)
        o_ref[k] = jnp.dot(s, w, preferred_element_type=F32) + b_ref[k]


def _ada_call(c_all, w_ada, b_ada):
    L, D, _ = w_ada.shape
    nseq = c_all.shape[0]
    n_sub = N_MOD // 3
    return pl.pallas_call(
        _ada_kernel,
        grid=(L, n_sub),
        in_specs=[
            pl.BlockSpec((nseq, D), lambda l, k: (0, 0)),
            pl.BlockSpec((None, D, 3 * D), lambda l, k: (l, 0, k)),
            pl.BlockSpec((None, 3, 1, D), lambda l, k: (l, k, 0, 0)),
        ],
        out_specs=pl.BlockSpec((None, 3, nseq, D), lambda l, k: (l, k, 0, 0)),
        out_shape=jax.ShapeDtypeStruct((L, N_MOD, nseq, D), F32),
        compiler_params=_params(("arbitrary", "arbitrary")),
        name="adaln_mod",
    )(c_all, w_ada, b_ada.reshape(L, N_MOD, 1, D))


def _block_diag_rows(t, nblk):
    r, width = t.shape
    c = width // nblk
    tiled = jnp.concatenate([t] * nblk, axis=0)
    row_blk = lax.broadcasted_iota(jnp.int32, tiled.shape, 0) // r
    col_blk = lax.broadcasted_iota(jnp.int32, tiled.shape, 1) // c
    return jnp.where(row_blk == col_blk, tiled, 0.0)


def _prep_kernel(are_ref, aim_ref, ldt_ref, bt_ref, ct_ref, wt_ref,
                 abr_ref, abi_ref, bmat_ref, cmat_ref, wg_ref):
    ar = jnp.minimum(are_ref[...], S5_A_RE_MAX)
    ai = aim_ref[...]
    dt = jnp.exp(ldt_ref[...])
    mag = jnp.exp(ar * dt)
    abr = mag * jnp.cos(ai * dt)
    abi = mag * jnp.sin(ai * dt)
    abr_ref[...] = abr
    abi_ref[...] = abi
    den = ar * ar + ai * ai
    f_r = ((abr - 1.0) * ar + abi * ai) / den
    f_i = (abi * ar - (abr - 1.0) * ai) / den
    b_r = bt_ref[0]
    b_i = bt_ref[1]
    bb = (f_r * b_r - f_i * b_i, f_r * b_i + f_i * b_r)
    hn = bb[0].shape[-1] // 2
    nblk = bmat_ref.shape[-2] // bb[0].shape[0]
    for part in range(2):
        for hf in range(2):
            cols = slice(hf * hn, (hf + 1) * hn)
            bmat_ref[part, hf] = _block_diag_rows(bb[part][:, cols], nblk).astype(BF16)
            cmat_ref[hf, part] = _block_diag_rows(ct_ref[part][:, cols], nblk).T.astype(BF16)
    heads = wg_ref.shape[0] // wt_ref.shape[1]
    dl = wg_ref.shape[0]
    for gate in range(2):
        wg_ref[:, gate * dl:(gate + 1) * dl] = _block_diag_rows(wt_ref[gate], heads).astype(BF16)


def _prep_call(a_re, a_im, log_dt, b_re, b_im, c_re, c_im, w_rg, w_ig):
    L, G, N = a_re.shape
    J = b_re.shape[-1]
    H, HD, _ = w_rg.shape[1:]
    GN, HN, DL = G * N, G * N // 2, H * HD
    row = lambda v: v.reshape(L, 1, GN)
    ldt = jnp.broadcast_to(log_dt[:, :, None], (L, G, N))
    bt = jnp.stack([b_re, b_im], axis=1).transpose(0, 1, 4, 2, 3).reshape(L, 2, J, GN)
    ct = jnp.stack([c_re, c_im], axis=1).transpose(0, 1, 3, 2, 4).reshape(L, 2, J, GN)
    wt = jnp.stack([w_rg, w_ig], axis=1).transpose(0, 1, 3, 2, 4).reshape(L, 2, HD, DL)
    rspec = pl.BlockSpec((None, 1, GN), lambda l: (l, 0, 0))
    jspec = pl.BlockSpec((None, 2, J, GN), lambda l: (l, 0, 0, 0))
    return pl.pallas_call(
        _prep_kernel,
        grid=(L,),
        in_specs=[rspec, rspec, rspec, jspec, jspec,
                  pl.BlockSpec((None, 2, HD, DL), lambda l: (l, 0, 0, 0))],
        out_specs=[rspec, rspec,
                   pl.BlockSpec((None, 2, 2, G // 2 * J, HN), lambda l: (l, 0, 0, 0, 0)),
                   pl.BlockSpec((None, 2, 2, HN, G // 2 * J), lambda l: (l, 0, 0, 0, 0)),
                   pl.BlockSpec((None, DL, 2 * DL), lambda l: (l, 0, 0))],
        out_shape=[jax.ShapeDtypeStruct((L, 1, GN), F32), jax.ShapeDtypeStruct((L, 1, GN), F32),
                   jax.ShapeDtypeStruct((L, 2, 2, G // 2 * J, HN), BF16),
                   jax.ShapeDtypeStruct((L, 2, 2, HN, G // 2 * J), BF16),
                   jax.ShapeDtypeStruct((L, DL, 2 * DL), BF16)],
        compiler_params=_params(("arbitrary",)),
        name="param_prep",
    )(row(a_re), row(a_im), row(ldt), bt, ct, wt)


def _ffn_matmuls(x, h, inner, gate, w1_ref, w3_ref, w2_ref, gf_ref, final_norm):
    rows3 = lambda v: v.reshape(v.shape[0] // inner, inner, v.shape[1])
    a = jnp.dot(h, w1_ref[...], preferred_element_type=F32)
    b = jnp.dot(h, w3_ref[...], preferred_element_type=F32)
    act = (a * jax.nn.sigmoid(a) * b).astype(BF16)
    f = jnp.dot(act, w2_ref[...], preferred_element_type=F32)
    out = (rows3(x) + (FFN_RES * gate) * rows3(f)).reshape(x.shape)
    if final_norm:
        out = _rmsnorm(out, gf_ref[...])
    return out


def _ffn_rows(x_ref, mod_ref, o_ref, slab_scr, h_scr, wts, nb, final_norm, layout):
    D = x_ref.shape[-1]
    rows = o_ref.shape[0] * o_ref.shape[1] if layout == "ts" else o_ref.shape[0]
    steps = rows // nb
    sub = min(FFN_SUB_ROWS, rows)
    n_slab = D // LANES
    g_ref, w1_ref, w3_ref, w2_ref, gf_ref = wts

    def tile(r0):
        if layout == "st":
            b0, nseq = r0 // steps, sub // steps
            x = x_ref[b0:b0 + nseq].reshape(sub, D)
            return x, steps, [mod_ref[k, b0:b0 + nseq][:, None, :] for k in range(3)]
        return x_ref[r0:r0 + sub, :], nb, [mod_ref[k][None] for k in range(3)]

    for r0 in range(0, rows, sub):
        x, inner, (shift, scale, _) = tile(r0)
        h_scr[r0:r0 + sub, :] = _norm_modulate(x, inner, g_ref[...], shift, scale).astype(BF16)
    for r0 in range(0, rows, sub):
        x, inner, (_, _, gate) = tile(r0)
        out = _ffn_matmuls(x, h_scr[r0:r0 + sub, :], inner, gate, w1_ref, w3_ref, w2_ref,
                           gf_ref, final_norm)
        if layout == "st":
            b0, nseq = r0 // steps, sub // steps
        if layout == "tt":
            o_ref[r0:r0 + sub, :] = out
        elif layout == "st":
            for b in range(b0, b0 + nseq):
                seq = out[(b - b0) * steps:(b - b0 + 1) * steps]
                for s in range(n_slab):
                    slab_scr[s, pl.ds(b, steps, stride=nb), :] = seq[:, s * LANES:(s + 1) * LANES]
        else:
            for s in range(n_slab):
                slab_scr[s, r0:r0 + sub, :] = out[:, s * LANES:(s + 1) * LANES]
            t0, nt = r0 // nb, sub // nb
            for b in range(nb):
                for s in range(n_slab):
                    o_ref[b, t0:t0 + nt, s * LANES:(s + 1) * LANES] = (
                        slab_scr[s, pl.ds(r0 + b, nt, stride=nb), :])
    if layout == "st":
        for s in range(n_slab):
            o_ref[:, s * LANES:(s + 1) * LANES] = slab_scr[s]


def _ffn_kernel(*refs, layer, n_cast, nbp, nbs, n_prompt, final_norm, prompt_layout):
    (xp_ref, xs_ref, modp_ref, mods_ref, g_ref, w1_ref, w3_ref, w2_ref, gf_ref), refs = (
        refs[:9], refs[9:])
    cast_in, (op_ref, os_ref), cast_out, (slab_scr, h_scr) = (
        refs[:n_cast], refs[n_cast:n_cast + 2], refs[n_cast + 2:2 * n_cast + 2], refs[-2:])
    wts = (g_ref.at[layer:layer + 1], w1_ref, w3_ref, w2_ref, gf_ref)

    @pl.when(pl.program_id(0) < n_prompt)
    def _():
        _cast_blocks(cast_in, cast_out)
        _ffn_rows(xp_ref, modp_ref, op_ref, slab_scr, h_scr, wts, nbp, final_norm, prompt_layout)

    @pl.when(pl.program_id(0) == n_prompt)
    def _():
        _ffn_rows(xs_ref, mods_ref, os_ref, slab_scr, h_scr, wts, nbs, final_norm, "tt")


def _ffn_call(xp, xs, nbp, nbs, mod, layer, mod_group, g, w1, w3, w2, gf, final_norm,
              prompt_layout, cast_jobs, name):
    D = xp.shape[-1]
    rows_p = xp.size // D
    rows_s = xs.shape[0]
    tm = min(FFN_ROWS, rows_p)
    assert rows_p % tm == 0 and tm % nbp == 0 and rows_s % nbs == 0 and nbs % nbp == 0
    assert D % LANES == 0 and FFN_SUB_ROWS % (tm // nbp) == 0
    n_prompt = rows_p // tm
    tile_tm = pl.BlockSpec((tm, D), lambda i: (jnp.minimum(i, n_prompt - 1), 0))
    tile_sm = pl.BlockSpec((nbp, tm // nbp, D), lambda i: (0, jnp.minimum(i, n_prompt - 1), 0))
    shape_tm = jax.ShapeDtypeStruct((rows_p, D), F32)
    shape_sm = jax.ShapeDtypeStruct((nbp, rows_p // nbp, D), F32)
    n_cast_steps = _cast_steps(cast_jobs, n_prompt)
    cast_in, cast_out, cast_shapes = _cast_specs(
        cast_jobs, n_cast_steps,
        lambda i: jnp.minimum(i, n_prompt - 1) // (n_prompt // n_cast_steps))
    return pl.pallas_call(
        functools.partial(_ffn_kernel, layer=layer, n_cast=len(cast_jobs), nbp=nbp, nbs=nbs,
                          n_prompt=n_prompt, final_norm=final_norm, prompt_layout=prompt_layout),
        grid=(n_prompt + 1,),
        in_specs=[
            tile_sm if prompt_layout == "st" else tile_tm,
            _const_spec((rows_s, D), (0, 0)),
            _const_spec((None, 3, nbp, D), (layer, mod_group, nbs // nbp, 0)),
            _const_spec((None, 3, nbs, D), (layer, mod_group, 0, 0)),
            _const_spec(g.shape, (0, 0)), _const_spec(w1.shape, (0, 0)),
            _const_spec(w3.shape, (0, 0)),
            _const_spec(w2.shape, (0, 0)),
            _const_spec((1, D), (0, 0)),
        ] + cast_in,
        out_specs=[tile_sm if prompt_layout == "ts" else tile_tm,
                   pl.BlockSpec((rows_s, D), lambda i: (0, 0))] + cast_out,
        out_shape=[shape_sm if prompt_layout == "ts" else shape_tm,
                   jax.ShapeDtypeStruct((rows_s, D), F32)] + cast_shapes,
        scratch_shapes=[pltpu.VMEM((D // LANES, tm, LANES), F32), pltpu.VMEM((tm, D), BF16)],
        compiler_params=_params(("arbitrary",)),
        name=name,
    )(xp, xs, mod, mod, g, w1, w3, w2, gf, *[w for w, _ in cast_jobs])


def _softplus(v):
    return jnp.maximum(v, 0.0) + jnp.log1p(jnp.exp(-jnp.abs(v)))


def _mixer_kernel(*refs, layer, n_init, n_prev, n_cast, nb):
    (x_ref, mod_ref), refs = refs[:2], refs[2:]
    init_refs, refs = refs[:n_init], refs[n_init + n_prev:]
    (gn_ref, win_ref, convw_ref, convb_ref, wg_ref, brg_ref, big_ref, lam_ref,
     abr_ref, abi_ref, bmat_ref, cmat_ref, d_ref, wglu_ref, bglu_ref, wout_ref), refs = (
        refs[:16], refs[16:])
    cast_in, refs = refs[:n_cast], refs[n_cast:]
    (xo_ref, *state_refs), refs = refs[:5], refs[5:]
    convo_ref, ho_ref, sreo_ref, simo_ref = state_refs
    cast_out, (xp_scr, a_scr, b_scr, s_scr) = refs[:n_cast], refs[n_cast:]
    vec = lambda ref: ref[layer:layer + 1, :]
    rows, D = x_ref.shape
    tt = rows // nb
    DL = ho_ref.shape[-1]
    HS = d_ref.shape[-1] // 2
    HN = sreo_ref.shape[-1] // 2

    @pl.when(pl.program_id(0) == 0)
    def _():
        for k, st_ref in enumerate(state_refs):
            st_ref[...] = init_refs[k][...] if init_refs else jnp.zeros(st_ref.shape, F32)

    _cast_blocks(cast_in, cast_out)
    x = x_ref[...]
    h = _norm_modulate(x, nb, vec(gn_ref), mod_ref[0][None], mod_ref[1][None])
    z = jnp.dot(h.astype(BF16), win_ref[...], preferred_element_type=F32)
    xb = z[:, :DL]
    yb = z[:, DL:2 * DL]
    u = z[:, 2 * DL:]

    halo = (CONV_W - 1) * nb
    xp_scr[0:halo, :] = convo_ref[...]
    xp_scr[halo:halo + rows, :] = xb
    cw = convw_ref[...]
    xc = vec(convb_ref) + sum(xp_scr[k * nb:k * nb + rows, :] * cw[k:k + 1, :]
                              for k in range(CONV_W))
    convo_ref[...] = xp_scr[rows:rows + halo, :]

    gpre = jnp.dot(xc.astype(BF16), wg_ref[...], preferred_element_type=F32)
    r = jax.nn.sigmoid(gpre[:, :DL] + vec(brg_ref))
    ig = jax.nn.sigmoid(gpre[:, DL:] + vec(big_ref))
    log_a = -C_GATE * r * _softplus(-vec(lam_ref))
    a = jnp.exp(log_a)
    mult = _sqrt_nonneg(-jnp.tanh(log_a) * (a * a + 1.0))
    a_scr[...] = a
    b_scr[...] = mult * ig * xc

    for r0 in range(0, nb, SCAN_ROWS):
        hcur = ho_ref[r0:r0 + SCAN_ROWS, :]
        for t in range(tt):
            sl = slice(t * nb + r0, t * nb + r0 + SCAN_ROWS)
            hcur = a_scr[sl, :] * hcur + b_scr[sl, :]
            b_scr[sl, :] = hcur
        ho_ref[r0:r0 + SCAN_ROWS, :] = hcur
    y_lru = jax.nn.gelu(yb) * b_scr[...]

    ub = u.astype(BF16)
    for hf in range(2):
        for part in range(2):
            s_scr[hf, :, part * HN:(part + 1) * HN] = jnp.dot(
                ub[:, hf * HS:(hf + 1) * HS], bmat_ref[part, hf], preferred_element_type=F32)

    for hf in range(2):
        for c0 in range(0, HN, SCAN_LANES):
            st_l = slice(hf * HN + c0, hf * HN + c0 + SCAN_LANES)
            re_l = slice(c0, c0 + SCAN_LANES)
            im_l = slice(HN + c0, HN + c0 + SCAN_LANES)
            a_r = jnp.broadcast_to(abr_ref[:, st_l], (SCAN_ROWS, SCAN_LANES))
            a_i = jnp.broadcast_to(abi_ref[:, st_l], (SCAN_ROWS, SCAN_LANES))
            for r0 in range(0, nb, SCAN_ROWS):
                s_re = sreo_ref[r0:r0 + SCAN_ROWS, st_l]
                s_im = simo_ref[r0:r0 + SCAN_ROWS, st_l]
                for t in range(tt):
                    sl = slice(t * nb + r0, t * nb + r0 + SCAN_ROWS)
                    n_re = a_r * s_re - a_i * s_im + s_scr[hf, sl, re_l]
                    n_im = a_r * s_im + a_i * s_re + s_scr[hf, sl, im_l]
                    s_re, s_im = n_re, n_im
                    s_scr[hf, sl, re_l] = s_re
                    s_scr[hf, sl, im_l] = s_im
                sreo_ref[r0:r0 + SCAN_ROWS, st_l] = s_re
                simo_ref[r0:r0 + SCAN_ROWS, st_l] = s_im

    ys = []
    for hf in range(2):
        sb = s_scr[hf].astype(BF16)
        ys.append(jnp.dot(sb[:, :HN], cmat_ref[hf, 0], preferred_element_type=F32)
                  - jnp.dot(sb[:, HN:], cmat_ref[hf, 1], preferred_element_type=F32))
    ys = jnp.concatenate(ys, axis=-1) + vec(d_ref) * u
    g = jax.nn.gelu(ys)
    y_s5 = g * jax.nn.sigmoid(
        jnp.dot(g.astype(BF16), wglu_ref[...], preferred_element_type=F32) + vec(bglu_ref))

    ycat = jnp.concatenate([y_lru, y_s5], axis=-1).astype(BF16)
    out = jnp.dot(ycat, wout_ref[...], preferred_element_type=F32)
    xo_ref[...] = (_rows3(x, nb) + mod_ref[2][None] * _rows3(out, nb)).reshape(rows, D)


def _mixer_call(x, mod, layer, mod_rowblk, nb, init_states, prev_states, state_shapes, wts,
                cast_jobs, name):
    rows, D = x.shape
    tm = min(MIXER_ROWS, rows)
    assert rows % tm == 0 and tm % nb == 0 and tm // nb >= CONV_W - 1
    DL, GN = state_shapes[1][-1], state_shapes[2][-1]
    halo = (CONV_W - 1) * nb

    def wt_spec(w, kind):
        return _layer_spec(w, layer) if kind == "stacked" else _const_spec(w.shape, (0,) * w.ndim)

    n_lead = 2 + len(init_states)
    state_out_specs = [pl.BlockSpec((None,) + s[1:], lambda i: (layer, 0, 0))
                       for s in state_shapes]
    grid = rows // tm
    n_cast_steps = _cast_steps(cast_jobs, grid)
    cast_in, cast_out, cast_shapes = _cast_specs(cast_jobs, n_cast_steps,
                                                 lambda i: i // (grid // n_cast_steps))
    outs = pl.pallas_call(
        functools.partial(_mixer_kernel, layer=layer, n_init=len(init_states),
                          n_prev=len(prev_states), n_cast=len(cast_jobs), nb=nb),
        grid=(grid,),
        in_specs=[pl.BlockSpec((tm, D), lambda i: (i, 0)),
                  _const_spec((None, 3, nb, D), (layer, 1, mod_rowblk, 0))]
                 + [_layer_spec(s, layer) for s in init_states]
                 + [pl.BlockSpec(memory_space=pl.ANY)] * len(prev_states)
                 + [wt_spec(w, kind) for w, kind in wts] + cast_in,
        out_specs=[pl.BlockSpec((tm, D), lambda i: (i, 0))] + state_out_specs + cast_out,
        out_shape=[jax.ShapeDtypeStruct((rows, D), F32)]
                  + [jax.ShapeDtypeStruct(s, F32) for s in state_shapes] + cast_shapes,
        input_output_aliases={n_lead + k: 1 + k for k in range(len(prev_states))},
        scratch_shapes=[pltpu.VMEM((tm + halo, DL), F32),
                        pltpu.VMEM((tm, DL), F32),
                        pltpu.VMEM((tm, DL), F32),
                        pltpu.VMEM((2, tm, GN), F32)],
        compiler_params=_params(("arbitrary",)),
        name=name,
    )(x, mod, *init_states, *prev_states, *[w for w, _ in wts], *[w for w, _ in cast_jobs])
    n_st = len(state_shapes)
    return outs[0], tuple(outs[1:1 + n_st]), tuple(outs[1 + n_st:])


def kernel(x_prompt, x_sample, c_prompt, c_sample, state_lru_conv, state_lru_h, state_s5_re, state_s5_im, w_ada, b_ada, norm_ffn1, w1_ffn1, w3_ffn1, w2_ffn1, norm_mix, w_in, conv_w, conv_b, w_rg, b_rg, w_ig, b_ig, lru_lambda, s5_a_re, s5_a_im, s5_log_dt, s5_b_re, s5_b_im, s5_c_re, s5_c_im, s5_d, w_glu, b_glu, w_out, norm_ffn2, w1_ffn2, w3_ffn2, w2_ffn2, norm_final):
    L, D, _ = w_ada.shape
    Bp, Tp, _ = x_prompt.shape
    Bs, Ts, _ = x_sample.shape
    _, G, N = s5_a_re.shape
    DL = lru_lambda.shape[-1]
    GN = G * N
    halo = CONV_W - 1
    assert Bs % Bp == 0 and Bp % SCAN_ROWS == 0 and G % 2 == 0

    mod = _ada_call(jnp.concatenate([c_sample, c_prompt], axis=0), w_ada, b_ada)
    abr, abi, bmat, cmat, wgate = _prep_call(s5_a_re, s5_a_im, s5_log_dt, s5_b_re, s5_b_im,
                                             s5_c_re, s5_c_im, w_rg, w_ig)

    gfin = norm_final.reshape(1, D)
    ffn1_f32 = (w1_ffn1, w3_ffn1, w2_ffn1)
    ffn2_f32 = (w1_ffn2, w3_ffn2, w2_ffn2)
    mix_f32 = (w_in, w_glu, w_out)

    def mixer_wts(w_in_b, w_glu_b, w_out_b):
        return [(norm_mix, "rows"), (w_in_b, "layer"), (conv_w, "stacked"), (conv_b, "rows"),
                (wgate, "stacked"), (b_rg, "rows"), (b_ig, "rows"), (lru_lambda, "rows"),
                (abr, "stacked"), (abi, "stacked"), (bmat, "stacked"), (cmat, "stacked"),
                (s5_d, "rows"), (w_glu_b, "layer"), (b_glu, "rows"), (w_out_b, "layer")]

    def time_major(x):
        return jnp.swapaxes(x, 0, 1).reshape(x.shape[0] * x.shape[1], D)

    def state_shapes(nb):
        return ((L, halo * nb, DL), (L, nb, DL), (L, nb, GN), (L, nb, GN))

    init_s = (jnp.swapaxes(state_lru_conv, 1, 2).reshape(L, halo * Bs, DL), state_lru_h,
              state_s5_re.reshape(L, Bs, GN), state_s5_im.reshape(L, Bs, GN))
    st_p, st_s = (), ()
    xp, xs = x_prompt, time_major(x_sample)
    ffn1_b = tuple(w[0].astype(BF16) for w in ffn1_f32)
    for l in range(L):
        xp, xs, *mix_b = _ffn_call(xp, xs, Bp, Bs, mod, l, 0, norm_ffn1, *ffn1_b, gfin,
                                   False, "st" if l == 0 else "tt", [(w, l) for w in mix_f32],
                                   f"ffn1_{l}")
        xp, st_p, ffn2_b = _mixer_call(xp, mod, l, Bs // Bp, Bp, (), st_p, state_shapes(Bp),
                                       mixer_wts(*mix_b), [(w, l) for w in ffn2_f32],
                                       f"mixer_p{l}")
        xs, st_s, _ = _mixer_call(xs, mod, l, 0, Bs, init_s, st_s, state_shapes(Bs),
                                  mixer_wts(*mix_b), [], f"mixer_s{l}")
        jobs = [(w, l + 1) for w in ffn1_f32] if l + 1 < L else []
        xp, xs, *ffn1_b = _ffn_call(xp, xs, Bp, Bs, mod, l, 2, norm_ffn2, *ffn2_b, gfin,
                                    l == L - 1, "ts" if l == L - 1 else "tt", jobs, f"ffn2_{l}")

    def finish(y, st, nb):
        conv, hh, sre, sim = st
        return (y, jnp.swapaxes(conv.reshape(L, halo, nb, DL), 1, 2), hh,
                sre.reshape(L, nb, G, N), sim.reshape(L, nb, G, N))

    p_out = finish(xp, st_p, Bp)
    s_out = finish(jnp.swapaxes(xs.reshape(Ts, Bs, D), 0, 1), st_s, Bs)
    return (p_out[0], s_out[0]) + p_out[1:] + s_out[1:]
```

```python
import functools

import jax
import jax.numpy as jnp
from jax import lax
from jax.experimental import pallas as pl
from jax.experimental.pallas import tpu as pltpu

EPS = 1e-6
C_GATE = 8.0
FFN_RES = 0.5
CONV_W = 4
N_MOD = 9
S5_A_RE_MAX = -1e-4
BF16 = jnp.bfloat16
F32 = jnp.float32

VMEM_LIMIT_BYTES = 56 * 1024 * 1024
FFN_ROWS = 1024
FFN_SUB_ROWS = 512
MIXER_ROWS = 512
SCAN_ROWS = 8
SCAN_LANES = 512
LANES = 128
BF16_ROWS = 16


def _const_spec(shape, index):
    return pl.BlockSpec(shape, lambda i: index, pipeline_mode=pl.Buffered(1))


def _layer_spec(w, layer):
    return _const_spec((None,) + w.shape[1:], (layer,) + (0,) * (w.ndim - 1))


def _params(semantics):
    return pltpu.CompilerParams(dimension_semantics=semantics,
                                vmem_limit_bytes=VMEM_LIMIT_BYTES)


def _cast_steps(jobs, grid):
    n = grid
    while n > 1 and (grid % n or any(w.shape[1] % (BF16_ROWS * n) for w, _ in jobs)):
        n -= 1
    return n


def _cast_specs(jobs, n_steps, step_of):
    ins, outs, shapes = [], [], []
    for w, layer in jobs:
        _, R, C = w.shape
        assert R % (BF16_ROWS * n_steps) == 0
        ins.append(pl.BlockSpec((None, R // n_steps, C),
                                lambda i, layer=layer: (layer, step_of(i), 0)))
        outs.append(pl.BlockSpec((R // n_steps, C), lambda i: (step_of(i), 0)))
        shapes.append(jax.ShapeDtypeStruct((R, C), BF16))
    return ins, outs, shapes


def _cast_blocks(in_refs, out_refs):
    for src_ref, dst_ref in zip(in_refs, out_refs):
        dst_ref[...] = src_ref[...].astype(BF16)


def _rmsnorm(x, g):
    return x * lax.rsqrt(jnp.mean(x * x, axis=-1, keepdims=True) + EPS) * g


def _norm_modulate(x, inner, g, shift, scale):
    xn = x * lax.rsqrt(jnp.mean(x * x, axis=-1, keepdims=True) + EPS)
    xn = xn.reshape(x.shape[0] // inner, inner, x.shape[1])
    return (xn * (g * (1.0 + scale)) + shift).reshape(x.shape)


def _sqrt_nonneg(v):
    return jnp.where(v > 0.0, v * lax.rsqrt(v), v)


def _rows3(v, nb):
    return v.reshape(v.shape[0] // nb, nb, v.shape[1])


def _ada_kernel(c_ref, w_ref, b_ref, o_ref):
    c = c_ref[...]
    s = (c * jax.nn.sigmoid(c)).astype(BF16)
    D = c.shape[-1]
    for k in range(o_ref.shape[0]):
        w = w_ref[:, k * D:(k + 1) * D].astype(BF16)
        o_ref[k] = jnp.dot(s, w, preferred_element_type=F32) + b_ref[k]


def _ada_call(c_all, w_ada, b_ada):
    L, D, _ = w_ada.shape
    nseq = c_all.shape[0]
    n_sub = N_MOD // 3
    return pl.pallas_call(
        _ada_kernel,
        grid=(L, n_sub),
        in_specs=[
            pl.BlockSpec((nseq, D), lambda l, k: (0, 0)),
            pl.BlockSpec((None, D, 3 * D), lambda l, k: (l, 0, k)),
            pl.BlockSpec((None, 3, 1, D), lambda l, k: (l, k, 0, 0)),
        ],
        out_specs=pl.BlockSpec((None, 3, nseq, D), lambda l, k: (l, k, 0, 0)),
        out_shape=jax.ShapeDtypeStruct((L, N_MOD, nseq, D), F32),
        compiler_params=_params(("arbitrary", "arbitrary")),
        name="adaln_mod",
    )(c_all, w_ada, b_ada.reshape(L, N_MOD, 1, D))


def _block_diag_rows(t, nblk):
    r, width = t.shape
    c = width // nblk
    tiled = jnp.concatenate([t] * nblk, axis=0)
    row_blk = lax.broadcasted_iota(jnp.int32, tiled.shape, 0) // r
    col_blk = lax.broadcasted_iota(jnp.int32, tiled.shape, 1) // c
    return jnp.where(row_blk == col_blk, tiled, 0.0)


def _prep_kernel(are_ref, aim_ref, ldt_ref, bt_ref, ct_ref, wt_ref,
                 abr_ref, abi_ref, bmat_ref, cmat_ref, wg_ref):
    ar = jnp.minimum(are_ref[...], S5_A_RE_MAX)
    ai = aim_ref[...]
    dt = jnp.exp(ldt_ref[...])
    mag = jnp.exp(ar * dt)
    abr = mag * jnp.cos(ai * dt)
    abi = mag * jnp.sin(ai * dt)
    abr_ref[...] = abr
    abi_ref[...] = abi
    den = ar * ar + ai * ai
    f_r = ((abr - 1.0) * ar + abi * ai) / den
    f_i = (abi * ar - (abr - 1.0) * ai) / den
    b_r = bt_ref[0]
    b_i = bt_ref[1]
    bb = (f_r * b_r - f_i * b_i, f_r * b_i + f_i * b_r)
    hn = bb[0].shape[-1] // 2
    nblk = bmat_ref.shape[-2] // bb[0].shape[0]
    for part in range(2):
        for hf in range(2):
            cols = slice(hf * hn, (hf + 1) * hn)
            bmat_ref[part, hf] = _block_diag_rows(bb[part][:, cols], nblk).astype(BF16)
            cmat_ref[hf, part] = _block_diag_rows(ct_ref[part][:, cols], nblk).T.astype(BF16)
    heads = wg_ref.shape[0] // wt_ref.shape[1]
    dl = wg_ref.shape[0]
    for gate in range(2):
        wg_ref[:, gate * dl:(gate + 1) * dl] = _block_diag_rows(wt_ref[gate], heads).astype(BF16)


def _prep_call(a_re, a_im, log_dt, b_re, b_im, c_re, c_im, w_rg, w_ig):
    L, G, N = a_re.shape
    J = b_re.shape[-1]
    H, HD, _ = w_rg.shape[1:]
    GN, HN, DL = G * N, G * N // 2, H * HD
    row = lambda v: v.reshape(L, 1, GN)
    ldt = jnp.broadcast_to(log_dt[:, :, None], (L, G, N))
    bt = jnp.stack([b_re, b_im], axis=1).transpose(0, 1, 4, 2, 3).reshape(L, 2, J, GN)
    ct = jnp.stack([c_re, c_im], axis=1).transpose(0, 1, 3, 2, 4).reshape(L, 2, J, GN)
    wt = jnp.stack([w_rg, w_ig], axis=1).transpose(0, 1, 3, 2, 4).reshape(L, 2, HD, DL)
    rspec = pl.BlockSpec((None, 1, GN), lambda l: (l, 0, 0))
    jspec = pl.BlockSpec((None, 2, J, GN), lambda l: (l, 0, 0, 0))
    return pl.pallas_call(
        _prep_kernel,
        grid=(L,),
        in_specs=[rspec, rspec, rspec, jspec, jspec,
                  pl.BlockSpec((None, 2, HD, DL), lambda l: (l, 0, 0, 0))],
        out_specs=[rspec, rspec,
                   pl.BlockSpec((None, 2, 2, G // 2 * J, HN), lambda l: (l, 0, 0, 0, 0)),
                   pl.BlockSpec((None, 2, 2, HN, G // 2 * J), lambda l: (l, 0, 0, 0, 0)),
                   pl.BlockSpec((None, DL, 2 * DL), lambda l: (l, 0, 0))],
        out_shape=[jax.ShapeDtypeStruct((L, 1, GN), F32), jax.ShapeDtypeStruct((L, 1, GN), F32),
                   jax.ShapeDtypeStruct((L, 2, 2, G // 2 * J, HN), BF16),
                   jax.ShapeDtypeStruct((L, 2, 2, HN, G // 2 * J), BF16),
                   jax.ShapeDtypeStruct((L, DL, 2 * DL), BF16)],
        compiler_params=_params(("arbitrary",)),
        name="param_prep",
    )(row(a_re), row(a_im), row(ldt), bt, ct, wt)


def _ffn_matmuls(x, h, inner, gate, w1_ref, w3_ref, w2_ref, gf_ref, final_norm):
    rows3 = lambda v: v.reshape(v.shape[0] // inner, inner, v.shape[1])
    a = jnp.dot(h, w1_ref[...], preferred_element_type=F32)
    b = jnp.dot(h, w3_ref[...], preferred_element_type=F32)
    act = (a * jax.nn.sigmoid(a) * b).astype(BF16)
    f = jnp.dot(act, w2_ref[...], preferred_element_type=F32)
    out = (rows3(x) + (FFN_RES * gate) * rows3(f)).reshape(x.shape)
    if final_norm:
        out = _rmsnorm(out, gf_ref[...])
    return out


def _ffn_rows(x_ref, mod_ref, o_ref, slab_scr, h_scr, wts, nb, final_norm, layout):
    D = x_ref.shape[-1]
    rows = o_ref.shape[0] * o_ref.shape[1] if layout == "ts" else o_ref.shape[0]
    steps = rows // nb
    sub = min(FFN_SUB_ROWS, rows)
    n_slab = D // LANES
    g_ref, w1_ref, w3_ref, w2_ref, gf_ref = wts

    def tile(r0):
        if layout == "st":
            b0, nseq = r0 // steps, sub // steps
            x = x_ref[b0:b0 + nseq].reshape(sub, D)
            return x, steps, [mod_ref[k, b0:b0 + nseq][:, None, :] for k in range(3)]
        return x_ref[r0:r0 + sub, :], nb, [mod_ref[k][None] for k in range(3)]

    for r0 in range(0, rows, sub):
        x, inner, (shift, scale, _) = tile(r0)
        h_scr[r0:r0 + sub, :] = _norm_modulate(x, inner, g_ref[...], shift, scale).astype(BF16)
    for r0 in range(0, rows, sub):
        x, inner, (_, _, gate) = tile(r0)
        out = _ffn_matmuls(x, h_scr[r0:r0 + sub, :], inner, gate, w1_ref, w3_ref, w2_ref,
                           gf_ref, final_norm)
        if layout == "st":
            b0, nseq = r0 // steps, sub // steps
        if layout == "tt":
            o_ref[r0:r0 + sub, :] = out
        elif layout == "st":
            for b in range(b0, b0 + nseq):
                seq = out[(b - b0) * steps:(b - b0 + 1) * steps]
                for s in range(n_slab):
                    slab_scr[s, pl.ds(b, steps, stride=nb), :] = seq[:, s * LANES:(s + 1) * LANES]
        else:
            for s in range(n_slab):
                slab_scr[s, r0:r0 + sub, :] = out[:, s * LANES:(s + 1) * LANES]
            t0, nt = r0 // nb, sub // nb
            for b in range(nb):
                for s in range(n_slab):
                    o_ref[b, t0:t0 + nt, s * LANES:(s + 1) * LANES] = (
                        slab_scr[s, pl.ds(r0 + b, nt, stride=nb), :])
    if layout == "st":
        for s in range(n_slab):
            o_ref[:, s * LANES:(s + 1) * LANES] = slab_scr[s]


def _ffn_kernel(*refs, layer, n_cast, nbp, nbs, n_prompt, final_norm, prompt_layout):
    (xp_ref, xs_ref, modp_ref, mods_ref, g_ref, w1_ref, w3_ref, w2_ref, gf_ref), refs = (
        refs[:9], refs[9:])
    cast_in, (op_ref, os_ref), cast_out, (slab_scr, h_scr) = (
        refs[:n_cast], refs[n_cast:n_cast + 2], refs[n_cast + 2:2 * n_cast + 2], refs[-2:])
    wts = (g_ref.at[layer:layer + 1], w1_ref, w3_ref, w2_ref, gf_ref)

    @pl.when(pl.program_id(0) < n_prompt)
    def _():
        _cast_blocks(cast_in, cast_out)
        _ffn_rows(xp_ref, modp_ref, op_ref, slab_scr, h_scr, wts, nbp, final_norm, prompt_layout)

    @pl.when(pl.program_id(0) == n_prompt)
    def _():
        _ffn_rows(xs_ref, mods_ref, os_ref, slab_scr, h_scr, wts, nbs, final_norm, "tt")


def _ffn_call(xp, xs, nbp, nbs, mod, layer, mod_group, g, w1, w3, w2, gf, final_norm,
              prompt_layout, cast_jobs, name):
    D = xp.shape[-1]
    rows_p = xp.size // D
    rows_s = xs.shape[0]
    tm = min(FFN_ROWS, rows_p)
    assert rows_p % tm == 0 and tm % nbp == 0 and rows_s % nbs == 0 and nbs % nbp == 0
    assert D % LANES == 0 and FFN_SUB_ROWS % (tm // nbp) == 0
    n_prompt = rows_p // tm
    tile_tm = pl.BlockSpec((tm, D), lambda i: (jnp.minimum(i, n_prompt - 1), 0))
    tile_sm = pl.BlockSpec((nbp, tm // nbp, D), lambda i: (0, jnp.minimum(i, n_prompt - 1), 0))
    shape_tm = jax.ShapeDtypeStruct((rows_p, D), F32)
    shape_sm = jax.ShapeDtypeStruct((nbp, rows_p // nbp, D), F32)
    n_cast_steps = _cast_steps(cast_jobs, n_prompt)
    cast_in, cast_out, cast_shapes = _cast_specs(
        cast_jobs, n_cast_steps,
        lambda i: jnp.minimum(i, n_prompt - 1) // (n_prompt // n_cast_steps))
    return pl.pallas_call(
        functools.partial(_ffn_kernel, layer=layer, n_cast=len(cast_jobs), nbp=nbp, nbs=nbs,
                          n_prompt=n_prompt, final_norm=final_norm, prompt_layout=prompt_layout),
        grid=(n_prompt + 1,),
        in_specs=[
            tile_sm if prompt_layout == "st" else tile_tm,
            _const_spec((rows_s, D), (0, 0)),
            _const_spec((None, 3, nbp, D), (layer, mod_group, nbs // nbp, 0)),
            _const_spec((None, 3, nbs, D), (layer, mod_group, 0, 0)),
            _const_spec(g.shape, (0, 0)), _const_spec(w1.shape, (0, 0)),
            _const_spec(w3.shape, (0, 0)),
            _const_spec(w2.shape, (0, 0)),
            _const_spec((1, D), (0, 0)),
        ] + cast_in,
        out_specs=[tile_sm if prompt_layout == "ts" else tile_tm,
                   pl.BlockSpec((rows_s, D), lambda i: (0, 0))] + cast_out,
        out_shape=[shape_sm if prompt_layout == "ts" else shape_tm,
                   jax.ShapeDtypeStruct((rows_s, D), F32)] + cast_shapes,
        scratch_shapes=[pltpu.VMEM((D // LANES, tm, LANES), F32), pltpu.VMEM((tm, D), BF16)],
        compiler_params=_params(("arbitrary",)),
        name=name,
    )(xp, xs, mod, mod, g, w1, w3, w2, gf, *[w for w, _ in cast_jobs])


def _softplus(v):
    return jnp.maximum(v, 0.0) + jnp.log1p(jnp.exp(-jnp.abs(v)))


def _mixer_kernel(*refs, layer, n_init, n_prev, n_cast, nb):
    (x_ref, mod_ref), refs = refs[:2], refs[2:]
    init_refs, refs = refs[:n_init], refs[n_init + n_prev:]
    (gn_ref, win_ref, convw_ref, convb_ref, wg_ref, brg_ref, big_ref, lam_ref,
     abr_ref, abi_ref, bmat_ref, cmat_ref, d_ref, wglu_ref, bglu_ref, wout_ref), refs = (
        refs[:16], refs[16:])
    cast_in, refs = refs[:n_cast], refs[n_cast:]
    (xo_ref, *state_refs), refs = refs[:5], refs[5:]
    convo_ref, ho_ref, sreo_ref, simo_ref = state_refs
    cast_out, (xp_scr, a_scr, b_scr, s_scr) = refs[:n_cast], refs[n_cast:]
    vec = lambda ref: ref[layer:layer + 1, :]
    rows, D = x_ref.shape
    tt = rows // nb
    DL = ho_ref.shape[-1]
    HS = d_ref.shape[-1] // 2
    HN = sreo_ref.shape[-1] // 2

    @pl.when(pl.program_id(0) == 0)
    def _():
        for k, st_ref in enumerate(state_refs):
            st_ref[...] = init_refs[k][...] if init_refs else jnp.zeros(st_ref.shape, F32)

    _cast_blocks(cast_in, cast_out)
    x = x_ref[...]
    h = _norm_modulate(x, nb, vec(gn_ref), mod_ref[0][None], mod_ref[1][None])
    z = jnp.dot(h.astype(BF16), win_ref[...], preferred_element_type=F32)
    xb = z[:, :DL]
    yb = z[:, DL:2 * DL]
    u = z[:, 2 * DL:]

    halo = (CONV_W - 1) * nb
    xp_scr[0:halo, :] = convo_ref[...]
    xp_scr[halo:halo + rows, :] = xb
    cw = convw_ref[...]
    xc = vec(convb_ref) + sum(xp_scr[k * nb:k * nb + rows, :] * cw[k:k + 1, :]
                              for k in range(CONV_W))
    convo_ref[...] = xp_scr[rows:rows + halo, :]

    gpre = jnp.dot(xc.astype(BF16), wg_ref[...], preferred_element_type=F32)
    r = jax.nn.sigmoid(gpre[:, :DL] + vec(brg_ref))
    ig = jax.nn.sigmoid(gpre[:, DL:] + vec(big_ref))
    log_a = -C_GATE * r * _softplus(-vec(lam_ref))
    a = jnp.exp(log_a)
    mult = _sqrt_nonneg(-jnp.tanh(log_a) * (a * a + 1.0))
    a_scr[...] = a
    b_scr[...] = mult * ig * xc

    for r0 in range(0, nb, SCAN_ROWS):
        hcur = ho_ref[r0:r0 + SCAN_ROWS, :]
        for t in range(tt):
            sl = slice(t * nb + r0, t * nb + r0 + SCAN_ROWS)
            hcur = a_scr[sl, :] * hcur + b_scr[sl, :]
            b_scr[sl, :] = hcur
        ho_ref[r0:r0 + SCAN_ROWS, :] = hcur
    y_lru = jax.nn.gelu(yb) * b_scr[...]

    ub = u.astype(BF16)
    for hf in range(2):
        for part in range(2):
            s_scr[hf, :, part * HN:(part + 1) * HN] = jnp.dot(
                ub[:, hf * HS:(hf + 1) * HS], bmat_ref[part, hf], preferred_element_type=F32)

    for hf in range(2):
        for c0 in range(0, HN, SCAN_LANES):
            st_l = slice(hf * HN + c0, hf * HN + c0 + SCAN_LANES)
            re_l = slice(c0, c0 + SCAN_LANES)
            im_l = slice(HN + c0, HN + c0 + SCAN_LANES)
            a_r = jnp.broadcast_to(abr_ref[:, st_l], (SCAN_ROWS, SCAN_LANES))
            a_i = jnp.broadcast_to(abi_ref[:, st_l], (SCAN_ROWS, SCAN_LANES))
            for r0 in range(0, nb, SCAN_ROWS):
                s_re = sreo_ref[r0:r0 + SCAN_ROWS, st_l]
                s_im = simo_ref[r0:r0 + SCAN_ROWS, st_l]
                for t in range(tt):
                    sl = slice(t * nb + r0, t * nb + r0 + SCAN_ROWS)
                    n_re = a_r * s_re - a_i * s_im + s_scr[hf, sl, re_l]
                    n_im = a_r * s_im + a_i * s_re + s_scr[hf, sl, im_l]
                    s_re, s_im = n_re, n_im
                    s_scr[hf, sl, re_l] = s_re
                    s_scr[hf, sl, im_l] = s_im
                sreo_ref[r0:r0 + SCAN_ROWS, st_l] = s_re
                simo_ref[r0:r0 + SCAN_ROWS, st_l] = s_im

    ys = []
    for hf in range(2):
        sb = s_scr[hf].astype(BF16)
        ys.append(jnp.dot(sb[:, :HN], cmat_ref[hf, 0], preferred_element_type=F32)
                  - jnp.dot(sb[:, HN:], cmat_ref[hf, 1], preferred_element_type=F32))
    ys = jnp.concatenate(ys, axis=-1) + vec(d_ref) * u
    g = jax.nn.gelu(ys)
    y_s5 = g * jax.nn.sigmoid(
        jnp.dot(g.astype(BF16), wglu_ref[...], preferred_element_type=F32) + vec(bglu_ref))

    ycat = jnp.concatenate([y_lru, y_s5], axis=-1).astype(BF16)
    out = jnp.dot(ycat, wout_ref[...], preferred_element_type=F32)
    xo_ref[...] = (_rows3(x, nb) + mod_ref[2][None] * _rows3(out, nb)).reshape(rows, D)


def _mixer_call(x, mod, layer, mod_rowblk, nb, init_states, prev_states, state_shapes, wts,
                cast_jobs, name):
    rows, D = x.shape
    tm = min(MIXER_ROWS, rows)
    assert rows % tm == 0 and tm % nb == 0 and tm // nb >= CONV_W - 1
    DL, GN = state_shapes[1][-1], state_shapes[2][-1]
    halo = (CONV_W - 1) * nb

    def wt_spec(w, kind):
        return _layer_spec(w, layer) if kind == "stacked" else _const_spec(w.shape, (0,) * w.ndim)

    n_lead = 2 + len(init_states)
    state_out_specs = [pl.BlockSpec((None,) + s[1:], lambda i: (layer, 0, 0))
                       for s in state_shapes]
    grid = rows // tm
    n_cast_steps = _cast_steps(cast_jobs, grid)
    cast_in, cast_out, cast_shapes = _cast_specs(cast_jobs, n_cast_steps,
                                                 lambda i: i // (grid // n_cast_steps))
    outs = pl.pallas_call(
        functools.partial(_mixer_kernel, layer=layer, n_init=len(init_states),
                          n_prev=len(prev_states), n_cast=len(cast_jobs), nb=nb),
        grid=(grid,),
        in_specs=[pl.BlockSpec((tm, D), lambda i: (i, 0)),
                  _const_spec((None, 3, nb, D), (layer, 1, mod_rowblk, 0))]
                 + [_layer_spec(s, layer) for s in init_states]
                 + [pl.BlockSpec(memory_space=pl.ANY)] * len(prev_states)
                 + [wt_spec(w, kind) for w, kind in wts] + cast_in,
        out_specs=[pl.BlockSpec((tm, D), lambda i: (i, 0))] + state_out_specs + cast_out,
        out_shape=[jax.ShapeDtypeStruct((rows, D), F32)]
                  + [jax.ShapeDtypeStruct(s, F32) for s in state_shapes] + cast_shapes,
        input_output_aliases={n_lead + k: 1 + k for k in range(len(prev_states))},
        scratch_shapes=[pltpu.VMEM((tm + halo, DL), F32),
                        pltpu.VMEM((tm, DL), F32),
                        pltpu.VMEM((tm, DL), F32),
                        pltpu.VMEM((2, tm, GN), F32)],
        compiler_params=_params(("arbitrary",)),
        name=name,
    )(x, mod, *init_states, *prev_states, *[w for w, _ in wts], *[w for w, _ in cast_jobs])
    n_st = len(state_shapes)
    return outs[0], tuple(outs[1:1 + n_st]), tuple(outs[1 + n_st:])


def kernel(x_prompt, x_sample, c_prompt, c_sample, state_lru_conv, state_lru_h, state_s5_re, state_s5_im, w_ada, b_ada, norm_ffn1, w1_ffn1, w3_ffn1, w2_ffn1, norm_mix, w_in, conv_w, conv_b, w_rg, b_rg, w_ig, b_ig, lru_lambda, s5_a_re, s5_a_im, s5_log_dt, s5_b_re, s5_b_im, s5_c_re, s5_c_im, s5_d, w_glu, b_glu, w_out, norm_ffn2, w1_ffn2, w3_ffn2, w2_ffn2, norm_final):
    L, D, _ = w_ada.shape
    Bp, Tp, _ = x_prompt.shape
    Bs, Ts, _ = x_sample.shape
    _, G, N = s5_a_re.shape
    DL = lru_lambda.shape[-1]
    GN = G * N
    halo = CONV_W - 1
    assert Bs % Bp == 0 and Bp % SCAN_ROWS == 0 and G % 2 == 0

    mod = _ada_call(jnp.concatenate([c_sample, c_prompt], axis=0), w_ada, b_ada)
    abr, abi, bmat, cmat, wgate = _prep_call(s5_a_re, s5_a_im, s5_log_dt, s5_b_re, s5_b_im,
                                             s5_c_re, s5_c_im, w_rg, w_ig)

    gfin = norm_final.reshape(1, D)
    ffn1_f32 = (w1_ffn1, w3_ffn1, w2_ffn1)
    ffn2_f32 = (w1_ffn2, w3_ffn2, w2_ffn2)
    mix_f32 = (w_in, w_glu, w_out)

    def mixer_wts(w_in_b, w_glu_b, w_out_b):
        return [(norm_mix, "rows"), (w_in_b, "layer"), (conv_w, "stacked"), (conv_b, "rows"),
                (wgate, "stacked"), (b_rg, "rows"), (b_ig, "rows"), (lru_lambda, "rows"),
                (abr, "stacked"), (abi, "stacked"), (bmat, "stacked"), (cmat, "stacked"),
                (s5_d, "rows"), (w_glu_b, "layer"), (b_glu, "rows"), (w_out_b, "layer")]

    def time_major(x):
        return jnp.swapaxes(x, 0, 1).reshape(x.shape[0] * x.shape[1], D)

    def state_shapes(nb):
        return ((L, halo * nb, DL), (L, nb, DL), (L, nb, GN), (L, nb, GN))

    init_s = (jnp.swapaxes(state_lru_conv, 1, 2).reshape(L, halo * Bs, DL), state_lru_h,
              state_s5_re.reshape(L, Bs, GN), state_s5_im.reshape(L, Bs, GN))
    st_p = tuple(jnp.zeros(s, F32) for s in state_shapes(Bp))
    st_s = tuple(jnp.zeros(s, F32) for s in state_shapes(Bs))
    xp, xs = x_prompt, time_major(x_sample)
    ffn1_b = tuple(w[0].astype(BF16) for w in ffn1_f32)
    for l in range(L):
        xp, xs, *mix_b = _ffn_call(xp, xs, Bp, Bs, mod, l, 0, norm_ffn1, *ffn1_b, gfin,
                                   False, "st" if l == 0 else "tt", [(w, l) for w in mix_f32],
                                   f"ffn1_{l}")
        jobs = [(w, l) for w in ffn2_f32] + [(w, l + 1) for w in ffn1_f32 if l + 1 < L]
        xp, st_p, cast = _mixer_call(xp, mod, l, Bs // Bp, Bp, (), st_p, state_shapes(Bp),
                                     mixer_wts(*mix_b), jobs, f"mixer_p{l}")
        ffn2_b, ffn1_b = cast[:3], cast[3:]
        xs, st_s, _ = _mixer_call(xs, mod, l, 0, Bs, init_s, st_s, state_shapes(Bs),
                                  mixer_wts(*mix_b), [], f"mixer_s{l}")
        xp, xs = _ffn_call(xp, xs, Bp, Bs, mod, l, 2, norm_ffn2, *ffn2_b, gfin,
                           l == L - 1, "ts" if l == L - 1 else "tt", [], f"ffn2_{l}")

    def finish(y, st, nb):
        conv, hh, sre, sim = st
        return (y, jnp.swapaxes(conv.reshape(L, halo, nb, DL), 1, 2), hh,
                sre.reshape(L, nb, G, N), sim.reshape(L, nb, G, N))

    p_out = finish(xp, st_p, Bp)
    s_out = finish(jnp.swapaxes(xs.reshape(Ts, Bs, D), 0, 1), st_s, Bs)
    return (p_out[0], s_out[0]) + p_out[1:] + s_out[1:]
```

```python
import functools

import jax
import jax.numpy as jnp
from jax import lax
from jax.experimental import pallas as pl
from jax.experimental.pallas import tpu as pltpu

EPS = 1e-6
C_GATE = 8.0
FFN_RES = 0.5
CONV_W = 4
N_MOD = 9
S5_A_RE_MAX = -1e-4
BF16 = jnp.bfloat16
F32 = jnp.float32

VMEM_LIMIT_BYTES = 56 * 1024 * 1024
FFN_ROWS = 1024
FFN_SUB_ROWS = 512
MIXER_ROWS = 512
SCAN_ROWS = 8
SCAN_LANES = 512
LANES = 128
BF16_ROWS = 16


def _const_spec(shape, index):
    return pl.BlockSpec(shape, lambda i: index, pipeline_mode=pl.Buffered(1))


def _layer_spec(w, layer):
    return _const_spec((None,) + w.shape[1:], (layer,) + (0,) * (w.ndim - 1))


def _params(semantics):
    return pltpu.CompilerParams(dimension_semantics=semantics,
                                vmem_limit_bytes=VMEM_LIMIT_BYTES)


def _cast_steps(jobs, grid):
    n = grid
    while n > 1 and (grid % n or any(w.shape[1] % (BF16_ROWS * n) for w, _ in jobs)):
        n -= 1
    return n


def _cast_specs(jobs, n_steps, step_of):
    ins, outs, shapes = [], [], []
    for w, layer in jobs:
        _, R, C = w.shape
        assert R % (BF16_ROWS * n_steps) == 0
        ins.append(pl.BlockSpec((None, R // n_steps, C),
                                lambda i, layer=layer: (layer, step_of(i), 0)))
        outs.append(pl.BlockSpec((R // n_steps, C), lambda i: (step_of(i), 0)))
        shapes.append(jax.ShapeDtypeStruct((R, C), BF16))
    return ins, outs, shapes


def _cast_blocks(in_refs, out_refs):
    for src_ref, dst_ref in zip(in_refs, out_refs):
        dst_ref[...] = src_ref[...].astype(BF16)


def _rmsnorm(x, g):
    return x * lax.rsqrt(jnp.mean(x * x, axis=-1, keepdims=True) + EPS) * g


def _norm_modulate(x, inner, g, shift, scale):
    xn = x * lax.rsqrt(jnp.mean(x * x, axis=-1, keepdims=True) + EPS)
    xn = xn.reshape(x.shape[0] // inner, inner, x.shape[1])
    return (xn * (g * (1.0 + scale)) + shift).reshape(x.shape)


def _sqrt_nonneg(v):
    return jnp.where(v > 0.0, v * lax.rsqrt(v), v)


def _rows3(v, nb):
    return v.reshape(v.shape[0] // nb, nb, v.shape[1])


def _ada_kernel(c_ref, w_ref, b_ref, o_ref):
    c = c_ref[...]
    s = (c * jax.nn.sigmoid(c)).astype(BF16)
    D = c.shape[-1]
    for k in range(o_ref.shape[0]):
        w = w_ref[:, k * D:(k + 1) * D].astype(BF16)
        o_ref[k] = jnp.dot(s, w, preferred_element_type=F32) + b_ref[k]


def _ada_call(c_all, w_ada, b_ada):
    L, D, _ = w_ada.shape
    nseq = c_all.shape[0]
    n_sub = N_MOD // 3
    return pl.pallas_call(
        _ada_kernel,
        grid=(L, n_sub),
        in_specs=[
            pl.BlockSpec((nseq, D), lambda l, k: (0, 0)),
            pl.BlockSpec((None, D, 3 * D), lambda l, k: (l, 0, k)),
            pl.BlockSpec((None, 3, 1, D), lambda l, k: (l, k, 0, 0)),
        ],
        out_specs=pl.BlockSpec((None, 3, nseq, D), lambda l, k: (l, k, 0, 0)),
        out_shape=jax.ShapeDtypeStruct((L, N_MOD, nseq, D), F32),
        compiler_params=_params(("arbitrary", "arbitrary")),
        name="adaln_mod",
    )(c_all, w_ada, b_ada.reshape(L, N_MOD, 1, D))


def _block_diag_rows(t, nblk):
    r, width = t.shape
    c = width // nblk
    tiled = jnp.concatenate([t] * nblk, axis=0)
    row_blk = lax.broadcasted_iota(jnp.int32, tiled.shape, 0) // r
    col_blk = lax.broadcasted_iota(jnp.int32, tiled.shape, 1) // c
    return jnp.where(row_blk == col_blk, tiled, 0.0)


def _prep_kernel(are_ref, aim_ref, ldt_ref, bt_ref, ct_ref, wt_ref,
                 abr_ref, abi_ref, bmat_ref, cmat_ref, wg_ref):
    ar = jnp.minimum(are_ref[...], S5_A_RE_MAX)
    ai = aim_ref[...]
    dt = jnp.exp(ldt_ref[...])
    mag = jnp.exp(ar * dt)
    abr = mag * jnp.cos(ai * dt)
    abi = mag * jnp.sin(ai * dt)
    abr_ref[...] = abr
    abi_ref[...] = abi
    den = ar * ar + ai * ai
    f_r = ((abr - 1.0) * ar + abi * ai) / den
    f_i = (abi * ar - (abr - 1.0) * ai) / den
    b_r = bt_ref[0]
    b_i = bt_ref[1]
    bb = (f_r * b_r - f_i * b_i, f_r * b_i + f_i * b_r)
    hn = bb[0].shape[-1] // 2
    nblk = bmat_ref.shape[-2] // bb[0].shape[0]
    for part in range(2):
        for hf in range(2):
            cols = slice(hf * hn, (hf + 1) * hn)
            bmat_ref[part, hf] = _block_diag_rows(bb[part][:, cols], nblk).astype(BF16)
            cmat_ref[hf, part] = _block_diag_rows(ct_ref[part][:, cols], nblk).T.astype(BF16)
    hd = wt_ref.shape[1]
    hw = wg_ref.shape[1]
    for q in range(2):
        for gate in range(2):
            wg_ref[q, :, gate * hw:(gate + 1) * hw] = _block_diag_rows(
                wt_ref[gate][:, q * hw:(q + 1) * hw], hw // hd).astype(BF16)


def _prep_call(a_re, a_im, log_dt, b_re, b_im, c_re, c_im, w_rg, w_ig):
    L, G, N = a_re.shape
    J = b_re.shape[-1]
    H, HD, _ = w_rg.shape[1:]
    GN, HN, DL = G * N, G * N // 2, H * HD
    row = lambda v: v.reshape(L, 1, GN)
    ldt = jnp.broadcast_to(log_dt[:, :, None], (L, G, N))
    bt = jnp.stack([b_re, b_im], axis=1).transpose(0, 1, 4, 2, 3).reshape(L, 2, J, GN)
    ct = jnp.stack([c_re, c_im], axis=1).transpose(0, 1, 3, 2, 4).reshape(L, 2, J, GN)
    wt = jnp.stack([w_rg, w_ig], axis=1).transpose(0, 1, 3, 2, 4).reshape(L, 2, HD, DL)
    rspec = pl.BlockSpec((None, 1, GN), lambda l: (l, 0, 0))
    jspec = pl.BlockSpec((None, 2, J, GN), lambda l: (l, 0, 0, 0))
    return pl.pallas_call(
        _prep_kernel,
        grid=(L,),
        in_specs=[rspec, rspec, rspec, jspec, jspec,
                  pl.BlockSpec((None, 2, HD, DL), lambda l: (l, 0, 0, 0))],
        out_specs=[rspec, rspec,
                   pl.BlockSpec((None, 2, 2, G // 2 * J, HN), lambda l: (l, 0, 0, 0, 0)),
                   pl.BlockSpec((None, 2, 2, HN, G // 2 * J), lambda l: (l, 0, 0, 0, 0)),
                   pl.BlockSpec((None, 2, DL // 2, DL), lambda l: (l, 0, 0, 0))],
        out_shape=[jax.ShapeDtypeStruct((L, 1, GN), F32), jax.ShapeDtypeStruct((L, 1, GN), F32),
                   jax.ShapeDtypeStruct((L, 2, 2, G // 2 * J, HN), BF16),
                   jax.ShapeDtypeStruct((L, 2, 2, HN, G // 2 * J), BF16),
                   jax.ShapeDtypeStruct((L, 2, DL // 2, DL), BF16)],
        compiler_params=_params(("arbitrary",)),
        name="param_prep",
    )(row(a_re), row(a_im), row(ldt), bt, ct, wt)


def _ffn_matmuls(x, h, inner, gate, w1_ref, w3_ref, w2_ref, gf_ref, final_norm):
    rows3 = lambda v: v.reshape(v.shape[0] // inner, inner, v.shape[1])
    a = jnp.dot(h, w1_ref[...], preferred_element_type=F32)
    b = jnp.dot(h, w3_ref[...], preferred_element_type=F32)
    act = (a * jax.nn.sigmoid(a) * b).astype(BF16)
    f = jnp.dot(act, w2_ref[...], preferred_element_type=F32)
    out = (rows3(x) + (FFN_RES * gate) * rows3(f)).reshape(x.shape)
    if final_norm:
        out = _rmsnorm(out, gf_ref[...])
    return out


def _ffn_rows(x_ref, mod_ref, o_ref, slab_scr, h_scr, wts, nb, final_norm, layout):
    D = x_ref.shape[-1]
    rows = o_ref.shape[0] * o_ref.shape[1] if layout == "ts" else o_ref.shape[0]
    steps = rows // nb
    sub = min(FFN_SUB_ROWS, rows)
    n_slab = D // LANES
    g_ref, w1_ref, w3_ref, w2_ref, gf_ref = wts

    def tile(r0):
        if layout == "st":
            b0, nseq = r0 // steps, sub // steps
            x = x_ref[b0:b0 + nseq].reshape(sub, D)
            return x, steps, [mod_ref[k, b0:b0 + nseq][:, None, :] for k in range(3)]
        return x_ref[r0:r0 + sub, :], nb, [mod_ref[k][None] for k in range(3)]

    for r0 in range(0, rows, sub):
        x, inner, (shift, scale, _) = tile(r0)
        h_scr[r0:r0 + sub, :] = _norm_modulate(x, inner, g_ref[...], shift, scale).astype(BF16)
    for r0 in range(0, rows, sub):
        x, inner, (_, _, gate) = tile(r0)
        out = _ffn_matmuls(x, h_scr[r0:r0 + sub, :], inner, gate, w1_ref, w3_ref, w2_ref,
                           gf_ref, final_norm)
        if layout == "st":
            b0, nseq = r0 // steps, sub // steps
        if layout == "tt":
            o_ref[r0:r0 + sub, :] = out
        elif layout == "st":
            for b in range(b0, b0 + nseq):
                seq = out[(b - b0) * steps:(b - b0 + 1) * steps]
                for s in range(n_slab):
                    slab_scr[s, pl.ds(b, steps, stride=nb), :] = seq[:, s * LANES:(s + 1) * LANES]
        else:
            for s in range(n_slab):
                slab_scr[s, r0:r0 + sub, :] = out[:, s * LANES:(s + 1) * LANES]
            t0, nt = r0 // nb, sub // nb
            for b in range(nb):
                for s in range(n_slab):
                    o_ref[b, t0:t0 + nt, s * LANES:(s + 1) * LANES] = (
                        slab_scr[s, pl.ds(r0 + b, nt, stride=nb), :])
    if layout == "st":
        for s in range(n_slab):
            o_ref[:, s * LANES:(s + 1) * LANES] = slab_scr[s]


def _ffn_kernel(*refs, layer, n_cast, nbp, nbs, n_prompt, final_norm, prompt_layout):
    (xp_ref, xs_ref, modp_ref, mods_ref, g_ref, w1_ref, w3_ref, w2_ref, gf_ref), refs = (
        refs[:9], refs[9:])
    cast_in, (op_ref, os_ref), cast_out, (slab_scr, h_scr) = (
        refs[:n_cast], refs[n_cast:n_cast + 2], refs[n_cast + 2:2 * n_cast + 2], refs[-2:])
    wts = (g_ref.at[layer:layer + 1], w1_ref, w3_ref, w2_ref, gf_ref)

    @pl.when(pl.program_id(0) < n_prompt)
    def _():
        _cast_blocks(cast_in, cast_out)
        _ffn_rows(xp_ref, modp_ref, op_ref, slab_scr, h_scr, wts, nbp, final_norm, prompt_layout)

    @pl.when(pl.program_id(0) == n_prompt)
    def _():
        _ffn_rows(xs_ref, mods_ref, os_ref, slab_scr, h_scr, wts, nbs, final_norm, "tt")


def _ffn_call(xp, xs, nbp, nbs, mod, layer, mod_group, g, w1, w3, w2, gf, final_norm,
              prompt_layout, cast_jobs, name):
    D = xp.shape[-1]
    rows_p = xp.size // D
    rows_s = xs.shape[0]
    tm = min(FFN_ROWS, rows_p)
    assert rows_p % tm == 0 and tm % nbp == 0 and rows_s % nbs == 0 and nbs % nbp == 0
    assert D % LANES == 0 and FFN_SUB_ROWS % (tm // nbp) == 0
    n_prompt = rows_p // tm
    tile_tm = pl.BlockSpec((tm, D), lambda i: (jnp.minimum(i, n_prompt - 1), 0))
    tile_sm = pl.BlockSpec((nbp, tm // nbp, D), lambda i: (0, jnp.minimum(i, n_prompt - 1), 0))
    shape_tm = jax.ShapeDtypeStruct((rows_p, D), F32)
    shape_sm = jax.ShapeDtypeStruct((nbp, rows_p // nbp, D), F32)
    n_cast_steps = _cast_steps(cast_jobs, n_prompt)
    cast_in, cast_out, cast_shapes = _cast_specs(
        cast_jobs, n_cast_steps,
        lambda i: jnp.minimum(i, n_prompt - 1) // (n_prompt // n_cast_steps))
    return pl.pallas_call(
        functools.partial(_ffn_kernel, layer=layer, n_cast=len(cast_jobs), nbp=nbp, nbs=nbs,
                          n_prompt=n_prompt, final_norm=final_norm, prompt_layout=prompt_layout),
        grid=(n_prompt + 1,),
        in_specs=[
            tile_sm if prompt_layout == "st" else tile_tm,
            _const_spec((rows_s, D), (0, 0)),
            _const_spec((None, 3, nbp, D), (layer, mod_group, nbs // nbp, 0)),
            _const_spec((None, 3, nbs, D), (layer, mod_group, 0, 0)),
            _const_spec(g.shape, (0, 0)), _const_spec(w1.shape, (0, 0)),
            _const_spec(w3.shape, (0, 0)),
            _const_spec(w2.shape, (0, 0)),
            _const_spec((1, D), (0, 0)),
        ] + cast_in,
        out_specs=[tile_sm if prompt_layout == "ts" else tile_tm,
                   pl.BlockSpec((rows_s, D), lambda i: (0, 0))] + cast_out,
        out_shape=[shape_sm if prompt_layout == "ts" else shape_tm,
                   jax.ShapeDtypeStruct((rows_s, D), F32)] + cast_shapes,
        scratch_shapes=[pltpu.VMEM((D // LANES, tm, LANES), F32), pltpu.VMEM((tm, D), BF16)],
        compiler_params=_params(("arbitrary",)),
        name=name,
    )(xp, xs, mod, mod, g, w1, w3, w2, gf, *[w for w, _ in cast_jobs])


def _softplus(v):
    return jnp.maximum(v, 0.0) + jnp.log1p(jnp.exp(-jnp.abs(v)))


def _mixer_kernel(*refs, layer, n_init, n_prev, n_cast, nb):
    (x_ref, mod_ref), refs = refs[:2], refs[2:]
    init_refs, refs = refs[:n_init], refs[n_init + n_prev:]
    (gn_ref, win_ref, convw_ref, convb_ref, wg_ref, brg_ref, big_ref, lam_ref,
     abr_ref, abi_ref, bmat_ref, cmat_ref, d_ref, wglu_ref, bglu_ref, wout_ref), refs = (
        refs[:16], refs[16:])
    cast_in, refs = refs[:n_cast], refs[n_cast:]
    (xo_ref, *state_refs), refs = refs[:5], refs[5:]
    convo_ref, ho_ref, sreo_ref, simo_ref = state_refs
    cast_out, (xp_scr, a_scr, b_scr, s_scr) = refs[:n_cast], refs[n_cast:]
    vec = lambda ref: ref[layer:layer + 1, :]
    rows, D = x_ref.shape
    tt = rows // nb
    DL = ho_ref.shape[-1]
    HS = d_ref.shape[-1] // 2
    HN = sreo_ref.shape[-1] // 2

    @pl.when(pl.program_id(0) == 0)
    def _():
        for k, st_ref in enumerate(state_refs):
            st_ref[...] = init_refs[k][...] if init_refs else jnp.zeros(st_ref.shape, F32)

    x = x_ref[...]
    h = _norm_modulate(x, nb, vec(gn_ref), mod_ref[0][None], mod_ref[1][None])
    z = jnp.dot(h.astype(BF16), win_ref[...], preferred_element_type=F32)
    xb = z[:, :DL]
    yb = z[:, DL:2 * DL]
    u = z[:, 2 * DL:]

    halo = (CONV_W - 1) * nb
    xp_scr[0:halo, :] = convo_ref[...]
    xp_scr[halo:halo + rows, :] = xb
    cw = convw_ref[...]
    xc = vec(convb_ref) + sum(xp_scr[k * nb:k * nb + rows, :] * cw[k:k + 1, :]
                              for k in range(CONV_W))
    convo_ref[...] = xp_scr[rows:rows + halo, :]

    xcb = xc.astype(BF16)
    hw = DL // 2
    gq = [jnp.dot(xcb[:, q * hw:(q + 1) * hw], wg_ref[q], preferred_element_type=F32)
          for q in range(2)]
    r = jax.nn.sigmoid(jnp.concatenate([g[:, :hw] for g in gq], axis=-1) + vec(brg_ref))
    ig = jax.nn.sigmoid(jnp.concatenate([g[:, hw:] for g in gq], axis=-1) + vec(big_ref))
    log_a = -C_GATE * r * _softplus(-vec(lam_ref))
    a = jnp.exp(log_a)
    mult = _sqrt_nonneg(-jnp.tanh(log_a) * (a * a + 1.0))
    a_scr[...] = a
    b_scr[...] = mult * ig * xc

    for r0 in range(0, nb, SCAN_ROWS):
        hcur = ho_ref[r0:r0 + SCAN_ROWS, :]
        for t in range(tt):
            sl = slice(t * nb + r0, t * nb + r0 + SCAN_ROWS)
            hcur = a_scr[sl, :] * hcur + b_scr[sl, :]
            b_scr[sl, :] = hcur
        ho_ref[r0:r0 + SCAN_ROWS, :] = hcur
    y_lru = jax.nn.gelu(yb) * b_scr[...]

    ub = u.astype(BF16)
    for hf in range(2):
        for part in range(2):
            s_scr[hf, :, part * HN:(part + 1) * HN] = jnp.dot(
                ub[:, hf * HS:(hf + 1) * HS], bmat_ref[part, hf], preferred_element_type=F32)

    for hf in range(2):
        for c0 in range(0, HN, SCAN_LANES):
            st_l = slice(hf * HN + c0, hf * HN + c0 + SCAN_LANES)
            re_l = slice(c0, c0 + SCAN_LANES)
            im_l = slice(HN + c0, HN + c0 + SCAN_LANES)
            a_r = jnp.broadcast_to(abr_ref[:, st_l], (SCAN_ROWS, SCAN_LANES))
            a_i = jnp.broadcast_to(abi_ref[:, st_l], (SCAN_ROWS, SCAN_LANES))
            for r0 in range(0, nb, SCAN_ROWS):
                s_re = sreo_ref[r0:r0 + SCAN_ROWS, st_l]
                s_im = simo_ref[r0:r0 + SCAN_ROWS, st_l]
                for t in range(tt):
                    sl = slice(t * nb + r0, t * nb + r0 + SCAN_ROWS)
                    n_re = a_r * s_re - a_i * s_im + s_scr[hf, sl, re_l]
                    n_im = a_r * s_im + a_i * s_re + s_scr[hf, sl, im_l]
                    s_re, s_im = n_re, n_im
                    s_scr[hf, sl, re_l] = s_re
                    s_scr[hf, sl, im_l] = s_im
                sreo_ref[r0:r0 + SCAN_ROWS, st_l] = s_re
                simo_ref[r0:r0 + SCAN_ROWS, st_l] = s_im

    ys = []
    for hf in range(2):
        sb = s_scr[hf].astype(BF16)
        ys.append(jnp.dot(sb[:, :HN], cmat_ref[hf, 0], preferred_element_type=F32)
                  - jnp.dot(sb[:, HN:], cmat_ref[hf, 1], preferred_element_type=F32))
    ys = jnp.concatenate(ys, axis=-1) + vec(d_ref) * u
    g = jax.nn.gelu(ys)
    y_s5 = g * jax.nn.sigmoid(
        jnp.dot(g.astype(BF16), wglu_ref[...], preferred_element_type=F32) + vec(bglu_ref))

    ycat = jnp.concatenate([y_lru, y_s5], axis=-1).astype(BF16)
    out = jnp.dot(ycat, wout_ref[...], preferred_element_type=F32)
    xo_ref[...] = (_rows3(x, nb) + mod_ref[2][None] * _rows3(out, nb)).reshape(rows, D)
    _cast_blocks(cast_in, cast_out)


def _mixer_call(x, mod, layer, mod_rowblk, nb, init_states, prev_states, state_shapes, wts,
                cast_jobs, name):
    rows, D = x.shape
    tm = min(MIXER_ROWS, rows)
    assert rows % tm == 0 and tm % nb == 0 and tm // nb >= CONV_W - 1
    DL, GN = state_shapes[1][-1], state_shapes[2][-1]
    halo = (CONV_W - 1) * nb

    def wt_spec(w, kind):
        return _layer_spec(w, layer) if kind == "stacked" else _const_spec(w.shape, (0,) * w.ndim)

    n_lead = 2 + len(init_states)
    state_out_specs = [pl.BlockSpec((None,) + s[1:], lambda i: (layer, 0, 0))
                       for s in state_shapes]
    grid = rows // tm
    n_cast_steps = _cast_steps(cast_jobs, grid)
    cast_in, cast_out, cast_shapes = _cast_specs(cast_jobs, n_cast_steps,
                                                 lambda i: i // (grid // n_cast_steps))
    outs = pl.pallas_call(
        functools.partial(_mixer_kernel, layer=layer, n_init=len(init_states),
                          n_prev=len(prev_states), n_cast=len(cast_jobs), nb=nb),
        grid=(grid,),
        in_specs=[pl.BlockSpec((tm, D), lambda i: (i, 0)),
                  _const_spec((None, 3, nb, D), (layer, 1, mod_rowblk, 0))]
                 + [_layer_spec(s, layer) for s in init_states]
                 + [pl.BlockSpec(memory_space=pl.ANY)] * len(prev_states)
                 + [wt_spec(w, kind) for w, kind in wts] + cast_in,
        out_specs=[pl.BlockSpec((tm, D), lambda i: (i, 0))] + state_out_specs + cast_out,
        out_shape=[jax.ShapeDtypeStruct((rows, D), F32)]
                  + [jax.ShapeDtypeStruct(s, F32) for s in state_shapes] + cast_shapes,
        input_output_aliases={n_lead + k: 1 + k for k in range(len(prev_states))},
        scratch_shapes=[pltpu.VMEM((tm + halo, DL), F32),
                        pltpu.VMEM((tm, DL), F32),
                        pltpu.VMEM((tm, DL), F32),
                        pltpu.VMEM((2, tm, GN), F32)],
        compiler_params=_params(("arbitrary",)),
        name=name,
    )(x, mod, *init_states, *prev_states, *[w for w, _ in wts], *[w for w, _ in cast_jobs])
    n_st = len(state_shapes)
    return outs[0], tuple(outs[1:1 + n_st]), tuple(outs[1 + n_st:])


def kernel(x_prompt, x_sample, c_prompt, c_sample, state_lru_conv, state_lru_h, state_s5_re, state_s5_im, w_ada, b_ada, norm_ffn1, w1_ffn1, w3_ffn1, w2_ffn1, norm_mix, w_in, conv_w, conv_b, w_rg, b_rg, w_ig, b_ig, lru_lambda, s5_a_re, s5_a_im, s5_log_dt, s5_b_re, s5_b_im, s5_c_re, s5_c_im, s5_d, w_glu, b_glu, w_out, norm_ffn2, w1_ffn2, w3_ffn2, w2_ffn2, norm_final):
    L, D, _ = w_ada.shape
    Bp, Tp, _ = x_prompt.shape
    Bs, Ts, _ = x_sample.shape
    _, G, N = s5_a_re.shape
    DL = lru_lambda.shape[-1]
    GN = G * N
    halo = CONV_W - 1
    assert Bs % Bp == 0 and Bp % SCAN_ROWS == 0 and G % 2 == 0

    mod = _ada_call(jnp.concatenate([c_sample, c_prompt], axis=0), w_ada, b_ada)
    abr, abi, bmat, cmat, wgate = _prep_call(s5_a_re, s5_a_im, s5_log_dt, s5_b_re, s5_b_im,
                                             s5_c_re, s5_c_im, w_rg, w_ig)

    gfin = norm_final.reshape(1, D)
    ffn1_f32 = (w1_ffn1, w3_ffn1, w2_ffn1)
    ffn2_f32 = (w1_ffn2, w3_ffn2, w2_ffn2)
    mix_f32 = (w_in, w_glu, w_out)

    def mixer_wts(w_in_b, w_glu_b, w_out_b):
        return [(norm_mix, "rows"), (w_in_b, "layer"), (conv_w, "stacked"), (conv_b, "rows"),
                (wgate, "stacked"), (b_rg, "rows"), (b_ig, "rows"), (lru_lambda, "rows"),
                (abr, "stacked"), (abi, "stacked"), (bmat, "stacked"), (cmat, "stacked"),
                (s5_d, "rows"), (w_glu_b, "layer"), (b_glu, "rows"), (w_out_b, "layer")]

    def time_major(x):
        return jnp.swapaxes(x, 0, 1).reshape(x.shape[0] * x.shape[1], D)

    def state_shapes(nb):
        return ((L, halo * nb, DL), (L, nb, DL), (L, nb, GN), (L, nb, GN))

    init_s = (jnp.swapaxes(state_lru_conv, 1, 2).reshape(L, halo * Bs, DL), state_lru_h,
              state_s5_re.reshape(L, Bs, GN), state_s5_im.reshape(L, Bs, GN))
    st_p = tuple(jnp.zeros(s, F32) for s in state_shapes(Bp))
    st_s = tuple(jnp.zeros(s, F32) for s in state_shapes(Bs))
    xp, xs = x_prompt, time_major(x_sample)
    ffn1_b = tuple(w[0].astype(BF16) for w in ffn1_f32)
    for l in range(L):
        xp, xs, *mix_b = _ffn_call(xp, xs, Bp, Bs, mod, l, 0, norm_ffn1, *ffn1_b, gfin,
                                   False, "st" if l == 0 else "tt", [(w, l) for w in mix_f32],
                                   f"ffn1_{l}")
        jobs = [(w, l) for w in ffn2_f32] + [(w, l + 1) for w in ffn1_f32 if l + 1 < L]
        xp, st_p, cast = _mixer_call(xp, mod, l, Bs // Bp, Bp, (), st_p, state_shapes(Bp),
                                     mixer_wts(*mix_b), jobs, f"mixer_p{l}")
        ffn2_b, ffn1_b = cast[:3], cast[3:]
        xs, st_s, _ = _mixer_call(xs, mod, l, 0, Bs, init_s, st_s, state_shapes(Bs),
                                  mixer_wts(*mix_b), [], f"mixer_s{l}")
        xp, xs = _ffn_call(xp, xs, Bp, Bs, mod, l, 2, norm_ffn2, *ffn2_b, gfin,
                           l == L - 1, "ts" if l == L - 1 else "tt", [], f"ffn2_{l}")

    def finish(y, st, nb):
        conv, hh, sre, sim = st
        return (y, jnp.swapaxes(conv.reshape(L, halo, nb, DL), 1, 2), hh,
                sre.reshape(L, nb, G, N), sim.reshape(L, nb, G, N))

    p_out = finish(xp, st_p, Bp)
    s_out = finish(jnp.swapaxes(xs.reshape(Ts, Bs, D), 0, 1), st_s, Bs)
    return (p_out[0], s_out[0]) + p_out[1:] + s_out[1:]
```

```python
import functools

import jax
import jax.numpy as jnp
from jax import lax
from jax.experimental import pallas as pl
from jax.experimental.pallas import tpu as pltpu

EPS = 1e-6
C_GATE = 8.0
FFN_RES = 0.5
CONV_W = 4
N_MOD = 9
S5_A_RE_MAX = -1e-4
BF16 = jnp.bfloat16
F32 = jnp.float32

VMEM_LIMIT_BYTES = 58 * 1024 * 1024
FFN_ROWS = 1024
FFN_SUB_ROWS = 1024
MIXER_ROWS = 512
SCAN_ROWS = 8
SCAN_LANES = 512
LANES = 128
BF16_ROWS = 16


def _const_spec(shape, index):
    return pl.BlockSpec(shape, lambda i: index, pipeline_mode=pl.Buffered(1))


def _layer_spec(w, layer):
    return _const_spec((None,) + w.shape[1:], (layer,) + (0,) * (w.ndim - 1))


def _params(semantics):
    return pltpu.CompilerParams(dimension_semantics=semantics,
                                vmem_limit_bytes=VMEM_LIMIT_BYTES)


def _cast_steps(jobs, grid):
    n = grid
    while n > 1 and (grid % n or any(w.shape[1] % (BF16_ROWS * n) for w, _ in jobs)):
        n -= 1
    return n


def _cast_specs(jobs, n_steps, step_of):
    ins, outs, shapes = [], [], []
    for w, layer in jobs:
        _, R, C = w.shape
        assert R % (BF16_ROWS * n_steps) == 0
        ins.append(pl.BlockSpec((None, R // n_steps, C),
                                lambda i, layer=layer: (layer, step_of(i), 0)))
        outs.append(pl.BlockSpec((R // n_steps, C), lambda i: (step_of(i), 0)))
        shapes.append(jax.ShapeDtypeStruct((R, C), BF16))
    return ins, outs, shapes


def _cast_blocks(in_refs, out_refs):
    for src_ref, dst_ref in zip(in_refs, out_refs):
        dst_ref[...] = src_ref[...].astype(BF16)


def _rmsnorm(x, g):
    return x * lax.rsqrt(jnp.mean(x * x, axis=-1, keepdims=True) + EPS) * g


def _norm_modulate(x, inner, g, shift, scale):
    xn = x * lax.rsqrt(jnp.mean(x * x, axis=-1, keepdims=True) + EPS)
    xn = xn.reshape(x.shape[0] // inner, inner, x.shape[1])
    return (xn * (g * (1.0 + scale)) + shift).reshape(x.shape)


def _sqrt_nonneg(v):
    return jnp.where(v > 0.0, v * lax.rsqrt(v), v)


def _rows3(v, nb):
    return v.reshape(v.shape[0] // nb, nb, v.shape[1])


def _ada_kernel(c_ref, w_ref, b_ref, o_ref):
    c = c_ref[...]
    s = (c * jax.nn.sigmoid(c)).astype(BF16)
    D = c.shape[-1]
    for k in range(o_ref.shape[0]):
        w = w_ref[:, k * D:(k + 1) * D].astype(BF16)
        o_ref[k] = jnp.dot(s, w, preferred_element_type=F32) + b_ref[k]


def _ada_call(c_all, w_ada, b_ada):
    L, D, _ = w_ada.shape
    nseq = c_all.shape[0]
    n_sub = N_MOD // 3
    return pl.pallas_call(
        _ada_kernel,
        grid=(L, n_sub),
        in_specs=[
            pl.BlockSpec((nseq, D), lambda l, k: (0, 0)),
            pl.BlockSpec((None, D, 3 * D), lambda l, k: (l, 0, k)),
            pl.BlockSpec((None, 3, 1, D), lambda l, k: (l, k, 0, 0)),
        ],
        out_specs=pl.BlockSpec((None, 3, nseq, D), lambda l, k: (l, k, 0, 0)),
        out_shape=jax.ShapeDtypeStruct((L, N_MOD, nseq, D), F32),
        compiler_params=_params(("arbitrary", "arbitrary")),
        name="adaln_mod",
    )(c_all, w_ada, b_ada.reshape(L, N_MOD, 1, D))


def _block_diag_rows(t, nblk):
    r, width = t.shape
    c = width // nblk
    tiled = jnp.concatenate([t] * nblk, axis=0)
    row_blk = lax.broadcasted_iota(jnp.int32, tiled.shape, 0) // r
    col_blk = lax.broadcasted_iota(jnp.int32, tiled.shape, 1) // c
    return jnp.where(row_blk == col_blk, tiled, 0.0)


def _prep_kernel(are_ref, aim_ref, ldt_ref, bt_ref, ct_ref, wt_ref,
                 abr_ref, abi_ref, bmat_ref, cmat_ref, wg_ref):
    ar = jnp.minimum(are_ref[...], S5_A_RE_MAX)
    ai = aim_ref[...]
    dt = jnp.exp(ldt_ref[...])
    mag = jnp.exp(ar * dt)
    abr = mag * jnp.cos(ai * dt)
    abi = mag * jnp.sin(ai * dt)
    abr_ref[...] = abr
    abi_ref[...] = abi
    den = ar * ar + ai * ai
    f_r = ((abr - 1.0) * ar + abi * ai) / den
    f_i = (abi * ar - (abr - 1.0) * ai) / den
    b_r = bt_ref[0]
    b_i = bt_ref[1]
    bb = (f_r * b_r - f_i * b_i, f_r * b_i + f_i * b_r)
    hn = bb[0].shape[-1] // 2
    nblk = bmat_ref.shape[-2] // bb[0].shape[0]
    for part in range(2):
        for hf in range(2):
            cols = slice(hf * hn, (hf + 1) * hn)
            bmat_ref[part, hf] = _block_diag_rows(bb[part][:, cols], nblk).astype(BF16)
            cmat_ref[hf, part] = _block_diag_rows(ct_ref[part][:, cols], nblk).T.astype(BF16)
    heads = wg_ref.shape[0] // wt_ref.shape[1]
    dl = wg_ref.shape[0]
    for gate in range(2):
        wg_ref[:, gate * dl:(gate + 1) * dl] = _block_diag_rows(wt_ref[gate], heads).astype(BF16)


def _prep_call(a_re, a_im, log_dt, b_re, b_im, c_re, c_im, w_rg, w_ig):
    L, G, N = a_re.shape
    J = b_re.shape[-1]
    H, HD, _ = w_rg.shape[1:]
    GN, HN, DL = G * N, G * N // 2, H * HD
    row = lambda v: v.reshape(L, 1, GN)
    ldt = jnp.broadcast_to(log_dt[:, :, None], (L, G, N))
    bt = jnp.stack([b_re, b_im], axis=1).transpose(0, 1, 4, 2, 3).reshape(L, 2, J, GN)
    ct = jnp.stack([c_re, c_im], axis=1).transpose(0, 1, 3, 2, 4).reshape(L, 2, J, GN)
    wt = jnp.stack([w_rg, w_ig], axis=1).transpose(0, 1, 3, 2, 4).reshape(L, 2, HD, DL)
    rspec = pl.BlockSpec((None, 1, GN), lambda l: (l, 0, 0))
    jspec = pl.BlockSpec((None, 2, J, GN), lambda l: (l, 0, 0, 0))
    return pl.pallas_call(
        _prep_kernel,
        grid=(L,),
        in_specs=[rspec, rspec, rspec, jspec, jspec,
                  pl.BlockSpec((None, 2, HD, DL), lambda l: (l, 0, 0, 0))],
        out_specs=[rspec, rspec,
                   pl.BlockSpec((None, 2, 2, G // 2 * J, HN), lambda l: (l, 0, 0, 0, 0)),
                   pl.BlockSpec((None, 2, 2, HN, G // 2 * J), lambda l: (l, 0, 0, 0, 0)),
                   pl.BlockSpec((None, DL, 2 * DL), lambda l: (l, 0, 0))],
        out_shape=[jax.ShapeDtypeStruct((L, 1, GN), F32), jax.ShapeDtypeStruct((L, 1, GN), F32),
                   jax.ShapeDtypeStruct((L, 2, 2, G // 2 * J, HN), BF16),
                   jax.ShapeDtypeStruct((L, 2, 2, HN, G // 2 * J), BF16),
                   jax.ShapeDtypeStruct((L, DL, 2 * DL), BF16)],
        compiler_params=_params(("arbitrary",)),
        name="param_prep",
    )(row(a_re), row(a_im), row(ldt), bt, ct, wt)


def _ffn_matmuls(x, h, inner, gate, w1_ref, w3_ref, w2_ref, gf_ref, final_norm):
    rows3 = lambda v: v.reshape(v.shape[0] // inner, inner, v.shape[1])
    a = jnp.dot(h, w1_ref[...], preferred_element_type=F32)
    b = jnp.dot(h, w3_ref[...], preferred_element_type=F32)
    act = (a * jax.nn.sigmoid(a) * b).astype(BF16)
    f = jnp.dot(act, w2_ref[...], preferred_element_type=F32)
    out = (rows3(x) + (FFN_RES * gate) * rows3(f)).reshape(x.shape)
    if final_norm:
        out = _rmsnorm(out, gf_ref[...])
    return out


def _ffn_rows(x_ref, mod_ref, o_ref, slab_scr, h_scr, wts, nb, final_norm, layout):
    D = x_ref.shape[-1]
    rows = o_ref.shape[0] * o_ref.shape[1] if layout == "ts" else o_ref.shape[0]
    steps = rows // nb
    sub = min(FFN_SUB_ROWS, rows)
    n_slab = D // LANES
    g_ref, w1_ref, w3_ref, w2_ref, gf_ref = wts

    def tile(r0):
        if layout == "st":
            b0, nseq = r0 // steps, sub // steps
            x = x_ref[b0:b0 + nseq].reshape(sub, D)
            return x, steps, [mod_ref[k, b0:b0 + nseq][:, None, :] for k in range(3)]
        return x_ref[r0:r0 + sub, :], nb, [mod_ref[k][None] for k in range(3)]

    for r0 in range(0, rows, sub):
        x, inner, (shift, scale, _) = tile(r0)
        h_scr[r0:r0 + sub, :] = _norm_modulate(x, inner, g_ref[...], shift, scale).astype(BF16)
    for r0 in range(0, rows, sub):
        x, inner, (_, _, gate) = tile(r0)
        out = _ffn_matmuls(x, h_scr[r0:r0 + sub, :], inner, gate, w1_ref, w3_ref, w2_ref,
                           gf_ref, final_norm)
        if layout == "st":
            b0, nseq = r0 // steps, sub // steps
        if layout == "tt":
            o_ref[r0:r0 + sub, :] = out
        elif layout == "st":
            for b in range(b0, b0 + nseq):
                seq = out[(b - b0) * steps:(b - b0 + 1) * steps]
                for s in range(n_slab):
                    slab_scr[s, pl.ds(b, steps, stride=nb), :] = seq[:, s * LANES:(s + 1) * LANES]
        else:
            for s in range(n_slab):
                slab_scr[s, r0:r0 + sub, :] = out[:, s * LANES:(s + 1) * LANES]
            t0, nt = r0 // nb, sub // nb
            for b in range(nb):
                for s in range(n_slab):
                    o_ref[b, t0:t0 + nt, s * LANES:(s + 1) * LANES] = (
                        slab_scr[s, pl.ds(r0 + b, nt, stride=nb), :])
    if layout == "st":
        for s in range(n_slab):
            o_ref[:, s * LANES:(s + 1) * LANES] = slab_scr[s]


def _ffn_kernel(*refs, layer, n_cast, nbp, nbs, n_prompt, final_norm, prompt_layout):
    (xp_ref, xs_ref, modp_ref, mods_ref, g_ref, w1_ref, w3_ref, w2_ref, gf_ref), refs = (
        refs[:9], refs[9:])
    cast_in, (op_ref, os_ref), cast_out, (slab_scr, h_scr) = (
        refs[:n_cast], refs[n_cast:n_cast + 2], refs[n_cast + 2:2 * n_cast + 2], refs[-2:])
    wts = (g_ref.at[layer:layer + 1], w1_ref, w3_ref, w2_ref, gf_ref)

    @pl.when(pl.program_id(0) < n_prompt)
    def _():
        _cast_blocks(cast_in, cast_out)
        _ffn_rows(xp_ref, modp_ref, op_ref, slab_scr, h_scr, wts, nbp, final_norm, prompt_layout)

    @pl.when(pl.program_id(0) == n_prompt)
    def _():
        _ffn_rows(xs_ref, mods_ref, os_ref, slab_scr, h_scr, wts, nbs, final_norm, "tt")


def _ffn_call(xp, xs, nbp, nbs, mod, layer, mod_group, g, w1, w3, w2, gf, final_norm,
              prompt_layout, cast_jobs, name):
    D = xp.shape[-1]
    rows_p = xp.size // D
    rows_s = xs.shape[0]
    tm = min(FFN_ROWS, rows_p)
    assert rows_p % tm == 0 and tm % nbp == 0 and rows_s % nbs == 0 and nbs % nbp == 0
    assert D % LANES == 0 and FFN_SUB_ROWS % (tm // nbp) == 0
    n_prompt = rows_p // tm
    tile_tm = pl.BlockSpec((tm, D), lambda i: (jnp.minimum(i, n_prompt - 1), 0))
    tile_sm = pl.BlockSpec((nbp, tm // nbp, D), lambda i: (0, jnp.minimum(i, n_prompt - 1), 0))
    shape_tm = jax.ShapeDtypeStruct((rows_p, D), F32)
    shape_sm = jax.ShapeDtypeStruct((nbp, rows_p // nbp, D), F32)
    n_cast_steps = _cast_steps(cast_jobs, n_prompt)
    cast_in, cast_out, cast_shapes = _cast_specs(
        cast_jobs, n_cast_steps,
        lambda i: jnp.minimum(i, n_prompt - 1) // (n_prompt // n_cast_steps))
    return pl.pallas_call(
        functools.partial(_ffn_kernel, layer=layer, n_cast=len(cast_jobs), nbp=nbp, nbs=nbs,
                          n_prompt=n_prompt, final_norm=final_norm, prompt_layout=prompt_layout),
        grid=(n_prompt + 1,),
        in_specs=[
            tile_sm if prompt_layout == "st" else tile_tm,
            _const_spec((rows_s, D), (0, 0)),
            _const_spec((None, 3, nbp, D), (layer, mod_group, nbs // nbp, 0)),
            _const_spec((None, 3, nbs, D), (layer, mod_group, 0, 0)),
            _const_spec(g.shape, (0, 0)), _const_spec(w1.shape, (0, 0)),
            _const_spec(w3.shape, (0, 0)),
            _const_spec(w2.shape, (0, 0)),
            _const_spec((1, D), (0, 0)),
        ] + cast_in,
        out_specs=[tile_sm if prompt_layout == "ts" else tile_tm,
                   pl.BlockSpec((rows_s, D), lambda i: (0, 0))] + cast_out,
        out_shape=[shape_sm if prompt_layout == "ts" else shape_tm,
                   jax.ShapeDtypeStruct((rows_s, D), F32)] + cast_shapes,
        scratch_shapes=[pltpu.VMEM((D // LANES, tm, LANES), F32), pltpu.VMEM((tm, D), BF16)],
        compiler_params=_params(("arbitrary",)),
        name=name,
    )(xp, xs, mod, mod, g, w1, w3, w2, gf, *[w for w, _ in cast_jobs])


def _softplus(v):
    return jnp.maximum(v, 0.0) + jnp.log1p(jnp.exp(-jnp.abs(v)))


def _mixer_kernel(*refs, layer, n_init, n_prev, n_cast, nb):
    (x_ref, mod_ref), refs = refs[:2], refs[2:]
    init_refs, refs = refs[:n_init], refs[n_init + n_prev:]
    (gn_ref, win_ref, convw_ref, convb_ref, wg_ref, brg_ref, big_ref, lam_ref,
     abr_ref, abi_ref, bmat_ref, cmat_ref, d_ref, wglu_ref, bglu_ref, wout_ref), refs = (
        refs[:16], refs[16:])
    cast_in, refs = refs[:n_cast], refs[n_cast:]
    (xo_ref, *state_refs), refs = refs[:5], refs[5:]
    convo_ref, ho_ref, sreo_ref, simo_ref = state_refs
    cast_out, (xp_scr, a_scr, b_scr, s_scr) = refs[:n_cast], refs[n_cast:]
    vec = lambda ref: ref[layer:layer + 1, :]
    rows, D = x_ref.shape
    tt = rows // nb
    DL = ho_ref.shape[-1]
    HS = d_ref.shape[-1] // 2
    HN = sreo_ref.shape[-1] // 2

    @pl.when(pl.program_id(0) == 0)
    def _():
        for k, st_ref in enumerate(state_refs):
            st_ref[...] = init_refs[k][...] if init_refs else jnp.zeros(st_ref.shape, F32)

    _cast_blocks(cast_in, cast_out)
    x = x_ref[...]
    h = _norm_modulate(x, nb, vec(gn_ref), mod_ref[0][None], mod_ref[1][None])
    z = jnp.dot(h.astype(BF16), win_ref[...], preferred_element_type=F32)
    xb = z[:, :DL]
    yb = z[:, DL:2 * DL]
    u = z[:, 2 * DL:]

    halo = (CONV_W - 1) * nb
    xp_scr[0:halo, :] = convo_ref[...]
    xp_scr[halo:halo + rows, :] = xb
    cw = convw_ref[...]
    xc = vec(convb_ref) + sum(xp_scr[k * nb:k * nb + rows, :] * cw[k:k + 1, :]
                              for k in range(CONV_W))
    convo_ref[...] = xp_scr[rows:rows + halo, :]

    gpre = jnp.dot(xc.astype(BF16), wg_ref[...], preferred_element_type=F32)
    r = jax.nn.sigmoid(gpre[:, :DL] + vec(brg_ref))
    ig = jax.nn.sigmoid(gpre[:, DL:] + vec(big_ref))
    log_a = -C_GATE * r * _softplus(-vec(lam_ref))
    a = jnp.exp(log_a)
    mult = _sqrt_nonneg(-jnp.tanh(log_a) * (a * a + 1.0))
    a_scr[...] = a
    b_scr[...] = mult * ig * xc

    for r0 in range(0, nb, SCAN_ROWS):
        hcur = ho_ref[r0:r0 + SCAN_ROWS, :]
        for t in range(tt):
            sl = slice(t * nb + r0, t * nb + r0 + SCAN_ROWS)
            hcur = a_scr[sl, :] * hcur + b_scr[sl, :]
            b_scr[sl, :] = hcur
        ho_ref[r0:r0 + SCAN_ROWS, :] = hcur
    y_lru = jax.nn.gelu(yb) * b_scr[...]

    ub = u.astype(BF16)
    for hf in range(2):
        for part in range(2):
            s_scr[hf, :, part * HN:(part + 1) * HN] = jnp.dot(
                ub[:, hf * HS:(hf + 1) * HS], bmat_ref[part, hf], preferred_element_type=F32)

    for hf in range(2):
        for c0 in range(0, HN, SCAN_LANES):
            st_l = slice(hf * HN + c0, hf * HN + c0 + SCAN_LANES)
            re_l = slice(c0, c0 + SCAN_LANES)
            im_l = slice(HN + c0, HN + c0 + SCAN_LANES)
            a_r = jnp.broadcast_to(abr_ref[:, st_l], (SCAN_ROWS, SCAN_LANES))
            a_i = jnp.broadcast_to(abi_ref[:, st_l], (SCAN_ROWS, SCAN_LANES))
            for r0 in range(0, nb, SCAN_ROWS):
                s_re = sreo_ref[r0:r0 + SCAN_ROWS, st_l]
                s_im = simo_ref[r0:r0 + SCAN_ROWS, st_l]
                for t in range(tt):
                    sl = slice(t * nb + r0, t * nb + r0 + SCAN_ROWS)
                    n_re = a_r * s_re - a_i * s_im + s_scr[hf, sl, re_l]
                    n_im = a_r * s_im + a_i * s_re + s_scr[hf, sl, im_l]
                    s_re, s_im = n_re, n_im
                    s_scr[hf, sl, re_l] = s_re
                    s_scr[hf, sl, im_l] = s_im
                sreo_ref[r0:r0 + SCAN_ROWS, st_l] = s_re
                simo_ref[r0:r0 + SCAN_ROWS, st_l] = s_im

    ys = []
    for hf in range(2):
        sb = s_scr[hf].astype(BF16)
        ys.append(jnp.dot(sb[:, :HN], cmat_ref[hf, 0], preferred_element_type=F32)
                  - jnp.dot(sb[:, HN:], cmat_ref[hf, 1], preferred_element_type=F32))
    ys = jnp.concatenate(ys, axis=-1) + vec(d_ref) * u
    g = jax.nn.gelu(ys)
    y_s5 = g * jax.nn.sigmoid(
        jnp.dot(g.astype(BF16), wglu_ref[...], preferred_element_type=F32) + vec(bglu_ref))

    ycat = jnp.concatenate([y_lru, y_s5], axis=-1).astype(BF16)
    out = jnp.dot(ycat, wout_ref[...], preferred_element_type=F32)
    xo_ref[...] = (_rows3(x, nb) + mod_ref[2][None] * _rows3(out, nb)).reshape(rows, D)


def _mixer_call(x, mod, layer, mod_rowblk, nb, init_states, prev_states, state_shapes, wts,
                cast_jobs, name):
    rows, D = x.shape
    tm = min(MIXER_ROWS, rows)
    assert rows % tm == 0 and tm % nb == 0 and tm // nb >= CONV_W - 1
    DL, GN = state_shapes[1][-1], state_shapes[2][-1]
    halo = (CONV_W - 1) * nb

    def wt_spec(w, kind):
        return _layer_spec(w, layer) if kind == "stacked" else _const_spec(w.shape, (0,) * w.ndim)

    n_lead = 2 + len(init_states)
    state_out_specs = [pl.BlockSpec((None,) + s[1:], lambda i: (layer, 0, 0))
                       for s in state_shapes]
    grid = rows // tm
    n_cast_steps = _cast_steps(cast_jobs, grid)
    cast_in, cast_out, cast_shapes = _cast_specs(cast_jobs, n_cast_steps,
                                                 lambda i: i // (grid // n_cast_steps))
    outs = pl.pallas_call(
        functools.partial(_mixer_kernel, layer=layer, n_init=len(init_states),
                          n_prev=len(prev_states), n_cast=len(cast_jobs), nb=nb),
        grid=(grid,),
        in_specs=[pl.BlockSpec((tm, D), lambda i: (i, 0)),
                  _const_spec((None, 3, nb, D), (layer, 1, mod_rowblk, 0))]
                 + [_layer_spec(s, layer) for s in init_states]
                 + [pl.BlockSpec(memory_space=pl.ANY)] * len(prev_states)
                 + [wt_spec(w, kind) for w, kind in wts] + cast_in,
        out_specs=[pl.BlockSpec((tm, D), lambda i: (i, 0))] + state_out_specs + cast_out,
        out_shape=[jax.ShapeDtypeStruct((rows, D), F32)]
                  + [jax.ShapeDtypeStruct(s, F32) for s in state_shapes] + cast_shapes,
        input_output_aliases={n_lead + k: 1 + k for k in range(len(prev_states))},
        scratch_shapes=[pltpu.VMEM((tm + halo, DL), F32),
                        pltpu.VMEM((tm, DL), F32),
                        pltpu.VMEM((tm, DL), F32),
                        pltpu.VMEM((2, tm, GN), F32)],
        compiler_params=_params(("arbitrary",)),
        name=name,
    )(x, mod, *init_states, *prev_states, *[w for w, _ in wts], *[w for w, _ in cast_jobs])
    n_st = len(state_shapes)
    return outs[0], tuple(outs[1:1 + n_st]), tuple(outs[1 + n_st:])


def kernel(x_prompt, x_sample, c_prompt, c_sample, state_lru_conv, state_lru_h, state_s5_re, state_s5_im, w_ada, b_ada, norm_ffn1, w1_ffn1, w3_ffn1, w2_ffn1, norm_mix, w_in, conv_w, conv_b, w_rg, b_rg, w_ig, b_ig, lru_lambda, s5_a_re, s5_a_im, s5_log_dt, s5_b_re, s5_b_im, s5_c_re, s5_c_im, s5_d, w_glu, b_glu, w_out, norm_ffn2, w1_ffn2, w3_ffn2, w2_ffn2, norm_final):
    L, D, _ = w_ada.shape
    Bp, Tp, _ = x_prompt.shape
    Bs, Ts, _ = x_sample.shape
    _, G, N = s5_a_re.shape
    DL = lru_lambda.shape[-1]
    GN = G * N
    halo = CONV_W - 1
    assert Bs % Bp == 0 and Bp % SCAN_ROWS == 0 and G % 2 == 0

    mod = _ada_call(jnp.concatenate([c_sample, c_prompt], axis=0), w_ada, b_ada)
    abr, abi, bmat, cmat, wgate = _prep_call(s5_a_re, s5_a_im, s5_log_dt, s5_b_re, s5_b_im,
                                             s5_c_re, s5_c_im, w_rg, w_ig)

    gfin = norm_final.reshape(1, D)
    ffn1_f32 = (w1_ffn1, w3_ffn1, w2_ffn1)
    ffn2_f32 = (w1_ffn2, w3_ffn2, w2_ffn2)
    mix_f32 = (w_in, w_glu, w_out)

    def mixer_wts(w_in_b, w_glu_b, w_out_b):
        return [(norm_mix, "rows"), (w_in_b, "layer"), (conv_w, "stacked"), (conv_b, "rows"),
                (wgate, "stacked"), (b_rg, "rows"), (b_ig, "rows"), (lru_lambda, "rows"),
                (abr, "stacked"), (abi, "stacked"), (bmat, "stacked"), (cmat, "stacked"),
                (s5_d, "rows"), (w_glu_b, "layer"), (b_glu, "rows"), (w_out_b, "layer")]

    def time_major(x):
        return jnp.swapaxes(x, 0, 1).reshape(x.shape[0] * x.shape[1], D)

    def state_shapes(nb):
        return ((L, halo * nb, DL), (L, nb, DL), (L, nb, GN), (L, nb, GN))

    init_s = (jnp.swapaxes(state_lru_conv, 1, 2).reshape(L, halo * Bs, DL), state_lru_h,
              state_s5_re.reshape(L, Bs, GN), state_s5_im.reshape(L, Bs, GN))
    st_p = tuple(jnp.zeros(s, F32) for s in state_shapes(Bp))
    st_s = tuple(jnp.zeros(s, F32) for s in state_shapes(Bs))
    xp, xs = x_prompt, time_major(x_sample)
    ffn1_b = tuple(w[0].astype(BF16) for w in ffn1_f32)
    for l in range(L):
        xp, xs, *mix_b = _ffn_call(xp, xs, Bp, Bs, mod, l, 0, norm_ffn1, *ffn1_b, gfin,
                                   False, "st" if l == 0 else "tt", [(w, l) for w in mix_f32],
                                   f"ffn1_{l}")
        jobs = [(w, l) for w in ffn2_f32] + [(w, l + 1) for w in ffn1_f32 if l + 1 < L]
        xp, st_p, cast = _mixer_call(xp, mod, l, Bs // Bp, Bp, (), st_p, state_shapes(Bp),
                                     mixer_wts(*mix_b), jobs, f"mixer_p{l}")
        ffn2_b, ffn1_b = cast[:3], cast[3:]
        xs, st_s, _ = _mixer_call(xs, mod, l, 0, Bs, init_s, st_s, state_shapes(Bs),
                                  mixer_wts(*mix_b), [], f"mixer_s{l}")
        xp, xs = _ffn_call(xp, xs, Bp, Bs, mod, l, 2, norm_ffn2, *ffn2_b, gfin,
                           l == L - 1, "ts" if l == L - 1 else "tt", [], f"ffn2_{l}")

    def finish(y, st, nb):
        conv, hh, sre, sim = st
        return (y, jnp.swapaxes(conv.reshape(L, halo, nb, DL), 1, 2), hh,
                sre.reshape(L, nb, G, N), sim.reshape(L, nb, G, N))

    p_out = finish(xp, st_p, Bp)
    s_out = finish(jnp.swapaxes(xs.reshape(Ts, Bs, D), 0, 1), st_s, Bs)
    return (p_out[0], s_out[0]) + p_out[1:] + s_out[1:]
```

```python
import functools

import jax
import jax.numpy as jnp
from jax import lax
from jax.experimental import pallas as pl
from jax.experimental.pallas import tpu as pltpu

EPS = 1e-6
C_GATE = 8.0
FFN_RES = 0.5
CONV_W = 4
N_MOD = 9
S5_A_RE_MAX = -1e-4
BF16 = jnp.bfloat16
F32 = jnp.float32

VMEM_LIMIT_BYTES = 58 * 1024 * 1024
FFN_ROWS = 1024
MIXER_ROWS = 512
SCAN_ROWS = 8
SCAN_LANES = 512
LANES = 128
BF16_ROWS = 16


def _const_spec(shape, index):
    return pl.BlockSpec(shape, lambda i: index, pipeline_mode=pl.Buffered(1))


def _layer_spec(w, layer):
    return _const_spec((None,) + w.shape[1:], (layer,) + (0,) * (w.ndim - 1))


def _params(semantics):
    return pltpu.CompilerParams(dimension_semantics=semantics,
                                vmem_limit_bytes=VMEM_LIMIT_BYTES)


def _cast_steps(jobs, grid):
    n = grid
    while n > 1 and (grid % n or any(w.shape[1] % (BF16_ROWS * n) for w, _ in jobs)):
        n -= 1
    return n


def _cast_specs(jobs, n_steps, step_of):
    ins, outs, shapes = [], [], []
    for w, layer in jobs:
        _, R, C = w.shape
        assert R % (BF16_ROWS * n_steps) == 0
        ins.append(pl.BlockSpec((None, R // n_steps, C),
                                lambda i, layer=layer: (layer, step_of(i), 0)))
        outs.append(pl.BlockSpec((R // n_steps, C), lambda i: (step_of(i), 0)))
        shapes.append(jax.ShapeDtypeStruct((R, C), BF16))
    return ins, outs, shapes


def _cast_blocks(in_refs, out_refs):
    for src_ref, dst_ref in zip(in_refs, out_refs):
        dst_ref[...] = src_ref[...].astype(BF16)


def _rmsnorm(x, g):
    return x * lax.rsqrt(jnp.mean(x * x, axis=-1, keepdims=True) + EPS) * g


def _norm_modulate(x, inner, g, shift, scale):
    xn = x * lax.rsqrt(jnp.mean(x * x, axis=-1, keepdims=True) + EPS)
    xn = xn.reshape(x.shape[0] // inner, inner, x.shape[1])
    return (xn * (g * (1.0 + scale)) + shift).reshape(x.shape)


def _sqrt_nonneg(v):
    return jnp.where(v > 0.0, v * lax.rsqrt(v), v)


def _rows3(v, nb):
    return v.reshape(v.shape[0] // nb, nb, v.shape[1])


def _ada_kernel(c_ref, w_ref, b_ref, o_ref):
    c = c_ref[...]
    s = (c * jax.nn.sigmoid(c)).astype(BF16)
    D = c.shape[-1]
    for k in range(o_ref.shape[0]):
        w = w_ref[:, k * D:(k + 1) * D].astype(BF16)
        o_ref[k] = jnp.dot(s, w, preferred_element_type=F32) + b_ref[k]


def _ada_call(c_all, w_ada, b_ada):
    L, D, _ = w_ada.shape
    nseq = c_all.shape[0]
    n_sub = N_MOD // 3
    return pl.pallas_call(
        _ada_kernel,
        grid=(L, n_sub),
        in_specs=[
            pl.BlockSpec((nseq, D), lambda l, k: (0, 0)),
            pl.BlockSpec((None, D, 3 * D), lambda l, k: (l, 0, k)),
            pl.BlockSpec((None, 3, 1, D), lambda l, k: (l, k, 0, 0)),
        ],
        out_specs=pl.BlockSpec((None, 3, nseq, D), lambda l, k: (l, k, 0, 0)),
        out_shape=jax.ShapeDtypeStruct((L, N_MOD, nseq, D), F32),
        compiler_params=_params(("arbitrary", "arbitrary")),
        name="adaln_mod",
    )(c_all, w_ada, b_ada.reshape(L, N_MOD, 1, D))


def _block_diag_rows(t, nblk):
    r, width = t.shape
    c = width // nblk
    tiled = jnp.concatenate([t] * nblk, axis=0)
    row_blk = lax.broadcasted_iota(jnp.int32, tiled.shape, 0) // r
    col_blk = lax.broadcasted_iota(jnp.int32, tiled.shape, 1) // c
    return jnp.where(row_blk == col_blk, tiled, 0.0)


def _prep_kernel(are_ref, aim_ref, ldt_ref, bt_ref, ct_ref, wt_ref,
                 abr_ref, abi_ref, bmat_ref, cmat_ref, wg_ref):
    ar = jnp.minimum(are_ref[...], S5_A_RE_MAX)
    ai = aim_ref[...]
    dt = jnp.exp(ldt_ref[...])
    mag = jnp.exp(ar * dt)
    abr = mag * jnp.cos(ai * dt)
    abi = mag * jnp.sin(ai * dt)
    abr_ref[...] = abr
    abi_ref[...] = abi
    den = ar * ar + ai * ai
    f_r = ((abr - 1.0) * ar + abi * ai) / den
    f_i = (abi * ar - (abr - 1.0) * ai) / den
    b_r = bt_ref[0]
    b_i = bt_ref[1]
    bb = (f_r * b_r - f_i * b_i, f_r * b_i + f_i * b_r)
    hn = bb[0].shape[-1] // 2
    nblk = bmat_ref.shape[-2] // bb[0].shape[0]
    for part in range(2):
        for hf in range(2):
            cols = slice(hf * hn, (hf + 1) * hn)
            bmat_ref[part, hf] = _block_diag_rows(bb[part][:, cols], nblk).astype(BF16)
            cmat_ref[hf, part] = _block_diag_rows(ct_ref[part][:, cols], nblk).T.astype(BF16)
    heads = wg_ref.shape[0] // wt_ref.shape[1]
    dl = wg_ref.shape[0]
    for gate in range(2):
        wg_ref[:, gate * dl:(gate + 1) * dl] = _block_diag_rows(wt_ref[gate], heads).astype(BF16)


def _prep_call(a_re, a_im, log_dt, b_re, b_im, c_re, c_im, w_rg, w_ig):
    L, G, N = a_re.shape
    J = b_re.shape[-1]
    H, HD, _ = w_rg.shape[1:]
    GN, HN, DL = G * N, G * N // 2, H * HD
    row = lambda v: v.reshape(L, 1, GN)
    ldt = jnp.broadcast_to(log_dt[:, :, None], (L, G, N))
    bt = jnp.stack([b_re, b_im], axis=1).transpose(0, 1, 4, 2, 3).reshape(L, 2, J, GN)
    ct = jnp.stack([c_re, c_im], axis=1).transpose(0, 1, 3, 2, 4).reshape(L, 2, J, GN)
    wt = jnp.stack([w_rg, w_ig], axis=1).transpose(0, 1, 3, 2, 4).reshape(L, 2, HD, DL)
    rspec = pl.BlockSpec((None, 1, GN), lambda l: (l, 0, 0))
    jspec = pl.BlockSpec((None, 2, J, GN), lambda l: (l, 0, 0, 0))
    return pl.pallas_call(
        _prep_kernel,
        grid=(L,),
        in_specs=[rspec, rspec, rspec, jspec, jspec,
                  pl.BlockSpec((None, 2, HD, DL), lambda l: (l, 0, 0, 0))],
        out_specs=[rspec, rspec,
                   pl.BlockSpec((None, 2, 2, G // 2 * J, HN), lambda l: (l, 0, 0, 0, 0)),
                   pl.BlockSpec((None, 2, 2, HN, G // 2 * J), lambda l: (l, 0, 0, 0, 0)),
                   pl.BlockSpec((None, DL, 2 * DL), lambda l: (l, 0, 0))],
        out_shape=[jax.ShapeDtypeStruct((L, 1, GN), F32), jax.ShapeDtypeStruct((L, 1, GN), F32),
                   jax.ShapeDtypeStruct((L, 2, 2, G // 2 * J, HN), BF16),
                   jax.ShapeDtypeStruct((L, 2, 2, HN, G // 2 * J), BF16),
                   jax.ShapeDtypeStruct((L, DL, 2 * DL), BF16)],
        compiler_params=_params(("arbitrary",)),
        name="param_prep",
    )(row(a_re), row(a_im), row(ldt), bt, ct, wt)


def _ffn_rows(x_ref, mod_ref, o_ref, slab_scr, wts, nb, final_norm, layout):
    g_ref, w1_ref, w3_ref, w2_ref, gf_ref = wts
    D = x_ref.shape[-1]
    if layout == "st":
        steps = x_ref.shape[1]
        x = x_ref[...].reshape(nb * steps, D)
        inner, (shift, scale, gate) = steps, [mod_ref[k][:, None, :] for k in range(3)]
    else:
        x = x_ref[...]
        steps = x.shape[0] // nb
        inner, (shift, scale, gate) = nb, [mod_ref[k][None] for k in range(3)]
    rows3 = lambda v: v.reshape(v.shape[0] // inner, inner, D)

    h = _norm_modulate(x, inner, g_ref[...], shift, scale).astype(BF16)
    a = jnp.dot(h, w1_ref[...], preferred_element_type=F32)
    b = jnp.dot(h, w3_ref[...], preferred_element_type=F32)
    act = (a * jax.nn.sigmoid(a) * b).astype(BF16)
    f = jnp.dot(act, w2_ref[...], preferred_element_type=F32)
    out = (rows3(x) + (FFN_RES * gate) * rows3(f)).reshape(x.shape)
    if final_norm:
        out = _rmsnorm(out, gf_ref[...])

    n_slab = D // LANES
    if layout == "tt":
        o_ref[...] = out
    elif layout == "st":
        for seq in range(nb):
            rows_of_seq = out[seq * steps:(seq + 1) * steps]
            for s in range(n_slab):
                slab_scr[s, pl.ds(seq, steps, stride=nb), :] = (
                    rows_of_seq[:, s * LANES:(s + 1) * LANES])
        for s in range(n_slab):
            o_ref[:, s * LANES:(s + 1) * LANES] = slab_scr[s]
    else:
        for s in range(n_slab):
            slab_scr[s] = out[:, s * LANES:(s + 1) * LANES]
        for seq in range(nb):
            for s in range(n_slab):
                o_ref[seq, :, s * LANES:(s + 1) * LANES] = (
                    slab_scr[s, pl.ds(seq, steps, stride=nb), :])


def _ffn_kernel(*refs, layer, n_cast, nbp, nbs, n_prompt, final_norm, prompt_layout):
    (xp_ref, xs_ref, modp_ref, mods_ref, g_ref, w1_ref, w3_ref, w2_ref, gf_ref), refs = (
        refs[:9], refs[9:])
    cast_in, (op_ref, os_ref), cast_out, slab_scr = (
        refs[:n_cast], refs[n_cast:n_cast + 2], refs[n_cast + 2:2 * n_cast + 2], refs[-1])
    wts = (g_ref.at[layer:layer + 1], w1_ref, w3_ref, w2_ref, gf_ref)

    @pl.when(pl.program_id(0) < n_prompt)
    def _():
        _cast_blocks(cast_in, cast_out)
        _ffn_rows(xp_ref, modp_ref, op_ref, slab_scr, wts, nbp, final_norm, prompt_layout)

    @pl.when(pl.program_id(0) == n_prompt)
    def _():
        _ffn_rows(xs_ref, mods_ref, os_ref, slab_scr, wts, nbs, final_norm, "tt")


def _ffn_call(xp, xs, nbp, nbs, mod, layer, mod_group, g, w1, w3, w2, gf, final_norm,
              prompt_layout, cast_jobs, name):
    D = xp.shape[-1]
    rows_p = xp.size // D
    rows_s = xs.shape[0]
    tm = min(FFN_ROWS, rows_p)
    assert rows_p % tm == 0 and tm % nbp == 0 and rows_s % nbs == 0 and nbs % nbp == 0
    assert D % LANES == 0
    n_prompt = rows_p // tm
    tile_tm = pl.BlockSpec((tm, D), lambda i: (jnp.minimum(i, n_prompt - 1), 0))
    tile_sm = pl.BlockSpec((nbp, tm // nbp, D), lambda i: (0, jnp.minimum(i, n_prompt - 1), 0))
    shape_tm = jax.ShapeDtypeStruct((rows_p, D), F32)
    shape_sm = jax.ShapeDtypeStruct((nbp, rows_p // nbp, D), F32)
    n_cast_steps = _cast_steps(cast_jobs, n_prompt)
    cast_in, cast_out, cast_shapes = _cast_specs(
        cast_jobs, n_cast_steps,
        lambda i: jnp.minimum(i, n_prompt - 1) // (n_prompt // n_cast_steps))
    return pl.pallas_call(
        functools.partial(_ffn_kernel, layer=layer, n_cast=len(cast_jobs), nbp=nbp, nbs=nbs,
                          n_prompt=n_prompt, final_norm=final_norm, prompt_layout=prompt_layout),
        grid=(n_prompt + 1,),
        in_specs=[
            tile_sm if prompt_layout == "st" else tile_tm,
            _const_spec((rows_s, D), (0, 0)),
            _const_spec((None, 3, nbp, D), (layer, mod_group, nbs // nbp, 0)),
            _const_spec((None, 3, nbs, D), (layer, mod_group, 0, 0)),
            _const_spec(g.shape, (0, 0)), _const_spec(w1.shape, (0, 0)),
            _const_spec(w3.shape, (0, 0)),
            _const_spec(w2.shape, (0, 0)),
            _const_spec((1, D), (0, 0)),
        ] + cast_in,
        out_specs=[tile_sm if prompt_layout == "ts" else tile_tm,
                   pl.BlockSpec((rows_s, D), lambda i: (0, 0))] + cast_out,
        out_shape=[shape_sm if prompt_layout == "ts" else shape_tm,
                   jax.ShapeDtypeStruct((rows_s, D), F32)] + cast_shapes,
        scratch_shapes=[pltpu.VMEM((D // LANES, tm, LANES), F32)],
        compiler_params=_params(("arbitrary",)),
        name=name,
    )(xp, xs, mod, mod, g, w1, w3, w2, gf, *[w for w, _ in cast_jobs])


def _softplus(v):
    return jnp.maximum(v, 0.0) + jnp.log1p(jnp.exp(-jnp.abs(v)))


def _mixer_kernel(*refs, layer, n_init, n_prev, n_cast, nb):
    (x_ref, mod_ref), refs = refs[:2], refs[2:]
    init_refs, refs = refs[:n_init], refs[n_init + n_prev:]
    (gn_ref, win_ref, convw_ref, convb_ref, wg_ref, brg_ref, big_ref, lam_ref,
     abr_ref, abi_ref, bmat_ref, cmat_ref, d_ref, wglu_ref, bglu_ref, wout_ref), refs = (
        refs[:16], refs[16:])
    cast_in, refs = refs[:n_cast], refs[n_cast:]
    (xo_ref, *state_refs), refs = refs[:5], refs[5:]
    convo_ref, ho_ref, sreo_ref, simo_ref = state_refs
    cast_out, (xp_scr, a_scr, b_scr, s_scr) = refs[:n_cast], refs[n_cast:]
    vec = lambda ref: ref[layer:layer + 1, :]
    rows, D = x_ref.shape
    tt = rows // nb
    DL = ho_ref.shape[-1]
    HS = d_ref.shape[-1] // 2
    HN = sreo_ref.shape[-1] // 2

    @pl.when(pl.program_id(0) == 0)
    def _():
        for k, st_ref in enumerate(state_refs):
            st_ref[...] = init_refs[k][...] if init_refs else jnp.zeros(st_ref.shape, F32)

    _cast_blocks(cast_in, cast_out)
    x = x_ref[...]
    h = _norm_modulate(x, nb, vec(gn_ref), mod_ref[0][None], mod_ref[1][None])
    z = jnp.dot(h.astype(BF16), win_ref[...], preferred_element_type=F32)
    xb = z[:, :DL]
    yb = z[:, DL:2 * DL]
    u = z[:, 2 * DL:]

    halo = (CONV_W - 1) * nb
    xp_scr[0:halo, :] = convo_ref[...]
    xp_scr[halo:halo + rows, :] = xb
    cw = convw_ref[...]
    xc = vec(convb_ref) + sum(xp_scr[k * nb:k * nb + rows, :] * cw[k:k + 1, :]
                              for k in range(CONV_W))
    convo_ref[...] = xp_scr[rows:rows + halo, :]

    gpre = jnp.dot(xc.astype(BF16), wg_ref[...], preferred_element_type=F32)
    r = jax.nn.sigmoid(gpre[:, :DL] + vec(brg_ref))
    ig = jax.nn.sigmoid(gpre[:, DL:] + vec(big_ref))
    log_a = -C_GATE * r * _softplus(-vec(lam_ref))
    a = jnp.exp(log_a)
    mult = _sqrt_nonneg(-jnp.tanh(log_a) * (a * a + 1.0))
    a_scr[...] = a
    b_scr[...] = mult * ig * xc

    for r0 in range(0, nb, SCAN_ROWS):
        hcur = ho_ref[r0:r0 + SCAN_ROWS, :]
        for t in range(tt):
            sl = slice(t * nb + r0, t * nb + r0 + SCAN_ROWS)
            hcur = a_scr[sl, :] * hcur + b_scr[sl, :]
            b_scr[sl, :] = hcur
        ho_ref[r0:r0 + SCAN_ROWS, :] = hcur
    y_lru = jax.nn.gelu(yb) * b_scr[...]

    ub = u.astype(BF16)
    for hf in range(2):
        for part in range(2):
            s_scr[hf, :, part * HN:(part + 1) * HN] = jnp.dot(
                ub[:, hf * HS:(hf + 1) * HS], bmat_ref[part, hf], preferred_element_type=F32)

    for hf in range(2):
        for c0 in range(0, HN, SCAN_LANES):
            st_l = slice(hf * HN + c0, hf * HN + c0 + SCAN_LANES)
            re_l = slice(c0, c0 + SCAN_LANES)
            im_l = slice(HN + c0, HN + c0 + SCAN_LANES)
            a_r = jnp.broadcast_to(abr_ref[:, st_l], (SCAN_ROWS, SCAN_LANES))
            a_i = jnp.broadcast_to(abi_ref[:, st_l], (SCAN_ROWS, SCAN_LANES))
            for r0 in range(0, nb, SCAN_ROWS):
                s_re = sreo_ref[r0:r0 + SCAN_ROWS, st_l]
                s_im = simo_ref[r0:r0 + SCAN_ROWS, st_l]
                for t in range(tt):
                    sl = slice(t * nb + r0, t * nb + r0 + SCAN_ROWS)
                    n_re = a_r * s_re - a_i * s_im + s_scr[hf, sl, re_l]
                    n_im = a_r * s_im + a_i * s_re + s_scr[hf, sl, im_l]
                    s_re, s_im = n_re, n_im
                    s_scr[hf, sl, re_l] = s_re
                    s_scr[hf, sl, im_l] = s_im
                sreo_ref[r0:r0 + SCAN_ROWS, st_l] = s_re
                simo_ref[r0:r0 + SCAN_ROWS, st_l] = s_im

    ys = []
    for hf in range(2):
        sb = s_scr[hf].astype(BF16)
        ys.append(jnp.dot(sb[:, :HN], cmat_ref[hf, 0], preferred_element_type=F32)
                  - jnp.dot(sb[:, HN:], cmat_ref[hf, 1], preferred_element_type=F32))
    ys = jnp.concatenate(ys, axis=-1) + vec(d_ref) * u
    g = jax.nn.gelu(ys)
    y_s5 = g * jax.nn.sigmoid(
        jnp.dot(g.astype(BF16), wglu_ref[...], preferred_element_type=F32) + vec(bglu_ref))

    ycat = jnp.concatenate([y_lru, y_s5], axis=-1).astype(BF16)
    out = jnp.dot(ycat, wout_ref[...], preferred_element_type=F32)
    xo_ref[...] = (_rows3(x, nb) + mod_ref[2][None] * _rows3(out, nb)).reshape(rows, D)


def _mixer_call(x, mod, layer, mod_rowblk, nb, init_states, prev_states, state_shapes, wts,
                cast_jobs, name):
    rows, D = x.shape
    tm = min(MIXER_ROWS, rows)
    assert rows % tm == 0 and tm % nb == 0 and tm // nb >= CONV_W - 1
    DL, GN = state_shapes[1][-1], state_shapes[2][-1]
    halo = (CONV_W - 1) * nb

    def wt_spec(w, kind):
        return _layer_spec(w, layer) if kind == "stacked" else _const_spec(w.shape, (0,) * w.ndim)

    n_lead = 2 + len(init_states)
    state_out_specs = [pl.BlockSpec((None,) + s[1:], lambda i: (layer, 0, 0))
                       for s in state_shapes]
    grid = rows // tm
    n_cast_steps = _cast_steps(cast_jobs, grid)
    cast_in, cast_out, cast_shapes = _cast_specs(cast_jobs, n_cast_steps,
                                                 lambda i: i // (grid // n_cast_steps))
    outs = pl.pallas_call(
        functools.partial(_mixer_kernel, layer=layer, n_init=len(init_states),
                          n_prev=len(prev_states), n_cast=len(cast_jobs), nb=nb),
        grid=(grid,),
        in_specs=[pl.BlockSpec((tm, D), lambda i: (i, 0)),
                  _const_spec((None, 3, nb, D), (layer, 1, mod_rowblk, 0))]
                 + [_layer_spec(s, layer) for s in init_states]
                 + [pl.BlockSpec(memory_space=pl.ANY)] * len(prev_states)
                 + [wt_spec(w, kind) for w, kind in wts] + cast_in,
        out_specs=[pl.BlockSpec((tm, D), lambda i: (i, 0))] + state_out_specs + cast_out,
        out_shape=[jax.ShapeDtypeStruct((rows, D), F32)]
                  + [jax.ShapeDtypeStruct(s, F32) for s in state_shapes] + cast_shapes,
        input_output_aliases={n_lead + k: 1 + k for k in range(len(prev_states))},
        scratch_shapes=[pltpu.VMEM((tm + halo, DL), F32),
                        pltpu.VMEM((tm, DL), F32),
                        pltpu.VMEM((tm, DL), F32),
                        pltpu.VMEM((2, tm, GN), F32)],
        compiler_params=_params(("arbitrary",)),
        name=name,
    )(x, mod, *init_states, *prev_states, *[w for w, _ in wts], *[w for w, _ in cast_jobs])
    n_st = len(state_shapes)
    return outs[0], tuple(outs[1:1 + n_st]), tuple(outs[1 + n_st:])


def kernel(x_prompt, x_sample, c_prompt, c_sample, state_lru_conv, state_lru_h, state_s5_re, state_s5_im, w_ada, b_ada, norm_ffn1, w1_ffn1, w3_ffn1, w2_ffn1, norm_mix, w_in, conv_w, conv_b, w_rg, b_rg, w_ig, b_ig, lru_lambda, s5_a_re, s5_a_im, s5_log_dt, s5_b_re, s5_b_im, s5_c_re, s5_c_im, s5_d, w_glu, b_glu, w_out, norm_ffn2, w1_ffn2, w3_ffn2, w2_ffn2, norm_final):
    L, D, _ = w_ada.shape
    Bp, Tp, _ = x_prompt.shape
    Bs, Ts, _ = x_sample.shape
    _, G, N = s5_a_re.shape
    DL = lru_lambda.shape[-1]
    GN = G * N
    halo = CONV_W - 1
    assert Bs % Bp == 0 and Bp % SCAN_ROWS == 0 and G % 2 == 0

    mod = _ada_call(jnp.concatenate([c_sample, c_prompt], axis=0), w_ada, b_ada)
    abr, abi, bmat, cmat, wgate = _prep_call(s5_a_re, s5_a_im, s5_log_dt, s5_b_re, s5_b_im,
                                             s5_c_re, s5_c_im, w_rg, w_ig)

    gfin = norm_final.reshape(1, D)
    ffn1_f32 = (w1_ffn1, w3_ffn1, w2_ffn1)
    ffn2_f32 = (w1_ffn2, w3_ffn2, w2_ffn2)
    mix_f32 = (w_in, w_glu, w_out)

    def mixer_wts(w_in_b, w_glu_b, w_out_b):
        return [(norm_mix, "rows"), (w_in_b, "layer"), (conv_w, "stacked"), (conv_b, "rows"),
                (wgate, "stacked"), (b_rg, "rows"), (b_ig, "rows"), (lru_lambda, "rows"),
                (abr, "stacked"), (abi, "stacked"), (bmat, "stacked"), (cmat, "stacked"),
                (s5_d, "rows"), (w_glu_b, "layer"), (b_glu, "rows"), (w_out_b, "layer")]

    def time_major(x):
        return jnp.swapaxes(x, 0, 1).reshape(x.shape[0] * x.shape[1], D)

    def state_shapes(nb):
        return ((L, halo * nb, DL), (L, nb, DL), (L, nb, GN), (L, nb, GN))

    init_s = (jnp.swapaxes(state_lru_conv, 1, 2).reshape(L, halo * Bs, DL), state_lru_h,
              state_s5_re.reshape(L, Bs, GN), state_s5_im.reshape(L, Bs, GN))
    st_p = tuple(jnp.zeros(s, F32) for s in state_shapes(Bp))
    st_s = tuple(jnp.zeros(s, F32) for s in state_shapes(Bs))
    xp, xs = x_prompt, time_major(x_sample)
    ffn1_b = tuple(w[0].astype(BF16) for w in ffn1_f32)
    for l in range(L):
        xp, xs, *mix_b = _ffn_call(xp, xs, Bp, Bs, mod, l, 0, norm_ffn1, *ffn1_b, gfin,
                                   False, "st" if l == 0 else "tt", [(w, l) for w in mix_f32],
                                   f"ffn1_{l}")
        jobs = [(w, l) for w in ffn2_f32] + [(w, l + 1) for w in ffn1_f32 if l + 1 < L]
        xp, st_p, cast = _mixer_call(xp, mod, l, Bs // Bp, Bp, (), st_p, state_shapes(Bp),
                                     mixer_wts(*mix_b), jobs, f"mixer_p{l}")
        ffn2_b, ffn1_b = cast[:3], cast[3:]
        xs, st_s, _ = _mixer_call(xs, mod, l, 0, Bs, init_s, st_s, state_shapes(Bs),
                                  mixer_wts(*mix_b), [], f"mixer_s{l}")
        xp, xs = _ffn_call(xp, xs, Bp, Bs, mod, l, 2, norm_ffn2, *ffn2_b, gfin,
                           l == L - 1, "ts" if l == L - 1 else "tt", [], f"ffn2_{l}")

    def finish(y, st, nb):
        conv, hh, sre, sim = st
        return (y, jnp.swapaxes(conv.reshape(L, halo, nb, DL), 1, 2), hh,
                sre.reshape(L, nb, G, N), sim.reshape(L, nb, G, N))

    p_out = finish(xp, st_p, Bp)
    s_out = finish(jnp.swapaxes(xs.reshape(Ts, Bs, D), 0, 1), st_s, Bs)
    return (p_out[0], s_out[0]) + p_out[1:] + s_out[1:]
```

```python
import functools

import jax
import jax.numpy as jnp
from jax import lax
from jax.experimental import pallas as pl
from jax.experimental.pallas import tpu as pltpu

EPS = 1e-6
C_GATE = 8.0
FFN_RES = 0.5
CONV_W = 4
N_MOD = 9
S5_A_RE_MAX = -1e-4
BF16 = jnp.bfloat16
F32 = jnp.float32

VMEM_LIMIT_BYTES = 58 * 1024 * 1024
FFN_ROWS = 1024
MIXER_ROWS = 512
SCAN_ROWS = 8
SCAN_LANES = 512
LANES = 128
BF16_ROWS = 16


def _const_spec(shape, index):
    return pl.BlockSpec(shape, lambda i: index, pipeline_mode=pl.Buffered(1))


def _layer_spec(w, layer):
    return _const_spec((None,) + w.shape[1:], (layer,) + (0,) * (w.ndim - 1))


def _params(semantics):
    return pltpu.CompilerParams(dimension_semantics=semantics,
                                vmem_limit_bytes=VMEM_LIMIT_BYTES)


def _cast_steps(jobs, grid):
    n = grid
    while n > 1 and (grid % n or any(w.shape[1] % (BF16_ROWS * n) for w, _ in jobs)):
        n -= 1
    return n


def _cast_specs(jobs, n_steps, step_of):
    ins, outs, shapes = [], [], []
    for w, layer in jobs:
        _, R, C = w.shape
        assert R % (BF16_ROWS * n_steps) == 0
        ins.append(pl.BlockSpec((None, R // n_steps, C),
                                lambda i, layer=layer: (layer, step_of(i), 0)))
        outs.append(pl.BlockSpec((R // n_steps, C), lambda i: (step_of(i), 0)))
        shapes.append(jax.ShapeDtypeStruct((R, C), BF16))
    return ins, outs, shapes


def _cast_blocks(in_refs, out_refs):
    for src_ref, dst_ref in zip(in_refs, out_refs):
        dst_ref[...] = src_ref[...].astype(BF16)


def _rmsnorm(x, g):
    return x * lax.rsqrt(jnp.mean(x * x, axis=-1, keepdims=True) + EPS) * g


def _norm_modulate(x, inner, g, shift, scale):
    xn = x * lax.rsqrt(jnp.mean(x * x, axis=-1, keepdims=True) + EPS)
    xn = xn.reshape(x.shape[0] // inner, inner, x.shape[1])
    return (xn * (g * (1.0 + scale)) + shift).reshape(x.shape)


def _sqrt_nonneg(v):
    return jnp.where(v > 0.0, v * lax.rsqrt(v), v)


def _rows3(v, nb):
    return v.reshape(v.shape[0] // nb, nb, v.shape[1])


def _ada_kernel(c_ref, w_ref, b_ref, o_ref):
    c = c_ref[...]
    s = (c * jax.nn.sigmoid(c)).astype(BF16)
    D = c.shape[-1]
    for k in range(o_ref.shape[0]):
        w = w_ref[:, k * D:(k + 1) * D].astype(BF16)
        o_ref[k] = jnp.dot(s, w, preferred_element_type=F32) + b_ref[k]


def _ada_call(c_all, w_ada, b_ada):
    L, D, _ = w_ada.shape
    nseq = c_all.shape[0]
    n_sub = N_MOD // 3
    return pl.pallas_call(
        _ada_kernel,
        grid=(L, n_sub),
        in_specs=[
            pl.BlockSpec((nseq, D), lambda l, k: (0, 0)),
            pl.BlockSpec((None, D, 3 * D), lambda l, k: (l, 0, k)),
            pl.BlockSpec((None, 3, 1, D), lambda l, k: (l, k, 0, 0)),
        ],
        out_specs=pl.BlockSpec((None, 3, nseq, D), lambda l, k: (l, k, 0, 0)),
        out_shape=jax.ShapeDtypeStruct((L, N_MOD, nseq, D), F32),
        compiler_params=_params(("arbitrary", "arbitrary")),
        name="adaln_mod",
    )(c_all, w_ada, b_ada.reshape(L, N_MOD, 1, D))


def _block_diag_rows(t, nblk):
    r, width = t.shape
    c = width // nblk
    tiled = jnp.concatenate([t] * nblk, axis=0)
    row_blk = lax.broadcasted_iota(jnp.int32, tiled.shape, 0) // r
    col_blk = lax.broadcasted_iota(jnp.int32, tiled.shape, 1) // c
    return jnp.where(row_blk == col_blk, tiled, 0.0)


def _prep_kernel(are_ref, aim_ref, ldt_ref, bt_ref, ct_ref, wt_ref,
                 abr_ref, abi_ref, bmat_ref, cmat_ref, wg_ref):
    ar = jnp.minimum(are_ref[...], S5_A_RE_MAX)
    ai = aim_ref[...]
    dt = jnp.exp(ldt_ref[...])
    mag = jnp.exp(ar * dt)
    abr = mag * jnp.cos(ai * dt)
    abi = mag * jnp.sin(ai * dt)
    abr_ref[...] = abr
    abi_ref[...] = abi
    den = ar * ar + ai * ai
    f_r = ((abr - 1.0) * ar + abi * ai) / den
    f_i = (abi * ar - (abr - 1.0) * ai) / den
    b_r = bt_ref[0]
    b_i = bt_ref[1]
    bb = (f_r * b_r - f_i * b_i, f_r * b_i + f_i * b_r)
    hn = bb[0].shape[-1] // 2
    nblk = bmat_ref.shape[-2] // bb[0].shape[0]
    for part in range(2):
        for hf in range(2):
            cols = slice(hf * hn, (hf + 1) * hn)
            bmat_ref[part, hf] = _block_diag_rows(bb[part][:, cols], nblk).astype(BF16)
            cmat_ref[hf, part] = _block_diag_rows(ct_ref[part][:, cols], nblk).T.astype(BF16)
    heads = wg_ref.shape[0] // wt_ref.shape[1]
    dl = wg_ref.shape[0]
    for gate in range(2):
        wg_ref[:, gate * dl:(gate + 1) * dl] = _block_diag_rows(wt_ref[gate], heads).astype(BF16)


def _prep_call(a_re, a_im, log_dt, b_re, b_im, c_re, c_im, w_rg, w_ig):
    L, G, N = a_re.shape
    J = b_re.shape[-1]
    H, HD, _ = w_rg.shape[1:]
    GN, HN, DL = G * N, G * N // 2, H * HD
    row = lambda v: v.reshape(L, 1, GN)
    ldt = jnp.broadcast_to(log_dt[:, :, None], (L, G, N))
    bt = jnp.stack([b_re, b_im], axis=1).transpose(0, 1, 4, 2, 3).reshape(L, 2, J, GN)
    ct = jnp.stack([c_re, c_im], axis=1).transpose(0, 1, 3, 2, 4).reshape(L, 2, J, GN)
    wt = jnp.stack([w_rg, w_ig], axis=1).transpose(0, 1, 3, 2, 4).reshape(L, 2, HD, DL)
    rspec = pl.BlockSpec((None, 1, GN), lambda l: (l, 0, 0))
    jspec = pl.BlockSpec((None, 2, J, GN), lambda l: (l, 0, 0, 0))
    return pl.pallas_call(
        _prep_kernel,
        grid=(L,),
        in_specs=[rspec, rspec, rspec, jspec, jspec,
                  pl.BlockSpec((None, 2, HD, DL), lambda l: (l, 0, 0, 0))],
        out_specs=[rspec, rspec,
                   pl.BlockSpec((None, 2, 2, G // 2 * J, HN), lambda l: (l, 0, 0, 0, 0)),
                   pl.BlockSpec((None, 2, 2, HN, G // 2 * J), lambda l: (l, 0, 0, 0, 0)),
                   pl.BlockSpec((None, DL, 2 * DL), lambda l: (l, 0, 0))],
        out_shape=[jax.ShapeDtypeStruct((L, 1, GN), F32), jax.ShapeDtypeStruct((L, 1, GN), F32),
                   jax.ShapeDtypeStruct((L, 2, 2, G // 2 * J, HN), BF16),
                   jax.ShapeDtypeStruct((L, 2, 2, HN, G // 2 * J), BF16),
                   jax.ShapeDtypeStruct((L, DL, 2 * DL), BF16)],
        compiler_params=_params(("arbitrary",)),
        name="param_prep",
    )(row(a_re), row(a_im), row(ldt), bt, ct, wt)


def _flush_slabs(slab_scr, o_ref, nb, layout):
    n_slab, rows, _ = slab_scr.shape
    for s in range(n_slab):
        lanes = slice(s * LANES, (s + 1) * LANES)
        if layout == "st":
            o_ref[:, lanes] = slab_scr[s]
        else:
            for seq in range(nb):
                o_ref[seq, :, lanes] = slab_scr[s, pl.ds(seq, rows // nb, stride=nb), :]


def _ffn_rows(x_ref, mod_ref, o_ref, slab_scr, wts, nb, final_norm, layout):
    g_ref, w1_ref, w3_ref, w2_ref, gf_ref = wts
    D = x_ref.shape[-1]
    if layout == "st":
        steps = x_ref.shape[1]
        x = x_ref[...].reshape(nb * steps, D)
        inner, (shift, scale, gate) = steps, [mod_ref[k][:, None, :] for k in range(3)]
    else:
        x = x_ref[...]
        inner, (shift, scale, gate) = nb, [mod_ref[k][None] for k in range(3)]
    rows3 = lambda v: v.reshape(v.shape[0] // inner, inner, D)

    h = _norm_modulate(x, inner, g_ref[...], shift, scale).astype(BF16)
    a = jnp.dot(h, w1_ref[...], preferred_element_type=F32)
    b = jnp.dot(h, w3_ref[...], preferred_element_type=F32)
    act = (a * jax.nn.sigmoid(a) * b).astype(BF16)
    f = jnp.dot(act, w2_ref[...], preferred_element_type=F32)
    out = (rows3(x) + (FFN_RES * gate) * rows3(f)).reshape(x.shape)
    if final_norm:
        out = _rmsnorm(out, gf_ref[...])

    if layout == "tt":
        o_ref[...] = out
        return
    for s in range(D // LANES):
        lanes = slice(s * LANES, (s + 1) * LANES)
        if layout == "ts":
            slab_scr[s] = out[:, lanes]
        else:
            for seq in range(nb):
                slab_scr[s, pl.ds(seq, steps, stride=nb), :] = (
                    out[seq * steps:(seq + 1) * steps, lanes])


def _ffn_kernel(*refs, layer, n_cast, nbp, nbs, n_prompt, final_norm, prompt_layout):
    (xp_ref, xs_ref, modp_ref, mods_ref, g_ref, w1_ref, w3_ref, w2_ref, gf_ref), refs = (
        refs[:9], refs[9:])
    cast_in, (op_ref, os_ref), cast_out, slab_scr = (
        refs[:n_cast], refs[n_cast:n_cast + 2], refs[n_cast + 2:2 * n_cast + 2], refs[-1])
    wts = (g_ref.at[layer:layer + 1], w1_ref, w3_ref, w2_ref, gf_ref)
    lagged = prompt_layout != "tt"

    if lagged:
        @pl.when(pl.program_id(0) == 0)
        def _():
            slab_scr[...] = jnp.zeros(slab_scr.shape, F32)

    @pl.when(pl.program_id(0) < n_prompt)
    def _():
        if lagged:
            _flush_slabs(slab_scr, op_ref, nbp, prompt_layout)
        _cast_blocks(cast_in, cast_out)
        _ffn_rows(xp_ref, modp_ref, op_ref, slab_scr, wts, nbp, final_norm, prompt_layout)

    @pl.when(pl.program_id(0) == n_prompt)
    def _():
        if lagged:
            _flush_slabs(slab_scr, op_ref, nbp, prompt_layout)
        _ffn_rows(xs_ref, mods_ref, os_ref, slab_scr, wts, nbs, final_norm, "tt")


def _ffn_call(xp, xs, nbp, nbs, mod, layer, mod_group, g, w1, w3, w2, gf, final_norm,
              prompt_layout, cast_jobs, name):
    D = xp.shape[-1]
    rows_p = xp.size // D
    rows_s = xs.shape[0]
    tm = min(FFN_ROWS, rows_p)
    assert rows_p % tm == 0 and tm % nbp == 0 and rows_s % nbs == 0 and nbs % nbp == 0
    assert D % LANES == 0
    n_prompt = rows_p // tm
    cur = lambda i: jnp.minimum(i, n_prompt - 1)
    prev = lambda i: jnp.maximum(i - 1, 0)
    tile_tm = lambda at: pl.BlockSpec((tm, D), lambda i: (at(i), 0))
    tile_sm = lambda at: pl.BlockSpec((nbp, tm // nbp, D), lambda i: (0, at(i), 0))
    shape_tm = jax.ShapeDtypeStruct((rows_p, D), F32)
    shape_sm = jax.ShapeDtypeStruct((nbp, rows_p // nbp, D), F32)
    n_cast_steps = _cast_steps(cast_jobs, n_prompt)
    cast_in, cast_out, cast_shapes = _cast_specs(
        cast_jobs, n_cast_steps,
        lambda i: cur(i) // (n_prompt // n_cast_steps))
    return pl.pallas_call(
        functools.partial(_ffn_kernel, layer=layer, n_cast=len(cast_jobs), nbp=nbp, nbs=nbs,
                          n_prompt=n_prompt, final_norm=final_norm, prompt_layout=prompt_layout),
        grid=(n_prompt + 1,),
        in_specs=[
            tile_sm(cur) if prompt_layout == "st" else tile_tm(cur),
            _const_spec((rows_s, D), (0, 0)),
            _const_spec((None, 3, nbp, D), (layer, mod_group, nbs // nbp, 0)),
            _const_spec((None, 3, nbs, D), (layer, mod_group, 0, 0)),
            _const_spec(g.shape, (0, 0)), _const_spec(w1.shape, (0, 0)),
            _const_spec(w3.shape, (0, 0)),
            _const_spec(w2.shape, (0, 0)),
            _const_spec((1, D), (0, 0)),
        ] + cast_in,
        out_specs=[{"tt": tile_tm(cur), "st": tile_tm(prev), "ts": tile_sm(prev)}[prompt_layout],
                   pl.BlockSpec((rows_s, D), lambda i: (0, 0))] + cast_out,
        out_shape=[shape_sm if prompt_layout == "ts" else shape_tm,
                   jax.ShapeDtypeStruct((rows_s, D), F32)] + cast_shapes,
        scratch_shapes=[pltpu.VMEM((D // LANES, tm, LANES), F32)],
        compiler_params=_params(("arbitrary",)),
        name=name,
    )(xp, xs, mod, mod, g, w1, w3, w2, gf, *[w for w, _ in cast_jobs])


def _softplus(v):
    return jnp.maximum(v, 0.0) + jnp.log1p(jnp.exp(-jnp.abs(v)))


def _mixer_kernel(*refs, layer, n_init, n_prev, n_cast, nb):
    (x_ref, mod_ref), refs = refs[:2], refs[2:]
    init_refs, refs = refs[:n_init], refs[n_init + n_prev:]
    (gn_ref, win_ref, convw_ref, convb_ref, wg_ref, brg_ref, big_ref, lam_ref,
     abr_ref, abi_ref, bmat_ref, cmat_ref, d_ref, wglu_ref, bglu_ref, wout_ref), refs = (
        refs[:16], refs[16:])
    cast_in, refs = refs[:n_cast], refs[n_cast:]
    (xo_ref, *state_refs), refs = refs[:5], refs[5:]
    convo_ref, ho_ref, sreo_ref, simo_ref = state_refs
    cast_out, (xp_scr, a_scr, b_scr, s_scr) = refs[:n_cast], refs[n_cast:]
    vec = lambda ref: ref[layer:layer + 1, :]
    rows, D = x_ref.shape
    tt = rows // nb
    DL = ho_ref.shape[-1]
    HS = d_ref.shape[-1] // 2
    HN = sreo_ref.shape[-1] // 2

    @pl.when(pl.program_id(0) == 0)
    def _():
        for k, st_ref in enumerate(state_refs):
            st_ref[...] = init_refs[k][...] if init_refs else jnp.zeros(st_ref.shape, F32)

    _cast_blocks(cast_in, cast_out)
    x = x_ref[...]
    h = _norm_modulate(x, nb, vec(gn_ref), mod_ref[0][None], mod_ref[1][None])
    z = jnp.dot(h.astype(BF16), win_ref[...], preferred_element_type=F32)
    xb = z[:, :DL]
    yb = z[:, DL:2 * DL]
    u = z[:, 2 * DL:]

    halo = (CONV_W - 1) * nb
    xp_scr[0:halo, :] = convo_ref[...]
    xp_scr[halo:halo + rows, :] = xb
    cw = convw_ref[...]
    xc = vec(convb_ref) + sum(xp_scr[k * nb:k * nb + rows, :] * cw[k:k + 1, :]
                              for k in range(CONV_W))
    convo_ref[...] = xp_scr[rows:rows + halo, :]

    gpre = jnp.dot(xc.astype(BF16), wg_ref[...], preferred_element_type=F32)
    r = jax.nn.sigmoid(gpre[:, :DL] + vec(brg_ref))
    ig = jax.nn.sigmoid(gpre[:, DL:] + vec(big_ref))
    log_a = -C_GATE * r * _softplus(-vec(lam_ref))
    a = jnp.exp(log_a)
    mult = _sqrt_nonneg(-jnp.tanh(log_a) * (a * a + 1.0))
    a_scr[...] = a
    b_scr[...] = mult * ig * xc

    for r0 in range(0, nb, SCAN_ROWS):
        hcur = ho_ref[r0:r0 + SCAN_ROWS, :]
        for t in range(tt):
            sl = slice(t * nb + r0, t * nb + r0 + SCAN_ROWS)
            hcur = a_scr[sl, :] * hcur + b_scr[sl, :]
            b_scr[sl, :] = hcur
        ho_ref[r0:r0 + SCAN_ROWS, :] = hcur
    y_lru = jax.nn.gelu(yb) * b_scr[...]

    ub = u.astype(BF16)
    for hf in range(2):
        for part in range(2):
            s_scr[hf, :, part * HN:(part + 1) * HN] = jnp.dot(
                ub[:, hf * HS:(hf + 1) * HS], bmat_ref[part, hf], preferred_element_type=F32)

    for hf in range(2):
        for c0 in range(0, HN, SCAN_LANES):
            st_l = slice(hf * HN + c0, hf * HN + c0 + SCAN_LANES)
            re_l = slice(c0, c0 + SCAN_LANES)
            im_l = slice(HN + c0, HN + c0 + SCAN_LANES)
            a_r = jnp.broadcast_to(abr_ref[:, st_l], (SCAN_ROWS, SCAN_LANES))
            a_i = jnp.broadcast_to(abi_ref[:, st_l], (SCAN_ROWS, SCAN_LANES))
            for r0 in range(0, nb, SCAN_ROWS):
                s_re = sreo_ref[r0:r0 + SCAN_ROWS, st_l]
                s_im = simo_ref[r0:r0 + SCAN_ROWS, st_l]
                for t in range(tt):
                    sl = slice(t * nb + r0, t * nb + r0 + SCAN_ROWS)
                    n_re = a_r * s_re - a_i * s_im + s_scr[hf, sl, re_l]
                    n_im = a_r * s_im + a_i * s_re + s_scr[hf, sl, im_l]
                    s_re, s_im = n_re, n_im
                    s_scr[hf, sl, re_l] = s_re
                    s_scr[hf, sl, im_l] = s_im
                sreo_ref[r0:r0 + SCAN_ROWS, st_l] = s_re
                simo_ref[r0:r0 + SCAN_ROWS, st_l] = s_im

    ys = []
    for hf in range(2):
        sb = s_scr[hf].astype(BF16)
        ys.append(jnp.dot(sb[:, :HN], cmat_ref[hf, 0], preferred_element_type=F32)
                  - jnp.dot(sb[:, HN:], cmat_ref[hf, 1], preferred_element_type=F32))
    ys = jnp.concatenate(ys, axis=-1) + vec(d_ref) * u
    g = jax.nn.gelu(ys)
    y_s5 = g * jax.nn.sigmoid(
        jnp.dot(g.astype(BF16), wglu_ref[...], preferred_element_type=F32) + vec(bglu_ref))

    ycat = jnp.concatenate([y_lru, y_s5], axis=-1).astype(BF16)
    out = jnp.dot(ycat, wout_ref[...], preferred_element_type=F32)
    xo_ref[...] = (_rows3(x, nb) + mod_ref[2][None] * _rows3(out, nb)).reshape(rows, D)


def _mixer_call(x, mod, layer, mod_rowblk, nb, init_states, prev_states, state_shapes, wts,
                cast_jobs, name):
    rows, D = x.shape
    tm = min(MIXER_ROWS, rows)
    assert rows % tm == 0 and tm % nb == 0 and tm // nb >= CONV_W - 1
    DL, GN = state_shapes[1][-1], state_shapes[2][-1]
    halo = (CONV_W - 1) * nb

    def wt_spec(w, kind):
        return _layer_spec(w, layer) if kind == "stacked" else _const_spec(w.shape, (0,) * w.ndim)

    n_lead = 2 + len(init_states)
    state_out_specs = [pl.BlockSpec((None,) + s[1:], lambda i: (layer, 0, 0))
                       for s in state_shapes]
    grid = rows // tm
    n_cast_steps = _cast_steps(cast_jobs, grid)
    cast_in, cast_out, cast_shapes = _cast_specs(cast_jobs, n_cast_steps,
                                                 lambda i: i // (grid // n_cast_steps))
    outs = pl.pallas_call(
        functools.partial(_mixer_kernel, layer=layer, n_init=len(init_states),
                          n_prev=len(prev_states), n_cast=len(cast_jobs), nb=nb),
        grid=(grid,),
        in_specs=[pl.BlockSpec((tm, D), lambda i: (i, 0)),
                  _const_spec((None, 3, nb, D), (layer, 1, mod_rowblk, 0))]
                 + [_layer_spec(s, layer) for s in init_states]
                 + [pl.BlockSpec(memory_space=pl.ANY)] * len(prev_states)
                 + [wt_spec(w, kind) for w, kind in wts] + cast_in,
        out_specs=[pl.BlockSpec((tm, D), lambda i: (i, 0))] + state_out_specs + cast_out,
        out_shape=[jax.ShapeDtypeStruct((rows, D), F32)]
                  + [jax.ShapeDtypeStruct(s, F32) for s in state_shapes] + cast_shapes,
        input_output_aliases={n_lead + k: 1 + k for k in range(len(prev_states))},
        scratch_shapes=[pltpu.VMEM((tm + halo, DL), F32),
                        pltpu.VMEM((tm, DL), F32),
                        pltpu.VMEM((tm, DL), F32),
                        pltpu.VMEM((2, tm, GN), F32)],
        compiler_params=_params(("arbitrary",)),
        name=name,
    )(x, mod, *init_states, *prev_states, *[w for w, _ in wts], *[w for w, _ in cast_jobs])
    n_st = len(state_shapes)
    return outs[0], tuple(outs[1:1 + n_st]), tuple(outs[1 + n_st:])


def kernel(x_prompt, x_sample, c_prompt, c_sample, state_lru_conv, state_lru_h, state_s5_re, state_s5_im, w_ada, b_ada, norm_ffn1, w1_ffn1, w3_ffn1, w2_ffn1, norm_mix, w_in, conv_w, conv_b, w_rg, b_rg, w_ig, b_ig, lru_lambda, s5_a_re, s5_a_im, s5_log_dt, s5_b_re, s5_b_im, s5_c_re, s5_c_im, s5_d, w_glu, b_glu, w_out, norm_ffn2, w1_ffn2, w3_ffn2, w2_ffn2, norm_final):
    L, D, _ = w_ada.shape
    Bp, Tp, _ = x_prompt.shape
    Bs, Ts, _ = x_sample.shape
    _, G, N = s5_a_re.shape
    DL = lru_lambda.shape[-1]
    GN = G * N
    halo = CONV_W - 1
    assert Bs % Bp == 0 and Bp % SCAN_ROWS == 0 and G % 2 == 0

    mod = _ada_call(jnp.concatenate([c_sample, c_prompt], axis=0), w_ada, b_ada)
    abr, abi, bmat, cmat, wgate = _prep_call(s5_a_re, s5_a_im, s5_log_dt, s5_b_re, s5_b_im,
                                             s5_c_re, s5_c_im, w_rg, w_ig)

    gfin = norm_final.reshape(1, D)
    ffn1_f32 = (w1_ffn1, w3_ffn1, w2_ffn1)
    ffn2_f32 = (w1_ffn2, w3_ffn2, w2_ffn2)
    mix_f32 = (w_in, w_glu, w_out)

    def mixer_wts(w_in_b, w_glu_b, w_out_b):
        return [(norm_mix, "rows"), (w_in_b, "layer"), (conv_w, "stacked"), (conv_b, "rows"),
                (wgate, "stacked"), (b_rg, "rows"), (b_ig, "rows"), (lru_lambda, "rows"),
                (abr, "stacked"), (abi, "stacked"), (bmat, "stacked"), (cmat, "stacked"),
                (s5_d, "rows"), (w_glu_b, "layer"), (b_glu, "rows"), (w_out_b, "layer")]

    def time_major(x):
        return jnp.swapaxes(x, 0, 1).reshape(x.shape[0] * x.shape[1], D)

    def state_shapes(nb):
        return ((L, halo * nb, DL), (L, nb, DL), (L, nb, GN), (L, nb, GN))

    init_s = (jnp.swapaxes(state_lru_conv, 1, 2).reshape(L, halo * Bs, DL), state_lru_h,
              state_s5_re.reshape(L, Bs, GN), state_s5_im.reshape(L, Bs, GN))
    st_p = tuple(jnp.zeros(s, F32) for s in state_shapes(Bp))
    st_s = tuple(jnp.zeros(s, F32) for s in state_shapes(Bs))
    xp, xs = x_prompt, time_major(x_sample)
    ffn1_b = tuple(w[0].astype(BF16) for w in ffn1_f32)
    for l in range(L):
        xp, xs, *mix_b = _ffn_call(xp, xs, Bp, Bs, mod, l, 0, norm_ffn1, *ffn1_b, gfin,
                                   False, "st" if l == 0 else "tt", [(w, l) for w in mix_f32],
                                   f"ffn1_{l}")
        jobs = [(w, l) for w in ffn2_f32] + [(w, l + 1) for w in ffn1_f32 if l + 1 < L]
        xp, st_p, cast = _mixer_call(xp, mod, l, Bs // Bp, Bp, (), st_p, state_shapes(Bp),
                                     mixer_wts(*mix_b), jobs, f"mixer_p{l}")
        ffn2_b, ffn1_b = cast[:3], cast[3:]
        xs, st_s, _ = _mixer_call(xs, mod, l, 0, Bs, init_s, st_s, state_shapes(Bs),
                                  mixer_wts(*mix_b), [], f"mixer_s{l}")
        xp, xs = _ffn_call(xp, xs, Bp, Bs, mod, l, 2, norm_ffn2, *ffn2_b, gfin,
                           l == L - 1, "ts" if l == L - 1 else "tt", [], f"ffn2_{l}")

    def finish(y, st, nb):
        conv, hh, sre, sim = st
        return (y, jnp.swapaxes(conv.reshape(L, halo, nb, DL), 1, 2), hh,
                sre.reshape(L, nb, G, N), sim.reshape(L, nb, G, N))

    p_out = finish(xp, st_p, Bp)
    s_out = finish(jnp.swapaxes(xs.reshape(Ts, Bs, D), 0, 1), st_s, Bs)
    return (p_out[0], s_out[0]) + p_out[1:] + s_out[1:]
```

```python
import functools

import jax
import jax.numpy as jnp
from jax import lax
from jax.experimental import pallas as pl
from jax.experimental.pallas import tpu as pltpu

EPS = 1e-6
C_GATE = 8.0
FFN_RES = 0.5
CONV_W = 4
N_MOD = 9
S5_A_RE_MAX = -1e-4
BF16 = jnp.bfloat16
F32 = jnp.float32

VMEM_LIMIT_BYTES = 58 * 1024 * 1024
FFN_ROWS = 1024
MIXER_ROWS = 512
SCAN_ROWS = 8
SCAN_LANES = 512
LANES = 128
BF16_ROWS = 16


def _const_spec(shape, index):
    return pl.BlockSpec(shape, lambda i: index, pipeline_mode=pl.Buffered(1))


def _layer_spec(w, layer):
    return _const_spec((None,) + w.shape[1:], (layer,) + (0,) * (w.ndim - 1))


def _params(semantics):
    return pltpu.CompilerParams(dimension_semantics=semantics,
                                vmem_limit_bytes=VMEM_LIMIT_BYTES)


def _cast_steps(jobs, grid):
    n = grid
    while n > 1 and (grid % n or any(w.shape[1] % (BF16_ROWS * n) for w, _ in jobs)):
        n -= 1
    return n


def _cast_specs(jobs, n_steps, step_of):
    ins, outs, shapes = [], [], []
    for w, layer in jobs:
        _, R, C = w.shape
        assert R % (BF16_ROWS * n_steps) == 0
        ins.append(pl.BlockSpec((None, R // n_steps, C),
                                lambda i, layer=layer: (layer, step_of(i), 0)))
        outs.append(pl.BlockSpec((R // n_steps, C), lambda i: (step_of(i), 0)))
        shapes.append(jax.ShapeDtypeStruct((R, C), BF16))
    return ins, outs, shapes


def _cast_blocks(in_refs, out_refs):
    for src_ref, dst_ref in zip(in_refs, out_refs):
        dst_ref[...] = src_ref[...].astype(BF16)


def _rmsnorm(x, g):
    return x * lax.rsqrt(jnp.mean(x * x, axis=-1, keepdims=True) + EPS) * g


def _norm_modulate(x, inner, g, shift, scale):
    xn = x * lax.rsqrt(jnp.mean(x * x, axis=-1, keepdims=True) + EPS)
    xn = xn.reshape(x.shape[0] // inner, inner, x.shape[1])
    return (xn * (g * (1.0 + scale)) + shift).reshape(x.shape)


def _sqrt_nonneg(v):
    return jnp.where(v > 0.0, v * lax.rsqrt(v), v)


def _rows3(v, nb):
    return v.reshape(v.shape[0] // nb, nb, v.shape[1])


def _ada_kernel(c_ref, w_ref, b_ref, o_ref):
    c = c_ref[...]
    s = (c * jax.nn.sigmoid(c)).astype(BF16)
    D = c.shape[-1]
    for k in range(o_ref.shape[0]):
        w = w_ref[:, k * D:(k + 1) * D].astype(BF16)
        o_ref[k] = jnp.dot(s, w, preferred_element_type=F32) + b_ref[k]


def _ada_call(c_all, w_ada, b_ada):
    L, D, _ = w_ada.shape
    nseq = c_all.shape[0]
    n_sub = N_MOD // 3
    return pl.pallas_call(
        _ada_kernel,
        grid=(L, n_sub),
        in_specs=[
            pl.BlockSpec((nseq, D), lambda l, k: (0, 0)),
            pl.BlockSpec((None, D, 3 * D), lambda l, k: (l, 0, k)),
            pl.BlockSpec((None, 3, 1, D), lambda l, k: (l, k, 0, 0)),
        ],
        out_specs=pl.BlockSpec((None, 3, nseq, D), lambda l, k: (l, k, 0, 0)),
        out_shape=jax.ShapeDtypeStruct((L, N_MOD, nseq, D), F32),
        compiler_params=_params(("arbitrary", "arbitrary")),
        name="adaln_mod",
    )(c_all, w_ada, b_ada.reshape(L, N_MOD, 1, D))


def _block_diag_rows(t, nblk):
    r, width = t.shape
    c = width // nblk
    tiled = jnp.concatenate([t] * nblk, axis=0)
    row_blk = lax.broadcasted_iota(jnp.int32, tiled.shape, 0) // r
    col_blk = lax.broadcasted_iota(jnp.int32, tiled.shape, 1) // c
    return jnp.where(row_blk == col_blk, tiled, 0.0)


def _prep_kernel(are_ref, aim_ref, ldt_ref, bt_ref, ct_ref, wt_ref,
                 abr_ref, abi_ref, bmat_ref, cmat_ref, wg_ref):
    ar = jnp.minimum(are_ref[...], S5_A_RE_MAX)
    ai = aim_ref[...]
    dt = jnp.exp(ldt_ref[...])
    mag = jnp.exp(ar * dt)
    abr = mag * jnp.cos(ai * dt)
    abi = mag * jnp.sin(ai * dt)
    abr_ref[...] = abr
    abi_ref[...] = abi
    den = ar * ar + ai * ai
    f_r = ((abr - 1.0) * ar + abi * ai) / den
    f_i = (abi * ar - (abr - 1.0) * ai) / den
    b_r = bt_ref[0]
    b_i = bt_ref[1]
    bb = (f_r * b_r - f_i * b_i, f_r * b_i + f_i * b_r)
    hn = bb[0].shape[-1] // 2
    nblk = bmat_ref.shape[-2] // bb[0].shape[0]
    for part in range(2):
        for hf in range(2):
            cols = slice(hf * hn, (hf + 1) * hn)
            bmat_ref[part, hf] = _block_diag_rows(bb[part][:, cols], nblk).astype(BF16)
            cmat_ref[hf, part] = _block_diag_rows(ct_ref[part][:, cols], nblk).T.astype(BF16)
    heads = wg_ref.shape[0] // wt_ref.shape[1]
    dl = wg_ref.shape[0]
    for gate in range(2):
        wg_ref[:, gate * dl:(gate + 1) * dl] = _block_diag_rows(wt_ref[gate], heads).astype(BF16)


def _prep_call(a_re, a_im, log_dt, b_re, b_im, c_re, c_im, w_rg, w_ig):
    L, G, N = a_re.shape
    J = b_re.shape[-1]
    H, HD, _ = w_rg.shape[1:]
    GN, HN, DL = G * N, G * N // 2, H * HD
    row = lambda v: v.reshape(L, 1, GN)
    ldt = jnp.broadcast_to(log_dt[:, :, None], (L, G, N))
    bt = jnp.stack([b_re, b_im], axis=1).transpose(0, 1, 4, 2, 3).reshape(L, 2, J, GN)
    ct = jnp.stack([c_re, c_im], axis=1).transpose(0, 1, 3, 2, 4).reshape(L, 2, J, GN)
    wt = jnp.stack([w_rg, w_ig], axis=1).transpose(0, 1, 3, 2, 4).reshape(L, 2, HD, DL)
    rspec = pl.BlockSpec((None, 1, GN), lambda l: (l, 0, 0))
    jspec = pl.BlockSpec((None, 2, J, GN), lambda l: (l, 0, 0, 0))
    return pl.pallas_call(
        _prep_kernel,
        grid=(L,),
        in_specs=[rspec, rspec, rspec, jspec, jspec,
                  pl.BlockSpec((None, 2, HD, DL), lambda l: (l, 0, 0, 0))],
        out_specs=[rspec, rspec,
                   pl.BlockSpec((None, 2, 2, G // 2 * J, HN), lambda l: (l, 0, 0, 0, 0)),
                   pl.BlockSpec((None, 2, 2, HN, G // 2 * J), lambda l: (l, 0, 0, 0, 0)),
                   pl.BlockSpec((None, DL, 2 * DL), lambda l: (l, 0, 0))],
        out_shape=[jax.ShapeDtypeStruct((L, 1, GN), F32), jax.ShapeDtypeStruct((L, 1, GN), F32),
                   jax.ShapeDtypeStruct((L, 2, 2, G // 2 * J, HN), BF16),
                   jax.ShapeDtypeStruct((L, 2, 2, HN, G // 2 * J), BF16),
                   jax.ShapeDtypeStruct((L, DL, 2 * DL), BF16)],
        compiler_params=_params(("arbitrary",)),
        name="param_prep",
    )(row(a_re), row(a_im), row(ldt), bt, ct, wt)


def _flush_slabs(slab_scr, o_ref, nb, layout):
    n_slab, rows, _ = slab_scr.shape
    for s in range(n_slab):
        lanes = slice(s * LANES, (s + 1) * LANES)
        if layout == "st":
            o_ref[:, lanes] = slab_scr[s]
        else:
            for seq in range(nb):
                o_ref[seq, :, lanes] = slab_scr[s, pl.ds(seq, rows // nb, stride=nb), :]


def _ffn_rows(x_ref, mod_ref, o_ref, wts, nb, final_norm, layout):
    g_ref, w1_ref, w3_ref, w2_ref, gf_ref = wts
    D = x_ref.shape[-1]
    if layout == "st":
        steps = x_ref.shape[1]
        x = x_ref[...].reshape(nb * steps, D)
        inner, (shift, scale, gate) = steps, [mod_ref[k][:, None, :] for k in range(3)]
    else:
        x = x_ref[...]
        inner, (shift, scale, gate) = nb, [mod_ref[k][None] for k in range(3)]
    rows3 = lambda v: v.reshape(v.shape[0] // inner, inner, D)

    h = _norm_modulate(x, inner, g_ref[...], shift, scale).astype(BF16)
    a = jnp.dot(h, w1_ref[...], preferred_element_type=F32)
    b = jnp.dot(h, w3_ref[...], preferred_element_type=F32)
    act = (a * jax.nn.sigmoid(a) * b).astype(BF16)
    f = jnp.dot(act, w2_ref[...], preferred_element_type=F32)
    out = (rows3(x) + (FFN_RES * gate) * rows3(f)).reshape(x.shape)
    if final_norm:
        out = _rmsnorm(out, gf_ref[...])

    if layout == "tt":
        o_ref[...] = out
    return out


def _stash_slabs(out, slab_scr, nb, layout):
    steps = out.shape[0] // nb
    for s in range(out.shape[1] // LANES):
        lanes = slice(s * LANES, (s + 1) * LANES)
        if layout == "ts":
            slab_scr[s] = out[:, lanes]
        else:
            for seq in range(nb):
                slab_scr[s, pl.ds(seq, steps, stride=nb), :] = (
                    out[seq * steps:(seq + 1) * steps, lanes])


def _ffn_kernel(*refs, layer, n_cast, nbp, nbs, n_prompt, final_norm, prompt_layout):
    (xp_ref, xs_ref, modp_ref, mods_ref, g_ref, w1_ref, w3_ref, w2_ref, gf_ref), refs = (
        refs[:9], refs[9:])
    cast_in, (op_ref, os_ref), cast_out, slab_scr = (
        refs[:n_cast], refs[n_cast:n_cast + 2], refs[n_cast + 2:2 * n_cast + 2], refs[-1])
    wts = (g_ref.at[layer:layer + 1], w1_ref, w3_ref, w2_ref, gf_ref)
    lagged = prompt_layout != "tt"

    if lagged:
        @pl.when(pl.program_id(0) == 0)
        def _():
            slab_scr[...] = jnp.zeros(slab_scr.shape, F32)

    @pl.when(pl.program_id(0) < n_prompt)
    def _():
        _cast_blocks(cast_in, cast_out)
        out = _ffn_rows(xp_ref, modp_ref, op_ref, wts, nbp, final_norm, prompt_layout)
        if lagged:
            _flush_slabs(slab_scr, op_ref, nbp, prompt_layout)
            _stash_slabs(out, slab_scr, nbp, prompt_layout)

    @pl.when(pl.program_id(0) == n_prompt)
    def _():
        _ffn_rows(xs_ref, mods_ref, os_ref, wts, nbs, final_norm, "tt")
        if lagged:
            _flush_slabs(slab_scr, op_ref, nbp, prompt_layout)


def _ffn_call(xp, xs, nbp, nbs, mod, layer, mod_group, g, w1, w3, w2, gf, final_norm,
              prompt_layout, cast_jobs, name):
    D = xp.shape[-1]
    rows_p = xp.size // D
    rows_s = xs.shape[0]
    tm = min(FFN_ROWS, rows_p)
    assert rows_p % tm == 0 and tm % nbp == 0 and rows_s % nbs == 0 and nbs % nbp == 0
    assert D % LANES == 0
    n_prompt = rows_p // tm
    cur = lambda i: jnp.minimum(i, n_prompt - 1)
    prev = lambda i: jnp.maximum(i - 1, 0)
    tile_tm = lambda at: pl.BlockSpec((tm, D), lambda i: (at(i), 0))
    tile_sm = lambda at: pl.BlockSpec((nbp, tm // nbp, D), lambda i: (0, at(i), 0))
    shape_tm = jax.ShapeDtypeStruct((rows_p, D), F32)
    shape_sm = jax.ShapeDtypeStruct((nbp, rows_p // nbp, D), F32)
    n_cast_steps = _cast_steps(cast_jobs, n_prompt)
    cast_in, cast_out, cast_shapes = _cast_specs(
        cast_jobs, n_cast_steps,
        lambda i: cur(i) // (n_prompt // n_cast_steps))
    return pl.pallas_call(
        functools.partial(_ffn_kernel, layer=layer, n_cast=len(cast_jobs), nbp=nbp, nbs=nbs,
                          n_prompt=n_prompt, final_norm=final_norm, prompt_layout=prompt_layout),
        grid=(n_prompt + 1,),
        in_specs=[
            tile_sm(cur) if prompt_layout == "st" else tile_tm(cur),
            _const_spec((rows_s, D), (0, 0)),
            _const_spec((None, 3, nbp, D), (layer, mod_group, nbs // nbp, 0)),
            _const_spec((None, 3, nbs, D), (layer, mod_group, 0, 0)),
            _const_spec(g.shape, (0, 0)), _const_spec(w1.shape, (0, 0)),
            _const_spec(w3.shape, (0, 0)),
            _const_spec(w2.shape, (0, 0)),
            _const_spec((1, D), (0, 0)),
        ] + cast_in,
        out_specs=[{"tt": tile_tm(cur), "st": tile_tm(prev), "ts": tile_sm(prev)}[prompt_layout],
                   pl.BlockSpec((rows_s, D), lambda i: (0, 0))] + cast_out,
        out_shape=[shape_sm if prompt_layout == "ts" else shape_tm,
                   jax.ShapeDtypeStruct((rows_s, D), F32)] + cast_shapes,
        scratch_shapes=[pltpu.VMEM((D // LANES, tm, LANES), F32)],
        compiler_params=_params(("arbitrary",)),
        name=name,
    )(xp, xs, mod, mod, g, w1, w3, w2, gf, *[w for w, _ in cast_jobs])


def _softplus(v):
    return jnp.maximum(v, 0.0) + jnp.log1p(jnp.exp(-jnp.abs(v)))


def _mixer_kernel(*refs, layer, n_init, n_prev, n_cast, nb):
    (x_ref, mod_ref), refs = refs[:2], refs[2:]
    init_refs, refs = refs[:n_init], refs[n_init + n_prev:]
    (gn_ref, win_ref, convw_ref, convb_ref, wg_ref, brg_ref, big_ref, lam_ref,
     abr_ref, abi_ref, bmat_ref, cmat_ref, d_ref, wglu_ref, bglu_ref, wout_ref), refs = (
        refs[:16], refs[16:])
    cast_in, refs = refs[:n_cast], refs[n_cast:]
    (xo_ref, *state_refs), refs = refs[:5], refs[5:]
    convo_ref, ho_ref, sreo_ref, simo_ref = state_refs
    cast_out, (xp_scr, a_scr, b_scr, s_scr) = refs[:n_cast], refs[n_cast:]
    vec = lambda ref: ref[layer:layer + 1, :]
    rows, D = x_ref.shape
    tt = rows // nb
    DL = ho_ref.shape[-1]
    HS = d_ref.shape[-1] // 2
    HN = sreo_ref.shape[-1] // 2

    @pl.when(pl.program_id(0) == 0)
    def _():
        for k, st_ref in enumerate(state_refs):
            st_ref[...] = init_refs[k][...] if init_refs else jnp.zeros(st_ref.shape, F32)

    _cast_blocks(cast_in, cast_out)
    x = x_ref[...]
    h = _norm_modulate(x, nb, vec(gn_ref), mod_ref[0][None], mod_ref[1][None])
    z = jnp.dot(h.astype(BF16), win_ref[...], preferred_element_type=F32)
    xb = z[:, :DL]
    yb = z[:, DL:2 * DL]
    u = z[:, 2 * DL:]

    halo = (CONV_W - 1) * nb
    xp_scr[0:halo, :] = convo_ref[...]
    xp_scr[halo:halo + rows, :] = xb
    cw = convw_ref[...]
    xc = vec(convb_ref) + sum(xp_scr[k * nb:k * nb + rows, :] * cw[k:k + 1, :]
                              for k in range(CONV_W))
    convo_ref[...] = xp_scr[rows:rows + halo, :]

    gpre = jnp.dot(xc.astype(BF16), wg_ref[...], preferred_element_type=F32)
    r = jax.nn.sigmoid(gpre[:, :DL] + vec(brg_ref))
    ig = jax.nn.sigmoid(gpre[:, DL:] + vec(big_ref))
    log_a = -C_GATE * r * _softplus(-vec(lam_ref))
    a = jnp.exp(log_a)
    mult = _sqrt_nonneg(-jnp.tanh(log_a) * (a * a + 1.0))
    a_scr[...] = a
    b_scr[...] = mult * ig * xc

    for r0 in range(0, nb, SCAN_ROWS):
        hcur = ho_ref[r0:r0 + SCAN_ROWS, :]
        for t in range(tt):
            sl = slice(t * nb + r0, t * nb + r0 + SCAN_ROWS)
            hcur = a_scr[sl, :] * hcur + b_scr[sl, :]
            b_scr[sl, :] = hcur
        ho_ref[r0:r0 + SCAN_ROWS, :] = hcur
    y_lru = jax.nn.gelu(yb) * b_scr[...]

    ub = u.astype(BF16)
    for hf in range(2):
        for part in range(2):
            s_scr[hf, :, part * HN:(part + 1) * HN] = jnp.dot(
                ub[:, hf * HS:(hf + 1) * HS], bmat_ref[part, hf], preferred_element_type=F32)

    for hf in range(2):
        for c0 in range(0, HN, SCAN_LANES):
            st_l = slice(hf * HN + c0, hf * HN + c0 + SCAN_LANES)
            re_l = slice(c0, c0 + SCAN_LANES)
            im_l = slice(HN + c0, HN + c0 + SCAN_LANES)
            a_r = jnp.broadcast_to(abr_ref[:, st_l], (SCAN_ROWS, SCAN_LANES))
            a_i = jnp.broadcast_to(abi_ref[:, st_l], (SCAN_ROWS, SCAN_LANES))
            for r0 in range(0, nb, SCAN_ROWS):
                s_re = sreo_ref[r0:r0 + SCAN_ROWS, st_l]
                s_im = simo_ref[r0:r0 + SCAN_ROWS, st_l]
                for t in range(tt):
                    sl = slice(t * nb + r0, t * nb + r0 + SCAN_ROWS)
                    n_re = a_r * s_re - a_i * s_im + s_scr[hf, sl, re_l]
                    n_im = a_r * s_im + a_i * s_re + s_scr[hf, sl, im_l]
                    s_re, s_im = n_re, n_im
                    s_scr[hf, sl, re_l] = s_re
                    s_scr[hf, sl, im_l] = s_im
                sreo_ref[r0:r0 + SCAN_ROWS, st_l] = s_re
                simo_ref[r0:r0 + SCAN_ROWS, st_l] = s_im

    ys = []
    for hf in range(2):
        sb = s_scr[hf].astype(BF16)
        ys.append(jnp.dot(sb[:, :HN], cmat_ref[hf, 0], preferred_element_type=F32)
                  - jnp.dot(sb[:, HN:], cmat_ref[hf, 1], preferred_element_type=F32))
    ys = jnp.concatenate(ys, axis=-1) + vec(d_ref) * u
    g = jax.nn.gelu(ys)
    y_s5 = g * jax.nn.sigmoid(
        jnp.dot(g.astype(BF16), wglu_ref[...], preferred_element_type=F32) + vec(bglu_ref))

    ycat = jnp.concatenate([y_lru, y_s5], axis=-1).astype(BF16)
    out = jnp.dot(ycat, wout_ref[...], preferred_element_type=F32)
    xo_ref[...] = (_rows3(x, nb) + mod_ref[2][None] * _rows3(out, nb)).reshape(rows, D)


def _mixer_call(x, mod, layer, mod_rowblk, nb, init_states, prev_states, state_shapes, wts,
                cast_jobs, name):
    rows, D = x.shape
    tm = min(MIXER_ROWS, rows)
    assert rows % tm == 0 and tm % nb == 0 and tm // nb >= CONV_W - 1
    DL, GN = state_shapes[1][-1], state_shapes[2][-1]
    halo = (CONV_W - 1) * nb

    def wt_spec(w, kind):
        return _layer_spec(w, layer) if kind == "stacked" else _const_spec(w.shape, (0,) * w.ndim)

    n_lead = 2 + len(init_states)
    state_out_specs = [pl.BlockSpec((None,) + s[1:], lambda i: (layer, 0, 0))
                       for s in state_shapes]
    grid = rows // tm
    n_cast_steps = _cast_steps(cast_jobs, grid)
    cast_in, cast_out, cast_shapes = _cast_specs(cast_jobs, n_cast_steps,
                                                 lambda i: i // (grid // n_cast_steps))
    outs = pl.pallas_call(
        functools.partial(_mixer_kernel, layer=layer, n_init=len(init_states),
                          n_prev=len(prev_states), n_cast=len(cast_jobs), nb=nb),
        grid=(grid,),
        in_specs=[pl.BlockSpec((tm, D), lambda i: (i, 0)),
                  _const_spec((None, 3, nb, D), (layer, 1, mod_rowblk, 0))]
                 + [_layer_spec(s, layer) for s in init_states]
                 + [pl.BlockSpec(memory_space=pl.ANY)] * len(prev_states)
                 + [wt_spec(w, kind) for w, kind in wts] + cast_in,
        out_specs=[pl.BlockSpec((tm, D), lambda i: (i, 0))] + state_out_specs + cast_out,
        out_shape=[jax.ShapeDtypeStruct((rows, D), F32)]
                  + [jax.ShapeDtypeStruct(s, F32) for s in state_shapes] + cast_shapes,
        input_output_aliases={n_lead + k: 1 + k for k in range(len(prev_states))},
        scratch_shapes=[pltpu.VMEM((tm + halo, DL), F32),
                        pltpu.VMEM((tm, DL), F32),
                        pltpu.VMEM((tm, DL), F32),
                        pltpu.VMEM((2, tm, GN), F32)],
        compiler_params=_params(("arbitrary",)),
        name=name,
    )(x, mod, *init_states, *prev_states, *[w for w, _ in wts], *[w for w, _ in cast_jobs])
    n_st = len(state_shapes)
    return outs[0], tuple(outs[1:1 + n_st]), tuple(outs[1 + n_st:])


def kernel(x_prompt, x_sample, c_prompt, c_sample, state_lru_conv, state_lru_h, state_s5_re, state_s5_im, w_ada, b_ada, norm_ffn1, w1_ffn1, w3_ffn1, w2_ffn1, norm_mix, w_in, conv_w, conv_b, w_rg, b_rg, w_ig, b_ig, lru_lambda, s5_a_re, s5_a_im, s5_log_dt, s5_b_re, s5_b_im, s5_c_re, s5_c_im, s5_d, w_glu, b_glu, w_out, norm_ffn2, w1_ffn2, w3_ffn2, w2_ffn2, norm_final):
    L, D, _ = w_ada.shape
    Bp, Tp, _ = x_prompt.shape
    Bs, Ts, _ = x_sample.shape
    _, G, N = s5_a_re.shape
    DL = lru_lambda.shape[-1]
    GN = G * N
    halo = CONV_W - 1
    assert Bs % Bp == 0 and Bp % SCAN_ROWS == 0 and G % 2 == 0

    mod = _ada_call(jnp.concatenate([c_sample, c_prompt], axis=0), w_ada, b_ada)
    abr, abi, bmat, cmat, wgate = _prep_call(s5_a_re, s5_a_im, s5_log_dt, s5_b_re, s5_b_im,
                                             s5_c_re, s5_c_im, w_rg, w_ig)

    gfin = norm_final.reshape(1, D)
    ffn1_f32 = (w1_ffn1, w3_ffn1, w2_ffn1)
    ffn2_f32 = (w1_ffn2, w3_ffn2, w2_ffn2)
    mix_f32 = (w_in, w_glu, w_out)

    def mixer_wts(w_in_b, w_glu_b, w_out_b):
        return [(norm_mix, "rows"), (w_in_b, "layer"), (conv_w, "stacked"), (conv_b, "rows"),
                (wgate, "stacked"), (b_rg, "rows"), (b_ig, "rows"), (lru_lambda, "rows"),
                (abr, "stacked"), (abi, "stacked"), (bmat, "stacked"), (cmat, "stacked"),
                (s5_d, "rows"), (w_glu_b, "layer"), (b_glu, "rows"), (w_out_b, "layer")]

    def time_major(x):
        return jnp.swapaxes(x, 0, 1).reshape(x.shape[0] * x.shape[1], D)

    def state_shapes(nb):
        return ((L, halo * nb, DL), (L, nb, DL), (L, nb, GN), (L, nb, GN))

    init_s = (jnp.swapaxes(state_lru_conv, 1, 2).reshape(L, halo * Bs, DL), state_lru_h,
              state_s5_re.reshape(L, Bs, GN), state_s5_im.reshape(L, Bs, GN))
    st_p = tuple(jnp.zeros(s, F32) for s in state_shapes(Bp))
    st_s = tuple(jnp.zeros(s, F32) for s in state_shapes(Bs))
    xp, xs = x_prompt, time_major(x_sample)
    ffn1_b = tuple(w[0].astype(BF16) for w in ffn1_f32)
    for l in range(L):
        xp, xs, *mix_b = _ffn_call(xp, xs, Bp, Bs, mod, l, 0, norm_ffn1, *ffn1_b, gfin,
                                   False, "st" if l == 0 else "tt", [(w, l) for w in mix_f32],
                                   f"ffn1_{l}")
        jobs = [(w, l) for w in ffn2_f32] + [(w, l + 1) for w in ffn1_f32 if l + 1 < L]
        xp, st_p, cast = _mixer_call(xp, mod, l, Bs // Bp, Bp, (), st_p, state_shapes(Bp),
                                     mixer_wts(*mix_b), jobs, f"mixer_p{l}")
        ffn2_b, ffn1_b = cast[:3], cast[3:]
        xs, st_s, _ = _mixer_call(xs, mod, l, 0, Bs, init_s, st_s, state_shapes(Bs),
                                  mixer_wts(*mix_b), [], f"mixer_s{l}")
        xp, xs = _ffn_call(xp, xs, Bp, Bs, mod, l, 2, norm_ffn2, *ffn2_b, gfin,
                           l == L - 1, "ts" if l == L - 1 else "tt", [], f"ffn2_{l}")

    def finish(y, st, nb):
        conv, hh, sre, sim = st
        return (y, jnp.swapaxes(conv.reshape(L, halo, nb, DL), 1, 2), hh,
                sre.reshape(L, nb, G, N), sim.reshape(L, nb, G, N))

    p_out = finish(xp, st_p, Bp)
    s_out = finish(jnp.swapaxes(xs.reshape(Ts, Bs, D), 0, 1), st_s, Bs)
    return (p_out[0], s_out[0]) + p_out[1:] + s_out[1:]
```

```python
import functools

import jax
import jax.numpy as jnp
from jax import lax
from jax.experimental import pallas as pl
from jax.experimental.pallas import tpu as pltpu

EPS = 1e-6
C_GATE = 8.0
FFN_RES = 0.5
CONV_W = 4
N_MOD = 9
S5_A_RE_MAX = -1e-4
BF16 = jnp.bfloat16
F32 = jnp.float32

VMEM_LIMIT_BYTES = 58 * 1024 * 1024
FFN_ROWS = 1024
MIXER_ROWS = 512
SCAN_ROWS = 8
SCAN_LANES = 512
LANES = 128
BF16_ROWS = 16


def _const_spec(shape, index):
    return pl.BlockSpec(shape, lambda i: index, pipeline_mode=pl.Buffered(1))


def _layer_spec(w, layer):
    return _const_spec((None,) + w.shape[1:], (layer,) + (0,) * (w.ndim - 1))


def _params(semantics):
    return pltpu.CompilerParams(dimension_semantics=semantics,
                                vmem_limit_bytes=VMEM_LIMIT_BYTES)


def _cast_steps(jobs, grid):
    n = grid
    while n > 1 and (grid % n or any(w.shape[1] % (BF16_ROWS * n) for w, _ in jobs)):
        n -= 1
    return n


def _cast_specs(jobs, n_steps, step_of):
    ins, outs, shapes = [], [], []
    for w, layer in jobs:
        _, R, C = w.shape
        assert R % (BF16_ROWS * n_steps) == 0
        ins.append(pl.BlockSpec((None, R // n_steps, C),
                                lambda i, layer=layer: (layer, step_of(i), 0)))
        outs.append(pl.BlockSpec((R // n_steps, C), lambda i: (step_of(i), 0)))
        shapes.append(jax.ShapeDtypeStruct((R, C), BF16))
    return ins, outs, shapes


def _cast_blocks(in_refs, out_refs):
    for src_ref, dst_ref in zip(in_refs, out_refs):
        dst_ref[...] = src_ref[...].astype(BF16)


def _rmsnorm(x, g):
    return x * lax.rsqrt(jnp.mean(x * x, axis=-1, keepdims=True) + EPS) * g


def _norm_modulate(x, inner, g, shift, scale):
    xn = x * lax.rsqrt(jnp.mean(x * x, axis=-1, keepdims=True) + EPS)
    xn = xn.reshape(x.shape[0] // inner, inner, x.shape[1])
    return (xn * (g * (1.0 + scale)) + shift).reshape(x.shape)


def _sqrt_nonneg(v):
    return jnp.where(v > 0.0, v * lax.rsqrt(v), v)


def _rows3(v, nb):
    return v.reshape(v.shape[0] // nb, nb, v.shape[1])


def _ada_kernel(c_ref, w_ref, b_ref, o_ref):
    c = c_ref[...]
    s = (c * jax.nn.sigmoid(c)).astype(BF16)
    D = c.shape[-1]
    for k in range(o_ref.shape[0]):
        w = w_ref[:, k * D:(k + 1) * D].astype(BF16)
        o_ref[k] = jnp.dot(s, w, preferred_element_type=F32) + b_ref[k]


def _ada_call(c_all, w_ada, b_ada):
    L, D, _ = w_ada.shape
    nseq = c_all.shape[0]
    n_sub = N_MOD // 3
    return pl.pallas_call(
        _ada_kernel,
        grid=(L, n_sub),
        in_specs=[
            pl.BlockSpec((nseq, D), lambda l, k: (0, 0)),
            pl.BlockSpec((None, D, 3 * D), lambda l, k: (l, 0, k)),
            pl.BlockSpec((None, 3, 1, D), lambda l, k: (l, k, 0, 0)),
        ],
        out_specs=pl.BlockSpec((None, 3, nseq, D), lambda l, k: (l, k, 0, 0)),
        out_shape=jax.ShapeDtypeStruct((L, N_MOD, nseq, D), F32),
        compiler_params=_params(("arbitrary", "arbitrary")),
        name="adaln_mod",
    )(c_all, w_ada, b_ada.reshape(L, N_MOD, 1, D))


def _block_diag_rows(t, nblk):
    r, width = t.shape
    c = width // nblk
    tiled = jnp.concatenate([t] * nblk, axis=0)
    row_blk = lax.broadcasted_iota(jnp.int32, tiled.shape, 0) // r
    col_blk = lax.broadcasted_iota(jnp.int32, tiled.shape, 1) // c
    return jnp.where(row_blk == col_blk, tiled, 0.0)


def _prep_kernel(are_ref, aim_ref, ldt_ref, bt_ref, ct_ref, wt_ref,
                 abr_ref, abi_ref, bmat_ref, cmat_ref, wg_ref):
    ar = jnp.minimum(are_ref[...], S5_A_RE_MAX)
    ai = aim_ref[...]
    dt = jnp.exp(ldt_ref[...])
    mag = jnp.exp(ar * dt)
    abr = mag * jnp.cos(ai * dt)
    abi = mag * jnp.sin(ai * dt)
    abr_ref[...] = abr
    abi_ref[...] = abi
    den = ar * ar + ai * ai
    f_r = ((abr - 1.0) * ar + abi * ai) / den
    f_i = (abi * ar - (abr - 1.0) * ai) / den
    b_r = bt_ref[0]
    b_i = bt_ref[1]
    bb = (f_r * b_r - f_i * b_i, f_r * b_i + f_i * b_r)
    hn = bb[0].shape[-1] // 2
    nblk = bmat_ref.shape[-2] // bb[0].shape[0]
    for part in range(2):
        for hf in range(2):
            cols = slice(hf * hn, (hf + 1) * hn)
            bmat_ref[part, hf] = _block_diag_rows(bb[part][:, cols], nblk).astype(BF16)
            cmat_ref[hf, part] = _block_diag_rows(ct_ref[part][:, cols], nblk).T.astype(BF16)
    heads = wg_ref.shape[0] // wt_ref.shape[1]
    dl = wg_ref.shape[0]
    for gate in range(2):
        wg_ref[:, gate * dl:(gate + 1) * dl] = _block_diag_rows(wt_ref[gate], heads).astype(BF16)


def _prep_call(a_re, a_im, log_dt, b_re, b_im, c_re, c_im, w_rg, w_ig):
    L, G, N = a_re.shape
    J = b_re.shape[-1]
    H, HD, _ = w_rg.shape[1:]
    GN, HN, DL = G * N, G * N // 2, H * HD
    row = lambda v: v.reshape(L, 1, GN)
    ldt = jnp.broadcast_to(log_dt[:, :, None], (L, G, N))
    bt = jnp.stack([b_re, b_im], axis=1).transpose(0, 1, 4, 2, 3).reshape(L, 2, J, GN)
    ct = jnp.stack([c_re, c_im], axis=1).transpose(0, 1, 3, 2, 4).reshape(L, 2, J, GN)
    wt = jnp.stack([w_rg, w_ig], axis=1).transpose(0, 1, 3, 2, 4).reshape(L, 2, HD, DL)
    rspec = pl.BlockSpec((None, 1, GN), lambda l: (l, 0, 0))
    jspec = pl.BlockSpec((None, 2, J, GN), lambda l: (l, 0, 0, 0))
    return pl.pallas_call(
        _prep_kernel,
        grid=(L,),
        in_specs=[rspec, rspec, rspec, jspec, jspec,
                  pl.BlockSpec((None, 2, HD, DL), lambda l: (l, 0, 0, 0))],
        out_specs=[rspec, rspec,
                   pl.BlockSpec((None, 2, 2, G // 2 * J, HN), lambda l: (l, 0, 0, 0, 0)),
                   pl.BlockSpec((None, 2, 2, HN, G // 2 * J), lambda l: (l, 0, 0, 0, 0)),
                   pl.BlockSpec((None, DL, 2 * DL), lambda l: (l, 0, 0))],
        out_shape=[jax.ShapeDtypeStruct((L, 1, GN), F32), jax.ShapeDtypeStruct((L, 1, GN), F32),
                   jax.ShapeDtypeStruct((L, 2, 2, G // 2 * J, HN), BF16),
                   jax.ShapeDtypeStruct((L, 2, 2, HN, G // 2 * J), BF16),
                   jax.ShapeDtypeStruct((L, DL, 2 * DL), BF16)],
        compiler_params=_params(("arbitrary",)),
        name="param_prep",
    )(row(a_re), row(a_im), row(ldt), bt, ct, wt)


def _flush_slabs(slab_scr, o_ref, nb, layout):
    n_slab, rows, _ = slab_scr.shape
    steps = rows // nb
    for s in range(n_slab):
        for seq in range(nb):
            if layout == "st":
                o_ref[s, pl.ds(seq, steps, stride=nb), :] = (
                    slab_scr[s, seq * steps:(seq + 1) * steps, :])
            else:
                o_ref[seq, :, s * LANES:(s + 1) * LANES] = (
                    slab_scr[s, pl.ds(seq, steps, stride=nb), :])


def _ffn_rows(x_ref, mod_ref, o_ref, wts, nb, final_norm, layout):
    g_ref, w1_ref, w3_ref, w2_ref, gf_ref = wts
    D = x_ref.shape[-1]
    if layout == "st":
        steps = x_ref.shape[1]
        x = x_ref[...].reshape(nb * steps, D)
        inner, (shift, scale, gate) = steps, [mod_ref[k][:, None, :] for k in range(3)]
    else:
        x = x_ref[...]
        inner, (shift, scale, gate) = nb, [mod_ref[k][None] for k in range(3)]
    rows3 = lambda v: v.reshape(v.shape[0] // inner, inner, D)

    h = _norm_modulate(x, inner, g_ref[...], shift, scale).astype(BF16)
    a = jnp.dot(h, w1_ref[...], preferred_element_type=F32)
    b = jnp.dot(h, w3_ref[...], preferred_element_type=F32)
    act = (a * jax.nn.sigmoid(a) * b).astype(BF16)
    f = jnp.dot(act, w2_ref[...], preferred_element_type=F32)
    out = (rows3(x) + (FFN_RES * gate) * rows3(f)).reshape(x.shape)
    if final_norm:
        out = _rmsnorm(out, gf_ref[...])

    if layout == "tt":
        o_ref[...] = out
    return out


def _stash_slabs(out, slab_scr):
    for s in range(out.shape[1] // LANES):
        slab_scr[s] = out[:, s * LANES:(s + 1) * LANES]


def _ffn_kernel(*refs, layer, n_cast, nbp, nbs, n_prompt, final_norm, prompt_layout):
    (xp_ref, xs_ref, modp_ref, mods_ref, g_ref, w1_ref, w3_ref, w2_ref, gf_ref), refs = (
        refs[:9], refs[9:])
    cast_in, (op_ref, os_ref), cast_out, slab_scr = (
        refs[:n_cast], refs[n_cast:n_cast + 2], refs[n_cast + 2:2 * n_cast + 2], refs[-1])
    wts = (g_ref.at[layer:layer + 1], w1_ref, w3_ref, w2_ref, gf_ref)
    lagged = prompt_layout != "tt"

    if lagged:
        @pl.when(pl.program_id(0) == 0)
        def _():
            slab_scr[...] = jnp.zeros(slab_scr.shape, F32)

    @pl.when(pl.program_id(0) < n_prompt)
    def _():
        _cast_blocks(cast_in, cast_out)
        out = _ffn_rows(xp_ref, modp_ref, op_ref, wts, nbp, final_norm, prompt_layout)
        if lagged:
            _flush_slabs(slab_scr, op_ref, nbp, prompt_layout)
            _stash_slabs(out, slab_scr)

    @pl.when(pl.program_id(0) == n_prompt)
    def _():
        _ffn_rows(xs_ref, mods_ref, os_ref, wts, nbs, final_norm, "tt")
        if lagged:
            _flush_slabs(slab_scr, op_ref, nbp, prompt_layout)


def _ffn_call(xp, xs, nbp, nbs, mod, layer, mod_group, g, w1, w3, w2, gf, final_norm,
              prompt_layout, cast_jobs, name):
    D = xp.shape[-1]
    rows_p = xp.size // D
    rows_s = xs.shape[0]
    tm = min(FFN_ROWS, rows_p)
    assert rows_p % tm == 0 and tm % nbp == 0 and rows_s % nbs == 0 and nbs % nbp == 0
    assert D % LANES == 0
    n_prompt = rows_p // tm
    cur = lambda i: jnp.minimum(i, n_prompt - 1)
    prev = lambda i: jnp.maximum(i - 1, 0)
    tile_tm = lambda at: pl.BlockSpec((tm, D), lambda i: (at(i), 0))
    tile_sm = lambda at: pl.BlockSpec((nbp, tm // nbp, D), lambda i: (0, at(i), 0))
    tile_slab = lambda at: pl.BlockSpec((D // LANES, tm, LANES), lambda i: (0, at(i), 0))
    shape_tm = jax.ShapeDtypeStruct((rows_p, D), F32)
    shape_sm = jax.ShapeDtypeStruct((nbp, rows_p // nbp, D), F32)
    shape_slab = jax.ShapeDtypeStruct((D // LANES, rows_p, LANES), F32)
    n_cast_steps = _cast_steps(cast_jobs, n_prompt)
    cast_in, cast_out, cast_shapes = _cast_specs(
        cast_jobs, n_cast_steps,
        lambda i: cur(i) // (n_prompt // n_cast_steps))
    return pl.pallas_call(
        functools.partial(_ffn_kernel, layer=layer, n_cast=len(cast_jobs), nbp=nbp, nbs=nbs,
                          n_prompt=n_prompt, final_norm=final_norm, prompt_layout=prompt_layout),
        grid=(n_prompt + 1,),
        in_specs=[
            tile_sm(cur) if prompt_layout == "st" else tile_tm(cur),
            _const_spec((rows_s, D), (0, 0)),
            _const_spec((None, 3, nbp, D), (layer, mod_group, nbs // nbp, 0)),
            _const_spec((None, 3, nbs, D), (layer, mod_group, 0, 0)),
            _const_spec(g.shape, (0, 0)), _const_spec(w1.shape, (0, 0)),
            _const_spec(w3.shape, (0, 0)),
            _const_spec(w2.shape, (0, 0)),
            _const_spec((1, D), (0, 0)),
        ] + cast_in,
        out_specs=[{"tt": tile_tm(cur), "st": tile_slab(prev), "ts": tile_sm(prev)}[prompt_layout],
                   pl.BlockSpec((rows_s, D), lambda i: (0, 0))] + cast_out,
        out_shape=[{"tt": shape_tm, "st": shape_slab, "ts": shape_sm}[prompt_layout],
                   jax.ShapeDtypeStruct((rows_s, D), F32)] + cast_shapes,
        scratch_shapes=[pltpu.VMEM((D // LANES, tm, LANES), F32)],
        compiler_params=_params(("arbitrary",)),
        name=name,
    )(xp, xs, mod, mod, g, w1, w3, w2, gf, *[w for w, _ in cast_jobs])


def _softplus(v):
    return jnp.maximum(v, 0.0) + jnp.log1p(jnp.exp(-jnp.abs(v)))


def _mixer_kernel(*refs, layer, n_init, n_prev, n_cast, nb, x_slab):
    (x_ref, mod_ref), refs = refs[:2], refs[2:]
    init_refs, refs = refs[:n_init], refs[n_init + n_prev:]
    (gn_ref, win_ref, convw_ref, convb_ref, wg_ref, brg_ref, big_ref, lam_ref,
     abr_ref, abi_ref, bmat_ref, cmat_ref, d_ref, wglu_ref, bglu_ref, wout_ref), refs = (
        refs[:16], refs[16:])
    cast_in, refs = refs[:n_cast], refs[n_cast:]
    (xo_ref, *state_refs), refs = refs[:5], refs[5:]
    convo_ref, ho_ref, sreo_ref, simo_ref = state_refs
    cast_out, (xp_scr, a_scr, b_scr, s_scr) = refs[:n_cast], refs[n_cast:]
    vec = lambda ref: ref[layer:layer + 1, :]
    rows = a_scr.shape[0]
    tt = rows // nb
    DL = ho_ref.shape[-1]
    HS = d_ref.shape[-1] // 2
    HN = sreo_ref.shape[-1] // 2

    @pl.when(pl.program_id(0) == 0)
    def _():
        for k, st_ref in enumerate(state_refs):
            st_ref[...] = init_refs[k][...] if init_refs else jnp.zeros(st_ref.shape, F32)

    _cast_blocks(cast_in, cast_out)
    x = (jnp.concatenate([x_ref[s] for s in range(x_ref.shape[0])], axis=-1) if x_slab
         else x_ref[...])
    D = x.shape[-1]
    h = _norm_modulate(x, nb, vec(gn_ref), mod_ref[0][None], mod_ref[1][None])
    z = jnp.dot(h.astype(BF16), win_ref[...], preferred_element_type=F32)
    xb = z[:, :DL]
    yb = z[:, DL:2 * DL]
    u = z[:, 2 * DL:]

    halo = (CONV_W - 1) * nb
    xp_scr[0:halo, :] = convo_ref[...]
    xp_scr[halo:halo + rows, :] = xb
    cw = convw_ref[...]
    xc = vec(convb_ref) + sum(xp_scr[k * nb:k * nb + rows, :] * cw[k:k + 1, :]
                              for k in range(CONV_W))
    convo_ref[...] = xp_scr[rows:rows + halo, :]

    gpre = jnp.dot(xc.astype(BF16), wg_ref[...], preferred_element_type=F32)
    r = jax.nn.sigmoid(gpre[:, :DL] + vec(brg_ref))
    ig = jax.nn.sigmoid(gpre[:, DL:] + vec(big_ref))
    log_a = -C_GATE * r * _softplus(-vec(lam_ref))
    a = jnp.exp(log_a)
    mult = _sqrt_nonneg(-jnp.tanh(log_a) * (a * a + 1.0))
    a_scr[...] = a
    b_scr[...] = mult * ig * xc

    for r0 in range(0, nb, SCAN_ROWS):
        hcur = ho_ref[r0:r0 + SCAN_ROWS, :]
        for t in range(tt):
            sl = slice(t * nb + r0, t * nb + r0 + SCAN_ROWS)
            hcur = a_scr[sl, :] * hcur + b_scr[sl, :]
            b_scr[sl, :] = hcur
        ho_ref[r0:r0 + SCAN_ROWS, :] = hcur
    y_lru = jax.nn.gelu(yb) * b_scr[...]

    ub = u.astype(BF16)
    for hf in range(2):
        for part in range(2):
            s_scr[hf, :, part * HN:(part + 1) * HN] = jnp.dot(
                ub[:, hf * HS:(hf + 1) * HS], bmat_ref[part, hf], preferred_element_type=F32)

    for hf in range(2):
        for c0 in range(0, HN, SCAN_LANES):
            st_l = slice(hf * HN + c0, hf * HN + c0 + SCAN_LANES)
            re_l = slice(c0, c0 + SCAN_LANES)
            im_l = slice(HN + c0, HN + c0 + SCAN_LANES)
            a_r = jnp.broadcast_to(abr_ref[:, st_l], (SCAN_ROWS, SCAN_LANES))
            a_i = jnp.broadcast_to(abi_ref[:, st_l], (SCAN_ROWS, SCAN_LANES))
            for r0 in range(0, nb, SCAN_ROWS):
                s_re = sreo_ref[r0:r0 + SCAN_ROWS, st_l]
                s_im = simo_ref[r0:r0 + SCAN_ROWS, st_l]
                for t in range(tt):
                    sl = slice(t * nb + r0, t * nb + r0 + SCAN_ROWS)
                    n_re = a_r * s_re - a_i * s_im + s_scr[hf, sl, re_l]
                    n_im = a_r * s_im + a_i * s_re + s_scr[hf, sl, im_l]
                    s_re, s_im = n_re, n_im
                    s_scr[hf, sl, re_l] = s_re
                    s_scr[hf, sl, im_l] = s_im
                sreo_ref[r0:r0 + SCAN_ROWS, st_l] = s_re
                simo_ref[r0:r0 + SCAN_ROWS, st_l] = s_im

    ys = []
    for hf in range(2):
        sb = s_scr[hf].astype(BF16)
        ys.append(jnp.dot(sb[:, :HN], cmat_ref[hf, 0], preferred_element_type=F32)
                  - jnp.dot(sb[:, HN:], cmat_ref[hf, 1], preferred_element_type=F32))
    ys = jnp.concatenate(ys, axis=-1) + vec(d_ref) * u
    g = jax.nn.gelu(ys)
    y_s5 = g * jax.nn.sigmoid(
        jnp.dot(g.astype(BF16), wglu_ref[...], preferred_element_type=F32) + vec(bglu_ref))

    ycat = jnp.concatenate([y_lru, y_s5], axis=-1).astype(BF16)
    out = jnp.dot(ycat, wout_ref[...], preferred_element_type=F32)
    xo_ref[...] = (_rows3(x, nb) + mod_ref[2][None] * _rows3(out, nb)).reshape(rows, D)


def _mixer_call(x, mod, layer, mod_rowblk, nb, init_states, prev_states, state_shapes, wts,
                cast_jobs, name):
    x_slab = x.ndim == 3
    rows, D = (x.shape[1], x.shape[0] * x.shape[2]) if x_slab else x.shape
    tm = min(MIXER_ROWS, rows)
    assert rows % tm == 0 and tm % nb == 0 and tm // nb >= CONV_W - 1
    DL, GN = state_shapes[1][-1], state_shapes[2][-1]
    halo = (CONV_W - 1) * nb

    def wt_spec(w, kind):
        return _layer_spec(w, layer) if kind == "stacked" else _const_spec(w.shape, (0,) * w.ndim)

    n_lead = 2 + len(init_states)
    state_out_specs = [pl.BlockSpec((None,) + s[1:], lambda i: (layer, 0, 0))
                       for s in state_shapes]
    grid = rows // tm
    n_cast_steps = _cast_steps(cast_jobs, grid)
    cast_in, cast_out, cast_shapes = _cast_specs(cast_jobs, n_cast_steps,
                                                 lambda i: i // (grid // n_cast_steps))
    outs = pl.pallas_call(
        functools.partial(_mixer_kernel, layer=layer, n_init=len(init_states),
                          n_prev=len(prev_states), n_cast=len(cast_jobs), nb=nb, x_slab=x_slab),
        grid=(grid,),
        in_specs=[pl.BlockSpec((D // LANES, tm, LANES), lambda i: (0, i, 0)) if x_slab
                  else pl.BlockSpec((tm, D), lambda i: (i, 0)),
                  _const_spec((None, 3, nb, D), (layer, 1, mod_rowblk, 0))]
                 + [_layer_spec(s, layer) for s in init_states]
                 + [pl.BlockSpec(memory_space=pl.ANY)] * len(prev_states)
                 + [wt_spec(w, kind) for w, kind in wts] + cast_in,
        out_specs=[pl.BlockSpec((tm, D), lambda i: (i, 0))] + state_out_specs + cast_out,
        out_shape=[jax.ShapeDtypeStruct((rows, D), F32)]
                  + [jax.ShapeDtypeStruct(s, F32) for s in state_shapes] + cast_shapes,
        input_output_aliases={n_lead + k: 1 + k for k in range(len(prev_states))},
        scratch_shapes=[pltpu.VMEM((tm + halo, DL), F32),
                        pltpu.VMEM((tm, DL), F32),
                        pltpu.VMEM((tm, DL), F32),
                        pltpu.VMEM((2, tm, GN), F32)],
        compiler_params=_params(("arbitrary",)),
        name=name,
    )(x, mod, *init_states, *prev_states, *[w for w, _ in wts], *[w for w, _ in cast_jobs])
    n_st = len(state_shapes)
    return outs[0], tuple(outs[1:1 + n_st]), tuple(outs[1 + n_st:])


def kernel(x_prompt, x_sample, c_prompt, c_sample, state_lru_conv, state_lru_h, state_s5_re, state_s5_im, w_ada, b_ada, norm_ffn1, w1_ffn1, w3_ffn1, w2_ffn1, norm_mix, w_in, conv_w, conv_b, w_rg, b_rg, w_ig, b_ig, lru_lambda, s5_a_re, s5_a_im, s5_log_dt, s5_b_re, s5_b_im, s5_c_re, s5_c_im, s5_d, w_glu, b_glu, w_out, norm_ffn2, w1_ffn2, w3_ffn2, w2_ffn2, norm_final):
    L, D, _ = w_ada.shape
    Bp, Tp, _ = x_prompt.shape
    Bs, Ts, _ = x_sample.shape
    _, G, N = s5_a_re.shape
    DL = lru_lambda.shape[-1]
    GN = G * N
    halo = CONV_W - 1
    assert Bs % Bp == 0 and Bp % SCAN_ROWS == 0 and G % 2 == 0

    mod = _ada_call(jnp.concatenate([c_sample, c_prompt], axis=0), w_ada, b_ada)
    abr, abi, bmat, cmat, wgate = _prep_call(s5_a_re, s5_a_im, s5_log_dt, s5_b_re, s5_b_im,
                                             s5_c_re, s5_c_im, w_rg, w_ig)

    gfin = norm_final.reshape(1, D)
    ffn1_f32 = (w1_ffn1, w3_ffn1, w2_ffn1)
    ffn2_f32 = (w1_ffn2, w3_ffn2, w2_ffn2)
    mix_f32 = (w_in, w_glu, w_out)

    def mixer_wts(w_in_b, w_glu_b, w_out_b):
        return [(norm_mix, "rows"), (w_in_b, "layer"), (conv_w, "stacked"), (conv_b, "rows"),
                (wgate, "stacked"), (b_rg, "rows"), (b_ig, "rows"), (lru_lambda, "rows"),
                (abr, "stacked"), (abi, "stacked"), (bmat, "stacked"), (cmat, "stacked"),
                (s5_d, "rows"), (w_glu_b, "layer"), (b_glu, "rows"), (w_out_b, "layer")]

    def time_major(x):
        return jnp.swapaxes(x, 0, 1).reshape(x.shape[0] * x.shape[1], D)

    def state_shapes(nb):
        return ((L, halo * nb, DL), (L, nb, DL), (L, nb, GN), (L, nb, GN))

    init_s = (jnp.swapaxes(state_lru_conv, 1, 2).reshape(L, halo * Bs, DL), state_lru_h,
              state_s5_re.reshape(L, Bs, GN), state_s5_im.reshape(L, Bs, GN))
    st_p = tuple(jnp.zeros(s, F32) for s in state_shapes(Bp))
    st_s = tuple(jnp.zeros(s, F32) for s in state_shapes(Bs))
    xp, xs = x_prompt, time_major(x_sample)
    ffn1_b = tuple(w[0].astype(BF16) for w in ffn1_f32)
    for l in range(L):
        xp, xs, *mix_b = _ffn_call(xp, xs, Bp, Bs, mod, l, 0, norm_ffn1, *ffn1_b, gfin,
                                   False, "st" if l == 0 else "tt", [(w, l) for w in mix_f32],
                                   f"ffn1_{l}")
        jobs = [(w, l) for w in ffn2_f32] + [(w, l + 1) for w in ffn1_f32 if l + 1 < L]
        xp, st_p, cast = _mixer_call(xp, mod, l, Bs // Bp, Bp, (), st_p, state_shapes(Bp),
                                     mixer_wts(*mix_b), jobs, f"mixer_p{l}")
        ffn2_b, ffn1_b = cast[:3], cast[3:]
        xs, st_s, _ = _mixer_call(xs, mod, l, 0, Bs, init_s, st_s, state_shapes(Bs),
                                  mixer_wts(*mix_b), [], f"mixer_s{l}")
        xp, xs = _ffn_call(xp, xs, Bp, Bs, mod, l, 2, norm_ffn2, *ffn2_b, gfin,
                           l == L - 1, "ts" if l == L - 1 else "tt", [], f"ffn2_{l}")

    def finish(y, st, nb):
        conv, hh, sre, sim = st
        return (y, jnp.swapaxes(conv.reshape(L, halo, nb, DL), 1, 2), hh,
                sre.reshape(L, nb, G, N), sim.reshape(L, nb, G, N))

    p_out = finish(xp, st_p, Bp)
    s_out = finish(jnp.swapaxes(xs.reshape(Ts, Bs, D), 0, 1), st_s, Bs)
    return (p_out[0], s_out[0]) + p_out[1:] + s_out[1:]
```

```python
import functools

import jax
import jax.numpy as jnp
from jax import lax
from jax.experimental import pallas as pl
from jax.experimental.pallas import tpu as pltpu

EPS = 1e-6
C_GATE = 8.0
FFN_RES = 0.5
CONV_W = 4
N_MOD = 9
S5_A_RE_MAX = -1e-4
BF16 = jnp.bfloat16
F32 = jnp.float32

VMEM_LIMIT_BYTES = 58 * 1024 * 1024
FFN_ROWS = 1024
MIXER_ROWS = 512
SCAN_ROWS = 8
SCAN_LANES = 512
LANES = 128
BF16_ROWS = 16


def _const_spec(shape, index):
    return pl.BlockSpec(shape, lambda i: index, pipeline_mode=pl.Buffered(1))


def _layer_spec(w, layer):
    return _const_spec((None,) + w.shape[1:], (layer,) + (0,) * (w.ndim - 1))


def _params(semantics):
    return pltpu.CompilerParams(dimension_semantics=semantics,
                                vmem_limit_bytes=VMEM_LIMIT_BYTES)


def _cast_steps(jobs, grid):
    n = grid
    while n > 1 and (grid % n or any(w.shape[1] % (BF16_ROWS * n) for w, _ in jobs)):
        n -= 1
    return n


def _cast_specs(jobs, n_steps, step_of):
    ins, outs, shapes = [], [], []
    for w, layer in jobs:
        _, R, C = w.shape
        assert R % (BF16_ROWS * n_steps) == 0
        ins.append(pl.BlockSpec((None, R // n_steps, C),
                                lambda i, layer=layer: (layer, step_of(i), 0)))
        outs.append(pl.BlockSpec((R // n_steps, C), lambda i: (step_of(i), 0)))
        shapes.append(jax.ShapeDtypeStruct((R, C), BF16))
    return ins, outs, shapes


def _cast_blocks(in_refs, out_refs):
    for src_ref, dst_ref in zip(in_refs, out_refs):
        dst_ref[...] = src_ref[...].astype(BF16)


def _rmsnorm(x, g):
    return x * lax.rsqrt(jnp.mean(x * x, axis=-1, keepdims=True) + EPS) * g


def _norm_modulate(x, inner, g, shift, scale):
    xn = x * lax.rsqrt(jnp.mean(x * x, axis=-1, keepdims=True) + EPS)
    xn = xn.reshape(x.shape[0] // inner, inner, x.shape[1])
    return (xn * (g * (1.0 + scale)) + shift).reshape(x.shape)


def _sqrt_nonneg(v):
    return jnp.where(v > 0.0, v * lax.rsqrt(v), v)


def _rows3(v, nb):
    return v.reshape(v.shape[0] // nb, nb, v.shape[1])


def _ada_kernel(c_ref, w_ref, b_ref, o_ref):
    c = c_ref[...]
    s = (c * jax.nn.sigmoid(c)).astype(BF16)
    D = c.shape[-1]
    for k in range(o_ref.shape[0]):
        w = w_ref[:, k * D:(k + 1) * D].astype(BF16)
        o_ref[k] = jnp.dot(s, w, preferred_element_type=F32) + b_ref[k]


def _ada_call(c_all, w_ada, b_ada):
    L, D, _ = w_ada.shape
    nseq = c_all.shape[0]
    n_sub = N_MOD // 3
    return pl.pallas_call(
        _ada_kernel,
        grid=(L, n_sub),
        in_specs=[
            pl.BlockSpec((nseq, D), lambda l, k: (0, 0)),
            pl.BlockSpec((None, D, 3 * D), lambda l, k: (l, 0, k)),
            pl.BlockSpec((None, 3, 1, D), lambda l, k: (l, k, 0, 0)),
        ],
        out_specs=pl.BlockSpec((None, 3, nseq, D), lambda l, k: (l, k, 0, 0)),
        out_shape=jax.ShapeDtypeStruct((L, N_MOD, nseq, D), F32),
        compiler_params=_params(("arbitrary", "arbitrary")),
        name="adaln_mod",
    )(c_all, w_ada, b_ada.reshape(L, N_MOD, 1, D))


def _block_diag_rows(t, nblk):
    r, width = t.shape
    c = width // nblk
    tiled = jnp.concatenate([t] * nblk, axis=0)
    row_blk = lax.broadcasted_iota(jnp.int32, tiled.shape, 0) // r
    col_blk = lax.broadcasted_iota(jnp.int32, tiled.shape, 1) // c
    return jnp.where(row_blk == col_blk, tiled, 0.0)


def _prep_kernel(are_ref, aim_ref, ldt_ref, bt_ref, ct_ref, wt_ref,
                 abr_ref, abi_ref, bmat_ref, cmat_ref, wg_ref):
    ar = jnp.minimum(are_ref[...], S5_A_RE_MAX)
    ai = aim_ref[...]
    dt = jnp.exp(ldt_ref[...])
    mag = jnp.exp(ar * dt)
    abr = mag * jnp.cos(ai * dt)
    abi = mag * jnp.sin(ai * dt)
    abr_ref[...] = abr
    abi_ref[...] = abi
    den = ar * ar + ai * ai
    f_r = ((abr - 1.0) * ar + abi * ai) / den
    f_i = (abi * ar - (abr - 1.0) * ai) / den
    b_r = bt_ref[0]
    b_i = bt_ref[1]
    bb = (f_r * b_r - f_i * b_i, f_r * b_i + f_i * b_r)
    hn = bb[0].shape[-1] // 2
    nblk = bmat_ref.shape[-2] // bb[0].shape[0]
    for part in range(2):
        for hf in range(2):
            cols = slice(hf * hn, (hf + 1) * hn)
            bmat_ref[part, hf] = _block_diag_rows(bb[part][:, cols], nblk).astype(BF16)
            cmat_ref[hf, part] = _block_diag_rows(ct_ref[part][:, cols], nblk).T.astype(BF16)
    heads = wg_ref.shape[0] // wt_ref.shape[1]
    dl = wg_ref.shape[0]
    for gate in range(2):
        wg_ref[:, gate * dl:(gate + 1) * dl] = _block_diag_rows(wt_ref[gate], heads).astype(BF16)


def _prep_call(a_re, a_im, log_dt, b_re, b_im, c_re, c_im, w_rg, w_ig):
    L, G, N = a_re.shape
    J = b_re.shape[-1]
    H, HD, _ = w_rg.shape[1:]
    GN, HN, DL = G * N, G * N // 2, H * HD
    row = lambda v: v.reshape(L, 1, GN)
    ldt = jnp.broadcast_to(log_dt[:, :, None], (L, G, N))
    bt = jnp.stack([b_re, b_im], axis=1).transpose(0, 1, 4, 2, 3).reshape(L, 2, J, GN)
    ct = jnp.stack([c_re, c_im], axis=1).transpose(0, 1, 3, 2, 4).reshape(L, 2, J, GN)
    wt = jnp.stack([w_rg, w_ig], axis=1).transpose(0, 1, 3, 2, 4).reshape(L, 2, HD, DL)
    rspec = pl.BlockSpec((None, 1, GN), lambda l: (l, 0, 0))
    jspec = pl.BlockSpec((None, 2, J, GN), lambda l: (l, 0, 0, 0))
    return pl.pallas_call(
        _prep_kernel,
        grid=(L,),
        in_specs=[rspec, rspec, rspec, jspec, jspec,
                  pl.BlockSpec((None, 2, HD, DL), lambda l: (l, 0, 0, 0))],
        out_specs=[rspec, rspec,
                   pl.BlockSpec((None, 2, 2, G // 2 * J, HN), lambda l: (l, 0, 0, 0, 0)),
                   pl.BlockSpec((None, 2, 2, HN, G // 2 * J), lambda l: (l, 0, 0, 0, 0)),
                   pl.BlockSpec((None, DL, 2 * DL), lambda l: (l, 0, 0))],
        out_shape=[jax.ShapeDtypeStruct((L, 1, GN), F32), jax.ShapeDtypeStruct((L, 1, GN), F32),
                   jax.ShapeDtypeStruct((L, 2, 2, G // 2 * J, HN), BF16),
                   jax.ShapeDtypeStruct((L, 2, 2, HN, G // 2 * J), BF16),
                   jax.ShapeDtypeStruct((L, DL, 2 * DL), BF16)],
        compiler_params=_params(("arbitrary",)),
        name="param_prep",
    )(row(a_re), row(a_im), row(ldt), bt, ct, wt)


def _flush_slabs(slab_scr, o_ref, nb, layout, gf_ref):
    n_slab, rows, _ = slab_scr.shape
    if layout == "st":
        for s in range(n_slab):
            o_ref[:, s * LANES:(s + 1) * LANES] = slab_scr[s]
        return
    for seq in range(nb):
        y = jnp.concatenate([slab_scr[s, pl.ds(seq, rows // nb, stride=nb), :]
                             for s in range(n_slab)], axis=-1)
        o_ref[seq] = y if gf_ref is None else _rmsnorm(y, gf_ref[...])


def _ffn_rows(x_ref, mod_ref, o_ref, slab_scr, wts, nb, final_norm, layout):
    g_ref, w1_ref, w3_ref, w2_ref, gf_ref = wts
    D = x_ref.shape[-1]
    if layout == "st":
        steps = x_ref.shape[1]
        x = x_ref[...].reshape(nb * steps, D)
        inner, (shift, scale, gate) = steps, [mod_ref[k][:, None, :] for k in range(3)]
    else:
        x = x_ref[...]
        inner, (shift, scale, gate) = nb, [mod_ref[k][None] for k in range(3)]
    rows3 = lambda v: v.reshape(v.shape[0] // inner, inner, D)

    h = _norm_modulate(x, inner, g_ref[...], shift, scale).astype(BF16)
    a = jnp.dot(h, w1_ref[...], preferred_element_type=F32)
    b = jnp.dot(h, w3_ref[...], preferred_element_type=F32)
    act = (a * jax.nn.sigmoid(a) * b).astype(BF16)
    f = jnp.dot(act, w2_ref[...], preferred_element_type=F32)
    out = (rows3(x) + (FFN_RES * gate) * rows3(f)).reshape(x.shape)
    if final_norm:
        out = _rmsnorm(out, gf_ref[...])

    if layout == "tt":
        o_ref[...] = out
        return
    for s in range(D // LANES):
        lanes = slice(s * LANES, (s + 1) * LANES)
        if layout == "ts":
            slab_scr[s] = out[:, lanes]
        else:
            for seq in range(nb):
                slab_scr[s, pl.ds(seq, steps, stride=nb), :] = (
                    out[seq * steps:(seq + 1) * steps, lanes])


def _ffn_kernel(*refs, layer, n_cast, nbp, nbs, n_prompt, final_norm, prompt_layout):
    (xp_ref, xs_ref, modp_ref, mods_ref, g_ref, w1_ref, w3_ref, w2_ref, gf_ref), refs = (
        refs[:9], refs[9:])
    cast_in, (op_ref, os_ref), cast_out, slab_scr = (
        refs[:n_cast], refs[n_cast:n_cast + 2], refs[n_cast + 2:2 * n_cast + 2], refs[-1])
    wts = (g_ref.at[layer:layer + 1], w1_ref, w3_ref, w2_ref, gf_ref)
    lagged = prompt_layout != "tt"
    flush_gain = gf_ref if final_norm and prompt_layout == "ts" else None

    if lagged:
        @pl.when(pl.program_id(0) == 0)
        def _():
            slab_scr[...] = jnp.zeros(slab_scr.shape, F32)

    @pl.when(pl.program_id(0) < n_prompt)
    def _():
        if lagged:
            _flush_slabs(slab_scr, op_ref, nbp, prompt_layout, flush_gain)
        _cast_blocks(cast_in, cast_out)
        _ffn_rows(xp_ref, modp_ref, op_ref, slab_scr, wts, nbp,
                  final_norm and flush_gain is None, prompt_layout)

    @pl.when(pl.program_id(0) == n_prompt)
    def _():
        if lagged:
            _flush_slabs(slab_scr, op_ref, nbp, prompt_layout, flush_gain)
        _ffn_rows(xs_ref, mods_ref, os_ref, slab_scr, wts, nbs, final_norm, "tt")


def _ffn_call(xp, xs, nbp, nbs, mod, layer, mod_group, g, w1, w3, w2, gf, final_norm,
              prompt_layout, cast_jobs, name):
    D = xp.shape[-1]
    rows_p = xp.size // D
    rows_s = xs.shape[0]
    tm = min(FFN_ROWS, rows_p)
    assert rows_p % tm == 0 and tm % nbp == 0 and rows_s % nbs == 0 and nbs % nbp == 0
    assert D % LANES == 0
    n_prompt = rows_p // tm
    cur = lambda i: jnp.minimum(i, n_prompt - 1)
    prev = lambda i: jnp.maximum(i - 1, 0)
    tile_tm = lambda at: pl.BlockSpec((tm, D), lambda i: (at(i), 0))
    tile_sm = lambda at: pl.BlockSpec((nbp, tm // nbp, D), lambda i: (0, at(i), 0))
    shape_tm = jax.ShapeDtypeStruct((rows_p, D), F32)
    shape_sm = jax.ShapeDtypeStruct((nbp, rows_p // nbp, D), F32)
    n_cast_steps = _cast_steps(cast_jobs, n_prompt)
    cast_in, cast_out, cast_shapes = _cast_specs(
        cast_jobs, n_cast_steps,
        lambda i: cur(i) // (n_prompt // n_cast_steps))
    return pl.pallas_call(
        functools.partial(_ffn_kernel, layer=layer, n_cast=len(cast_jobs), nbp=nbp, nbs=nbs,
                          n_prompt=n_prompt, final_norm=final_norm, prompt_layout=prompt_layout),
        grid=(n_prompt + 1,),
        in_specs=[
            tile_sm(cur) if prompt_layout == "st" else tile_tm(cur),
            _const_spec((rows_s, D), (0, 0)),
            _const_spec((None, 3, nbp, D), (layer, mod_group, nbs // nbp, 0)),
            _const_spec((None, 3, nbs, D), (layer, mod_group, 0, 0)),
            _const_spec(g.shape, (0, 0)), _const_spec(w1.shape, (0, 0)),
            _const_spec(w3.shape, (0, 0)),
            _const_spec(w2.shape, (0, 0)),
            _const_spec((1, D), (0, 0)),
        ] + cast_in,
        out_specs=[{"tt": tile_tm(cur), "st": tile_tm(prev), "ts": tile_sm(prev)}[prompt_layout],
                   pl.BlockSpec((rows_s, D), lambda i: (0, 0))] + cast_out,
        out_shape=[shape_sm if prompt_layout == "ts" else shape_tm,
                   jax.ShapeDtypeStruct((rows_s, D), F32)] + cast_shapes,
        scratch_shapes=[pltpu.VMEM((D // LANES, tm, LANES), F32)],
        compiler_params=_params(("arbitrary",)),
        name=name,
    )(xp, xs, mod, mod, g, w1, w3, w2, gf, *[w for w, _ in cast_jobs])


def _softplus(v):
    return jnp.maximum(v, 0.0) + jnp.log1p(jnp.exp(-jnp.abs(v)))


def _mixer_kernel(*refs, layer, n_init, n_prev, n_cast, nb):
    (x_ref, mod_ref), refs = refs[:2], refs[2:]
    init_refs, refs = refs[:n_init], refs[n_init + n_prev:]
    (gn_ref, win_ref, convw_ref, convb_ref, wg_ref, brg_ref, big_ref, lam_ref,
     abr_ref, abi_ref, bmat_ref, cmat_ref, d_ref, wglu_ref, bglu_ref, wout_ref), refs = (
        refs[:16], refs[16:])
    cast_in, refs = refs[:n_cast], refs[n_cast:]
    (xo_ref, *state_refs), refs = refs[:5], refs[5:]
    convo_ref, ho_ref, sreo_ref, simo_ref = state_refs
    cast_out, (xp_scr, a_scr, b_scr, s_scr) = refs[:n_cast], refs[n_cast:]
    vec = lambda ref: ref[layer:layer + 1, :]
    rows, D = x_ref.shape
    tt = rows // nb
    DL = ho_ref.shape[-1]
    HS = d_ref.shape[-1] // 2
    HN = sreo_ref.shape[-1] // 2

    @pl.when(pl.program_id(0) == 0)
    def _():
        for k, st_ref in enumerate(state_refs):
            st_ref[...] = init_refs[k][...] if init_refs else jnp.zeros(st_ref.shape, F32)

    _cast_blocks(cast_in, cast_out)
    x = x_ref[...]
    h = _norm_modulate(x, nb, vec(gn_ref), mod_ref[0][None], mod_ref[1][None])
    z = jnp.dot(h.astype(BF16), win_ref[...], preferred_element_type=F32)
    xb = z[:, :DL]
    yb = z[:, DL:2 * DL]
    u = z[:, 2 * DL:]

    halo = (CONV_W - 1) * nb
    xp_scr[0:halo, :] = convo_ref[...]
    xp_scr[halo:halo + rows, :] = xb
    cw = convw_ref[...]
    xc = vec(convb_ref) + sum(xp_scr[k * nb:k * nb + rows, :] * cw[k:k + 1, :]
                              for k in range(CONV_W))
    convo_ref[...] = xp_scr[rows:rows + halo, :]

    gpre = jnp.dot(xc.astype(BF16), wg_ref[...], preferred_element_type=F32)
    r = jax.nn.sigmoid(gpre[:, :DL] + vec(brg_ref))
    ig = jax.nn.sigmoid(gpre[:, DL:] + vec(big_ref))
    log_a = -C_GATE * r * _softplus(-vec(lam_ref))
    a = jnp.exp(log_a)
    mult = _sqrt_nonneg(-jnp.tanh(log_a) * (a * a + 1.0))
    a_scr[...] = a
    b_scr[...] = mult * ig * xc

    for r0 in range(0, nb, SCAN_ROWS):
        hcur = ho_ref[r0:r0 + SCAN_ROWS, :]
        for t in range(tt):
            sl = slice(t * nb + r0, t * nb + r0 + SCAN_ROWS)
            hcur = a_scr[sl, :] * hcur + b_scr[sl, :]
            b_scr[sl, :] = hcur
        ho_ref[r0:r0 + SCAN_ROWS, :] = hcur
    y_lru = jax.nn.gelu(yb) * b_scr[...]

    ub = u.astype(BF16)
    for hf in range(2):
        for part in range(2):
            s_scr[hf, :, part * HN:(part + 1) * HN] = jnp.dot(
                ub[:, hf * HS:(hf + 1) * HS], bmat_ref[part, hf], preferred_element_type=F32)

    for hf in range(2):
        for c0 in range(0, HN, SCAN_LANES):
            st_l = slice(hf * HN + c0, hf * HN + c0 + SCAN_LANES)
            re_l = slice(c0, c0 + SCAN_LANES)
            im_l = slice(HN + c0, HN + c0 + SCAN_LANES)
            a_r = jnp.broadcast_to(abr_ref[:, st_l], (SCAN_ROWS, SCAN_LANES))
            a_i = jnp.broadcast_to(abi_ref[:, st_l], (SCAN_ROWS, SCAN_LANES))
            for r0 in range(0, nb, SCAN_ROWS):
                s_re = sreo_ref[r0:r0 + SCAN_ROWS, st_l]
                s_im = simo_ref[r0:r0 + SCAN_ROWS, st_l]
                for t in range(tt):
                    sl = slice(t * nb + r0, t * nb + r0 + SCAN_ROWS)
                    n_re = a_r * s_re - a_i * s_im + s_scr[hf, sl, re_l]
                    n_im = a_r * s_im + a_i * s_re + s_scr[hf, sl, im_l]
                    s_re, s_im = n_re, n_im
                    s_scr[hf, sl, re_l] = s_re
                    s_scr[hf, sl, im_l] = s_im
                sreo_ref[r0:r0 + SCAN_ROWS, st_l] = s_re
                simo_ref[r0:r0 + SCAN_ROWS, st_l] = s_im

    ys = []
    for hf in range(2):
        sb = s_scr[hf].astype(BF16)
        ys.append(jnp.dot(sb[:, :HN], cmat_ref[hf, 0], preferred_element_type=F32)
                  - jnp.dot(sb[:, HN:], cmat_ref[hf, 1], preferred_element_type=F32))
    ys = jnp.concatenate(ys, axis=-1) + vec(d_ref) * u
    g = jax.nn.gelu(ys)
    y_s5 = g * jax.nn.sigmoid(
        jnp.dot(g.astype(BF16), wglu_ref[...], preferred_element_type=F32) + vec(bglu_ref))

    ycat = jnp.concatenate([y_lru, y_s5], axis=-1).astype(BF16)
    out = jnp.dot(ycat, wout_ref[...], preferred_element_type=F32)
    xo_ref[...] = (_rows3(x, nb) + mod_ref[2][None] * _rows3(out, nb)).reshape(rows, D)


def _mixer_call(x, mod, layer, mod_rowblk, nb, init_states, prev_states, state_shapes, wts,
                cast_jobs, name):
    rows, D = x.shape
    tm = min(MIXER_ROWS, rows)
    assert rows % tm == 0 and tm % nb == 0 and tm // nb >= CONV_W - 1
    DL, GN = state_shapes[1][-1], state_shapes[2][-1]
    halo = (CONV_W - 1) * nb

    def wt_spec(w, kind):
        return _layer_spec(w, layer) if kind == "stacked" else _const_spec(w.shape, (0,) * w.ndim)

    n_lead = 2 + len(init_states)
    state_out_specs = [pl.BlockSpec((None,) + s[1:], lambda i: (layer, 0, 0))
                       for s in state_shapes]
    grid = rows // tm
    n_cast_steps = _cast_steps(cast_jobs, grid)
    cast_in, cast_out, cast_shapes = _cast_specs(cast_jobs, n_cast_steps,
                                                 lambda i: i // (grid // n_cast_steps))
    outs = pl.pallas_call(
        functools.partial(_mixer_kernel, layer=layer, n_init=len(init_states),
                          n_prev=len(prev_states), n_cast=len(cast_jobs), nb=nb),
        grid=(grid,),
        in_specs=[pl.BlockSpec((tm, D), lambda i: (i, 0)),
                  _const_spec((None, 3, nb, D), (layer, 1, mod_rowblk, 0))]
                 + [_layer_spec(s, layer) for s in init_states]
                 + [pl.BlockSpec(memory_space=pl.ANY)] * len(prev_states)
                 + [wt_spec(w, kind) for w, kind in wts] + cast_in,
        out_specs=[pl.BlockSpec((tm, D), lambda i: (i, 0))] + state_out_specs + cast_out,
        out_shape=[jax.ShapeDtypeStruct((rows, D), F32)]
                  + [jax.ShapeDtypeStruct(s, F32) for s in state_shapes] + cast_shapes,
        input_output_aliases={n_lead + k: 1 + k for k in range(len(prev_states))},
        scratch_shapes=[pltpu.VMEM((tm + halo, DL), F32),
                        pltpu.VMEM((tm, DL), F32),
                        pltpu.VMEM((tm, DL), F32),
                        pltpu.VMEM((2, tm, GN), F32)],
        compiler_params=_params(("arbitrary",)),
        name=name,
    )(x, mod, *init_states, *prev_states, *[w for w, _ in wts], *[w for w, _ in cast_jobs])
    n_st = len(state_shapes)
    return outs[0], tuple(outs[1:1 + n_st]), tuple(outs[1 + n_st:])


def kernel(x_prompt, x_sample, c_prompt, c_sample, state_lru_conv, state_lru_h, state_s5_re, state_s5_im, w_ada, b_ada, norm_ffn1, w1_ffn1, w3_ffn1, w2_ffn1, norm_mix, w_in, conv_w, conv_b, w_rg, b_rg, w_ig, b_ig, lru_lambda, s5_a_re, s5_a_im, s5_log_dt, s5_b_re, s5_b_im, s5_c_re, s5_c_im, s5_d, w_glu, b_glu, w_out, norm_ffn2, w1_ffn2, w3_ffn2, w2_ffn2, norm_final):
    L, D, _ = w_ada.shape
    Bp, Tp, _ = x_prompt.shape
    Bs, Ts, _ = x_sample.shape
    _, G, N = s5_a_re.shape
    DL = lru_lambda.shape[-1]
    GN = G * N
    halo = CONV_W - 1
    assert Bs % Bp == 0 and Bp % SCAN_ROWS == 0 and G % 2 == 0

    mod = _ada_call(jnp.concatenate([c_sample, c_prompt], axis=0), w_ada, b_ada)
    abr, abi, bmat, cmat, wgate = _prep_call(s5_a_re, s5_a_im, s5_log_dt, s5_b_re, s5_b_im,
                                             s5_c_re, s5_c_im, w_rg, w_ig)

    gfin = norm_final.reshape(1, D)
    ffn1_f32 = (w1_ffn1, w3_ffn1, w2_ffn1)
    ffn2_f32 = (w1_ffn2, w3_ffn2, w2_ffn2)
    mix_f32 = (w_in, w_glu, w_out)

    def mixer_wts(w_in_b, w_glu_b, w_out_b):
        return [(norm_mix, "rows"), (w_in_b, "layer"), (conv_w, "stacked"), (conv_b, "rows"),
                (wgate, "stacked"), (b_rg, "rows"), (b_ig, "rows"), (lru_lambda, "rows"),
                (abr, "stacked"), (abi, "stacked"), (bmat, "stacked"), (cmat, "stacked"),
                (s5_d, "rows"), (w_glu_b, "layer"), (b_glu, "rows"), (w_out_b, "layer")]

    def time_major(x):
        return jnp.swapaxes(x, 0, 1).reshape(x.shape[0] * x.shape[1], D)

    def state_shapes(nb):
        return ((L, halo * nb, DL), (L, nb, DL), (L, nb, GN), (L, nb, GN))

    init_s = (jnp.swapaxes(state_lru_conv, 1, 2).reshape(L, halo * Bs, DL), state_lru_h,
              state_s5_re.reshape(L, Bs, GN), state_s5_im.reshape(L, Bs, GN))
    st_p = tuple(jnp.zeros(s, F32) for s in state_shapes(Bp))
    st_s = tuple(jnp.zeros(s, F32) for s in state_shapes(Bs))
    xp, xs = x_prompt, time_major(x_sample)
    ffn1_b = tuple(w[0].astype(BF16) for w in ffn1_f32)
    for l in range(L):
        xp, xs, *mix_b = _ffn_call(xp, xs, Bp, Bs, mod, l, 0, norm_ffn1, *ffn1_b, gfin,
                                   False, "st" if l == 0 else "tt", [(w, l) for w in mix_f32],
                                   f"ffn1_{l}")
        jobs = [(w, l) for w in ffn2_f32] + [(w, l + 1) for w in ffn1_f32 if l + 1 < L]
        xp, st_p, cast = _mixer_call(xp, mod, l, Bs // Bp, Bp, (), st_p, state_shapes(Bp),
                                     mixer_wts(*mix_b), jobs, f"mixer_p{l}")
        ffn2_b, ffn1_b = cast[:3], cast[3:]
        xs, st_s, _ = _mixer_call(xs, mod, l, 0, Bs, init_s, st_s, state_shapes(Bs),
                                  mixer_wts(*mix_b), [], f"mixer_s{l}")
        xp, xs = _ffn_call(xp, xs, Bp, Bs, mod, l, 2, norm_ffn2, *ffn2_b, gfin,
                           l == L - 1, "ts" if l == L - 1 else "tt", [], f"ffn2_{l}")

    def finish(y, st, nb):
        conv, hh, sre, sim = st
        return (y, jnp.swapaxes(conv.reshape(L, halo, nb, DL), 1, 2), hh,
                sre.reshape(L, nb, G, N), sim.reshape(L, nb, G, N))

    p_out = finish(xp, st_p, Bp)
    s_out = finish(jnp.swapaxes(xs.reshape(Ts, Bs, D), 0, 1), st_s, Bs)
    return (p_out[0], s_out[0]) + p_out[1:] + s_out[1:]
```

```python
import functools

import jax
import jax.numpy as jnp
from jax import lax
from jax.experimental import pallas as pl
from jax.experimental.pallas import tpu as pltpu

EPS = 1e-6
C_GATE = 8.0
FFN_RES = 0.5
CONV_W = 4
N_MOD = 9
S5_A_RE_MAX = -1e-4
BF16 = jnp.bfloat16
F32 = jnp.float32

VMEM_LIMIT_BYTES = 58 * 1024 * 1024
FFN_ROWS = 1024
MIXER_ROWS = 512
SCAN_ROWS = 8
SCAN_LANES = 512
LANES = 128
BF16_ROWS = 16


def _const_spec(shape, index):
    return pl.BlockSpec(shape, lambda i: index, pipeline_mode=pl.Buffered(1))


def _layer_spec(w, layer):
    return _const_spec((None,) + w.shape[1:], (layer,) + (0,) * (w.ndim - 1))


def _params(semantics):
    return pltpu.CompilerParams(dimension_semantics=semantics,
                                vmem_limit_bytes=VMEM_LIMIT_BYTES)


def _cast_steps(jobs, grid):
    n = grid
    while n > 1 and (grid % n or any(w.shape[1] % (BF16_ROWS * n) for w, _ in jobs)):
        n -= 1
    return n


def _cast_specs(jobs, n_steps, step_of):
    ins, outs, shapes = [], [], []
    for w, layer in jobs:
        _, R, C = w.shape
        assert R % (BF16_ROWS * n_steps) == 0
        ins.append(pl.BlockSpec((None, R // n_steps, C),
                                lambda i, layer=layer: (layer, step_of(i), 0)))
        outs.append(pl.BlockSpec((R // n_steps, C), lambda i: (step_of(i), 0)))
        shapes.append(jax.ShapeDtypeStruct((R, C), BF16))
    return ins, outs, shapes


def _cast_blocks(in_refs, out_refs):
    for src_ref, dst_ref in zip(in_refs, out_refs):
        dst_ref[...] = src_ref[...].astype(BF16)


def _rmsnorm(x, g):
    return x * lax.rsqrt(jnp.mean(x * x, axis=-1, keepdims=True) + EPS) * g


def _norm_modulate(x, inner, g, shift, scale):
    xn = x * lax.rsqrt(jnp.mean(x * x, axis=-1, keepdims=True) + EPS)
    xn = xn.reshape(x.shape[0] // inner, inner, x.shape[1])
    return (xn * (g * (1.0 + scale)) + shift).reshape(x.shape)


def _sqrt_nonneg(v):
    return jnp.where(v > 0.0, v * lax.rsqrt(v), v)


def _rows3(v, nb):
    return v.reshape(v.shape[0] // nb, nb, v.shape[1])


def _ada_kernel(c_ref, w_ref, b_ref, *refs):
    prep_in, o_ref, prep_out = refs[:6], refs[6], refs[7:]

    @pl.when(pl.program_id(1) == 0)
    def _():
        _prep_kernel(*prep_in, *prep_out)

    c = c_ref[...]
    s = (c * jax.nn.sigmoid(c)).astype(BF16)
    D = c.shape[-1]
    for k in range(o_ref.shape[0]):
        w = w_ref[:, k * D:(k + 1) * D].astype(BF16)
        o_ref[k] = jnp.dot(s, w, preferred_element_type=F32) + b_ref[k]


def _ada_prep_call(c_all, w_ada, b_ada, prep):
    L, D, _ = w_ada.shape
    nseq = c_all.shape[0]
    n_sub = N_MOD // 3
    prep_ops, prep_in, prep_out, prep_shapes = prep
    return pl.pallas_call(
        _ada_kernel,
        grid=(L, n_sub),
        in_specs=[
            pl.BlockSpec((nseq, D), lambda l, k: (0, 0)),
            pl.BlockSpec((None, D, 3 * D), lambda l, k: (l, 0, k)),
            pl.BlockSpec((None, 3, 1, D), lambda l, k: (l, k, 0, 0)),
        ] + prep_in,
        out_specs=[pl.BlockSpec((None, 3, nseq, D), lambda l, k: (l, k, 0, 0))] + prep_out,
        out_shape=[jax.ShapeDtypeStruct((L, N_MOD, nseq, D), F32)] + prep_shapes,
        compiler_params=_params(("arbitrary", "arbitrary")),
        name="adaln_mod_prep",
    )(c_all, w_ada, b_ada.reshape(L, N_MOD, 1, D), *prep_ops)


def _block_diag_rows(t, nblk):
    r, width = t.shape
    c = width // nblk
    tiled = jnp.concatenate([t] * nblk, axis=0)
    row_blk = lax.broadcasted_iota(jnp.int32, tiled.shape, 0) // r
    col_blk = lax.broadcasted_iota(jnp.int32, tiled.shape, 1) // c
    return jnp.where(row_blk == col_blk, tiled, 0.0)


def _prep_kernel(are_ref, aim_ref, ldt_ref, bt_ref, ct_ref, wt_ref,
                 abr_ref, abi_ref, bmat_ref, cmat_ref, wg_ref):
    ar = jnp.minimum(are_ref[...], S5_A_RE_MAX)
    ai = aim_ref[...]
    dt = jnp.exp(ldt_ref[...])
    mag = jnp.exp(ar * dt)
    abr = mag * jnp.cos(ai * dt)
    abi = mag * jnp.sin(ai * dt)
    abr_ref[...] = abr
    abi_ref[...] = abi
    den = ar * ar + ai * ai
    f_r = ((abr - 1.0) * ar + abi * ai) / den
    f_i = (abi * ar - (abr - 1.0) * ai) / den
    b_r = bt_ref[0]
    b_i = bt_ref[1]
    bb = (f_r * b_r - f_i * b_i, f_r * b_i + f_i * b_r)
    hn = bb[0].shape[-1] // 2
    nblk = bmat_ref.shape[-2] // bb[0].shape[0]
    for part in range(2):
        for hf in range(2):
            cols = slice(hf * hn, (hf + 1) * hn)
            bmat_ref[part, hf] = _block_diag_rows(bb[part][:, cols], nblk).astype(BF16)
            cmat_ref[hf, part] = _block_diag_rows(ct_ref[part][:, cols], nblk).T.astype(BF16)
    heads = wg_ref.shape[0] // wt_ref.shape[1]
    dl = wg_ref.shape[0]
    for gate in range(2):
        wg_ref[:, gate * dl:(gate + 1) * dl] = _block_diag_rows(wt_ref[gate], heads).astype(BF16)


def _prep_parts(a_re, a_im, log_dt, b_re, b_im, c_re, c_im, w_rg, w_ig):
    L, G, N = a_re.shape
    J = b_re.shape[-1]
    H, HD, _ = w_rg.shape[1:]
    GN, HN, DL = G * N, G * N // 2, H * HD
    row = lambda v: v.reshape(L, 1, GN)
    ldt = jnp.broadcast_to(log_dt[:, :, None], (L, G, N))
    bt = jnp.stack([b_re, b_im], axis=1).transpose(0, 1, 4, 2, 3).reshape(L, 2, J, GN)
    ct = jnp.stack([c_re, c_im], axis=1).transpose(0, 1, 3, 2, 4).reshape(L, 2, J, GN)
    wt = jnp.stack([w_rg, w_ig], axis=1).transpose(0, 1, 3, 2, 4).reshape(L, 2, HD, DL)

    def per_layer(*block):
        return pl.BlockSpec((None,) + block, lambda l, k: (l,) + (0,) * len(block))

    rspec, jspec = per_layer(1, GN), per_layer(2, J, GN)
    in_specs = [rspec, rspec, rspec, jspec, jspec, per_layer(2, HD, DL)]
    out_specs = [rspec, rspec, per_layer(2, 2, G // 2 * J, HN), per_layer(2, 2, HN, G // 2 * J),
                 per_layer(DL, 2 * DL)]
    out_shapes = [jax.ShapeDtypeStruct((L, 1, GN), F32), jax.ShapeDtypeStruct((L, 1, GN), F32),
                  jax.ShapeDtypeStruct((L, 2, 2, G // 2 * J, HN), BF16),
                  jax.ShapeDtypeStruct((L, 2, 2, HN, G // 2 * J), BF16),
                  jax.ShapeDtypeStruct((L, DL, 2 * DL), BF16)]
    return [row(a_re), row(a_im), row(ldt), bt, ct, wt], in_specs, out_specs, out_shapes


def _flush_slabs(slab_scr, o_ref, nb, layout):
    n_slab, rows, _ = slab_scr.shape
    for s in range(n_slab):
        lanes = slice(s * LANES, (s + 1) * LANES)
        if layout == "st":
            o_ref[:, lanes] = slab_scr[s]
        else:
            for seq in range(nb):
                o_ref[seq, :, lanes] = slab_scr[s, pl.ds(seq, rows // nb, stride=nb), :]


def _ffn_rows(x_ref, mod_ref, o_ref, slab_scr, wts, nb, final_norm, layout):
    g_ref, w1_ref, w3_ref, w2_ref, gf_ref = wts
    D = x_ref.shape[-1]
    if layout == "st":
        steps = x_ref.shape[1]
        x = x_ref[...].reshape(nb * steps, D)
        inner, (shift, scale, gate) = steps, [mod_ref[k][:, None, :] for k in range(3)]
    else:
        x = x_ref[...]
        inner, (shift, scale, gate) = nb, [mod_ref[k][None] for k in range(3)]
    rows3 = lambda v: v.reshape(v.shape[0] // inner, inner, D)

    h = _norm_modulate(x, inner, g_ref[...], shift, scale).astype(BF16)
    a = jnp.dot(h, w1_ref[...], preferred_element_type=F32)
    b = jnp.dot(h, w3_ref[...], preferred_element_type=F32)
    act = (a * jax.nn.sigmoid(a) * b).astype(BF16)
    f = jnp.dot(act, w2_ref[...], preferred_element_type=F32)
    out = (rows3(x) + (FFN_RES * gate) * rows3(f)).reshape(x.shape)
    if final_norm:
        out = _rmsnorm(out, gf_ref[...])

    if layout == "tt":
        o_ref[...] = out
        return
    for s in range(D // LANES):
        lanes = slice(s * LANES, (s + 1) * LANES)
        if layout == "ts":
            slab_scr[s] = out[:, lanes]
        else:
            for seq in range(nb):
                slab_scr[s, pl.ds(seq, steps, stride=nb), :] = (
                    out[seq * steps:(seq + 1) * steps, lanes])


def _ffn_kernel(*refs, layer, n_cast, nbp, nbs, n_prompt, final_norm, prompt_layout):
    (xp_ref, xs_ref, modp_ref, mods_ref, g_ref, w1_ref, w3_ref, w2_ref, gf_ref), refs = (
        refs[:9], refs[9:])
    cast_in, (op_ref, os_ref), cast_out, slab_scr = (
        refs[:n_cast], refs[n_cast:n_cast + 2], refs[n_cast + 2:2 * n_cast + 2], refs[-1])
    wts = (g_ref.at[layer:layer + 1], w1_ref, w3_ref, w2_ref, gf_ref)
    lagged = prompt_layout != "tt"

    if lagged:
        @pl.when(pl.program_id(0) == 0)
        def _():
            slab_scr[...] = jnp.zeros(slab_scr.shape, F32)

    @pl.when(pl.program_id(0) < n_prompt)
    def _():
        if lagged:
            _flush_slabs(slab_scr, op_ref, nbp, prompt_layout)
        _cast_blocks(cast_in, cast_out)
        _ffn_rows(xp_ref, modp_ref, op_ref, slab_scr, wts, nbp, final_norm, prompt_layout)

    @pl.when(pl.program_id(0) == n_prompt)
    def _():
        if lagged:
            _flush_slabs(slab_scr, op_ref, nbp, prompt_layout)
        _ffn_rows(xs_ref, mods_ref, os_ref, slab_scr, wts, nbs, final_norm, "tt")


def _ffn_call(xp, xs, nbp, nbs, mod, layer, mod_group, g, w1, w3, w2, gf, final_norm,
              prompt_layout, cast_jobs, name):
    D = xp.shape[-1]
    rows_p = xp.size // D
    rows_s = xs.shape[0]
    tm = min(FFN_ROWS, rows_p)
    assert rows_p % tm == 0 and tm % nbp == 0 and rows_s % nbs == 0 and nbs % nbp == 0
    assert D % LANES == 0
    n_prompt = rows_p // tm
    cur = lambda i: jnp.minimum(i, n_prompt - 1)
    prev = lambda i: jnp.maximum(i - 1, 0)
    tile_tm = lambda at: pl.BlockSpec((tm, D), lambda i: (at(i), 0))
    tile_sm = lambda at: pl.BlockSpec((nbp, tm // nbp, D), lambda i: (0, at(i), 0))
    shape_tm = jax.ShapeDtypeStruct((rows_p, D), F32)
    shape_sm = jax.ShapeDtypeStruct((nbp, rows_p // nbp, D), F32)
    n_cast_steps = _cast_steps(cast_jobs, n_prompt)
    cast_in, cast_out, cast_shapes = _cast_specs(
        cast_jobs, n_cast_steps,
        lambda i: cur(i) // (n_prompt // n_cast_steps))
    return pl.pallas_call(
        functools.partial(_ffn_kernel, layer=layer, n_cast=len(cast_jobs), nbp=nbp, nbs=nbs,
                          n_prompt=n_prompt, final_norm=final_norm, prompt_layout=prompt_layout),
        grid=(n_prompt + 1,),
        in_specs=[
            tile_sm(cur) if prompt_layout == "st" else tile_tm(cur),
            _const_spec((rows_s, D), (0, 0)),
            _const_spec((None, 3, nbp, D), (layer, mod_group, nbs // nbp, 0)),
            _const_spec((None, 3, nbs, D), (layer, mod_group, 0, 0)),
            _const_spec(g.shape, (0, 0)), _const_spec(w1.shape, (0, 0)),
            _const_spec(w3.shape, (0, 0)),
            _const_spec(w2.shape, (0, 0)),
            _const_spec((1, D), (0, 0)),
        ] + cast_in,
        out_specs=[{"tt": tile_tm(cur), "st": tile_tm(prev), "ts": tile_sm(prev)}[prompt_layout],
                   pl.BlockSpec((rows_s, D), lambda i: (0, 0))] + cast_out,
        out_shape=[shape_sm if prompt_layout == "ts" else shape_tm,
                   jax.ShapeDtypeStruct((rows_s, D), F32)] + cast_shapes,
        scratch_shapes=[pltpu.VMEM((D // LANES, tm, LANES), F32)],
        compiler_params=_params(("arbitrary",)),
        name=name,
    )(xp, xs, mod, mod, g, w1, w3, w2, gf, *[w for w, _ in cast_jobs])


def _softplus(v):
    return jnp.maximum(v, 0.0) + jnp.log1p(jnp.exp(-jnp.abs(v)))


def _mixer_kernel(*refs, layer, n_init, n_prev, n_cast, nb):
    (x_ref, mod_ref), refs = refs[:2], refs[2:]
    init_refs, refs = refs[:n_init], refs[n_init + n_prev:]
    (gn_ref, win_ref, convw_ref, convb_ref, wg_ref, brg_ref, big_ref, lam_ref,
     abr_ref, abi_ref, bmat_ref, cmat_ref, d_ref, wglu_ref, bglu_ref, wout_ref), refs = (
        refs[:16], refs[16:])
    cast_in, refs = refs[:n_cast], refs[n_cast:]
    (xo_ref, *state_refs), refs = refs[:5], refs[5:]
    convo_ref, ho_ref, sreo_ref, simo_ref = state_refs
    cast_out, (xp_scr, a_scr, b_scr, s_scr) = refs[:n_cast], refs[n_cast:]
    vec = lambda ref: ref[layer:layer + 1, :]
    rows, D = x_ref.shape
    tt = rows // nb
    DL = ho_ref.shape[-1]
    HS = d_ref.shape[-1] // 2
    HN = sreo_ref.shape[-1] // 2

    @pl.when(pl.program_id(0) == 0)
    def _():
        for k, st_ref in enumerate(state_refs):
            st_ref[...] = init_refs[k][...] if init_refs else jnp.zeros(st_ref.shape, F32)

    _cast_blocks(cast_in, cast_out)
    x = x_ref[...]
    h = _norm_modulate(x, nb, vec(gn_ref), mod_ref[0][None], mod_ref[1][None])
    z = jnp.dot(h.astype(BF16), win_ref[...], preferred_element_type=F32)
    xb = z[:, :DL]
    yb = z[:, DL:2 * DL]
    u = z[:, 2 * DL:]

    halo = (CONV_W - 1) * nb
    xp_scr[0:halo, :] = convo_ref[...]
    xp_scr[halo:halo + rows, :] = xb
    cw = convw_ref[...]
    xc = vec(convb_ref) + sum(xp_scr[k * nb:k * nb + rows, :] * cw[k:k + 1, :]
                              for k in range(CONV_W))
    convo_ref[...] = xp_scr[rows:rows + halo, :]

    gpre = jnp.dot(xc.astype(BF16), wg_ref[...], preferred_element_type=F32)
    r = jax.nn.sigmoid(gpre[:, :DL] + vec(brg_ref))
    ig = jax.nn.sigmoid(gpre[:, DL:] + vec(big_ref))
    log_a = -C_GATE * r * _softplus(-vec(lam_ref))
    a = jnp.exp(log_a)
    mult = _sqrt_nonneg(-jnp.tanh(log_a) * (a * a + 1.0))
    a_scr[...] = a
    b_scr[...] = mult * ig * xc

    for r0 in range(0, nb, SCAN_ROWS):
        hcur = ho_ref[r0:r0 + SCAN_ROWS, :]
        for t in range(tt):
            sl = slice(t * nb + r0, t * nb + r0 + SCAN_ROWS)
            hcur = a_scr[sl, :] * hcur + b_scr[sl, :]
            b_scr[sl, :] = hcur
        ho_ref[r0:r0 + SCAN_ROWS, :] = hcur
    y_lru = jax.nn.gelu(yb) * b_scr[...]

    ub = u.astype(BF16)
    for hf in range(2):
        for part in range(2):
            s_scr[hf, :, part * HN:(part + 1) * HN] = jnp.dot(
                ub[:, hf * HS:(hf + 1) * HS], bmat_ref[part, hf], preferred_element_type=F32)

    for hf in range(2):
        for c0 in range(0, HN, SCAN_LANES):
            st_l = slice(hf * HN + c0, hf * HN + c0 + SCAN_LANES)
            re_l = slice(c0, c0 + SCAN_LANES)
            im_l = slice(HN + c0, HN + c0 + SCAN_LANES)
            a_r = jnp.broadcast_to(abr_ref[:, st_l], (SCAN_ROWS, SCAN_LANES))
            a_i = jnp.broadcast_to(abi_ref[:, st_l], (SCAN_ROWS, SCAN_LANES))
            for r0 in range(0, nb, SCAN_ROWS):
                s_re = sreo_ref[r0:r0 + SCAN_ROWS, st_l]
                s_im = simo_ref[r0:r0 + SCAN_ROWS, st_l]
                for t in range(tt):
                    sl = slice(t * nb + r0, t * nb + r0 + SCAN_ROWS)
                    n_re = a_r * s_re - a_i * s_im + s_scr[hf, sl, re_l]
                    n_im = a_r * s_im + a_i * s_re + s_scr[hf, sl, im_l]
                    s_re, s_im = n_re, n_im
                    s_scr[hf, sl, re_l] = s_re
                    s_scr[hf, sl, im_l] = s_im
                sreo_ref[r0:r0 + SCAN_ROWS, st_l] = s_re
                simo_ref[r0:r0 + SCAN_ROWS, st_l] = s_im

    ys = []
    for hf in range(2):
        sb = s_scr[hf].astype(BF16)
        ys.append(jnp.dot(sb[:, :HN], cmat_ref[hf, 0], preferred_element_type=F32)
                  - jnp.dot(sb[:, HN:], cmat_ref[hf, 1], preferred_element_type=F32))
    ys = jnp.concatenate(ys, axis=-1) + vec(d_ref) * u
    g = jax.nn.gelu(ys)
    y_s5 = g * jax.nn.sigmoid(
        jnp.dot(g.astype(BF16), wglu_ref[...], preferred_element_type=F32) + vec(bglu_ref))

    ycat = jnp.concatenate([y_lru, y_s5], axis=-1).astype(BF16)
    out = jnp.dot(ycat, wout_ref[...], preferred_element_type=F32)
    xo_ref[...] = (_rows3(x, nb) + mod_ref[2][None] * _rows3(out, nb)).reshape(rows, D)


def _mixer_call(x, mod, layer, mod_rowblk, nb, init_states, prev_states, state_shapes, wts,
                cast_jobs, name):
    rows, D = x.shape
    tm = min(MIXER_ROWS, rows)
    assert rows % tm == 0 and tm % nb == 0 and tm // nb >= CONV_W - 1
    DL, GN = state_shapes[1][-1], state_shapes[2][-1]
    halo = (CONV_W - 1) * nb

    def wt_spec(w, kind):
        return _layer_spec(w, layer) if kind == "stacked" else _const_spec(w.shape, (0,) * w.ndim)

    n_lead = 2 + len(init_states)
    state_out_specs = [pl.BlockSpec((None,) + s[1:], lambda i: (layer, 0, 0))
                       for s in state_shapes]
    grid = rows // tm
    n_cast_steps = _cast_steps(cast_jobs, grid)
    cast_in, cast_out, cast_shapes = _cast_specs(cast_jobs, n_cast_steps,
                                                 lambda i: i // (grid // n_cast_steps))
    outs = pl.pallas_call(
        functools.partial(_mixer_kernel, layer=layer, n_init=len(init_states),
                          n_prev=len(prev_states), n_cast=len(cast_jobs), nb=nb),
        grid=(grid,),
        in_specs=[pl.BlockSpec((tm, D), lambda i: (i, 0)),
                  _const_spec((None, 3, nb, D), (layer, 1, mod_rowblk, 0))]
                 + [_layer_spec(s, layer) for s in init_states]
                 + [pl.BlockSpec(memory_space=pl.ANY)] * len(prev_states)
                 + [wt_spec(w, kind) for w, kind in wts] + cast_in,
        out_specs=[pl.BlockSpec((tm, D), lambda i: (i, 0))] + state_out_specs + cast_out,
        out_shape=[jax.ShapeDtypeStruct((rows, D), F32)]
                  + [jax.ShapeDtypeStruct(s, F32) for s in state_shapes] + cast_shapes,
        input_output_aliases={n_lead + k: 1 + k for k in range(len(prev_states))},
        scratch_shapes=[pltpu.VMEM((tm + halo, DL), F32),
                        pltpu.VMEM((tm, DL), F32),
                        pltpu.VMEM((tm, DL), F32),
                        pltpu.VMEM((2, tm, GN), F32)],
        compiler_params=_params(("arbitrary",)),
        name=name,
    )(x, mod, *init_states, *prev_states, *[w for w, _ in wts], *[w for w, _ in cast_jobs])
    n_st = len(state_shapes)
    return outs[0], tuple(outs[1:1 + n_st]), tuple(outs[1 + n_st:])


def kernel(x_prompt, x_sample, c_prompt, c_sample, state_lru_conv, state_lru_h, state_s5_re, state_s5_im, w_ada, b_ada, norm_ffn1, w1_ffn1, w3_ffn1, w2_ffn1, norm_mix, w_in, conv_w, conv_b, w_rg, b_rg, w_ig, b_ig, lru_lambda, s5_a_re, s5_a_im, s5_log_dt, s5_b_re, s5_b_im, s5_c_re, s5_c_im, s5_d, w_glu, b_glu, w_out, norm_ffn2, w1_ffn2, w3_ffn2, w2_ffn2, norm_final):
    L, D, _ = w_ada.shape
    Bp, Tp, _ = x_prompt.shape
    Bs, Ts, _ = x_sample.shape
    _, G, N = s5_a_re.shape
    DL = lru_lambda.shape[-1]
    GN = G * N
    halo = CONV_W - 1
    assert Bs % Bp == 0 and Bp % SCAN_ROWS == 0 and G % 2 == 0

    mod, abr, abi, bmat, cmat, wgate = _ada_prep_call(
        jnp.concatenate([c_sample, c_prompt], axis=0), w_ada, b_ada,
        _prep_parts(s5_a_re, s5_a_im, s5_log_dt, s5_b_re, s5_b_im, s5_c_re, s5_c_im, w_rg, w_ig))

    gfin = norm_final.reshape(1, D)
    ffn1_f32 = (w1_ffn1, w3_ffn1, w2_ffn1)
    ffn2_f32 = (w1_ffn2, w3_ffn2, w2_ffn2)
    mix_f32 = (w_in, w_glu, w_out)

    def mixer_wts(w_in_b, w_glu_b, w_out_b):
        return [(norm_mix, "rows"), (w_in_b, "layer"), (conv_w, "stacked"), (conv_b, "rows"),
                (wgate, "stacked"), (b_rg, "rows"), (b_ig, "rows"), (lru_lambda, "rows"),
                (abr, "stacked"), (abi, "stacked"), (bmat, "stacked"), (cmat, "stacked"),
                (s5_d, "rows"), (w_glu_b, "layer"), (b_glu, "rows"), (w_out_b, "layer")]

    def time_major(x):
        return jnp.swapaxes(x, 0, 1).reshape(x.shape[0] * x.shape[1], D)

    def state_shapes(nb):
        return ((L, halo * nb, DL), (L, nb, DL), (L, nb, GN), (L, nb, GN))

    init_s = (jnp.swapaxes(state_lru_conv, 1, 2).reshape(L, halo * Bs, DL), state_lru_h,
              state_s5_re.reshape(L, Bs, GN), state_s5_im.reshape(L, Bs, GN))
    st_p = tuple(jnp.zeros(s, F32) for s in state_shapes(Bp))
    st_s = tuple(jnp.zeros(s, F32) for s in state_shapes(Bs))
    xp, xs = x_prompt, time_major(x_sample)
    ffn1_b = tuple(w[0].astype(BF16) for w in ffn1_f32)
    for l in range(L):
        xp, xs, *mix_b = _ffn_call(xp, xs, Bp, Bs, mod, l, 0, norm_ffn1, *ffn1_b, gfin,
                                   False, "st" if l == 0 else "tt", [(w, l) for w in mix_f32],
                                   f"ffn1_{l}")
        jobs = [(w, l) for w in ffn2_f32] + [(w, l + 1) for w in ffn1_f32 if l + 1 < L]
        xp, st_p, cast = _mixer_call(xp, mod, l, Bs // Bp, Bp, (), st_p, state_shapes(Bp),
                                     mixer_wts(*mix_b), jobs, f"mixer_p{l}")
        ffn2_b, ffn1_b = cast[:3], cast[3:]
        xs, st_s, _ = _mixer_call(xs, mod, l, 0, Bs, init_s, st_s, state_shapes(Bs),
                                  mixer_wts(*mix_b), [], f"mixer_s{l}")
        xp, xs = _ffn_call(xp, xs, Bp, Bs, mod, l, 2, norm_ffn2, *ffn2_b, gfin,
                           l == L - 1, "ts" if l == L - 1 else "tt", [], f"ffn2_{l}")

    def finish(y, st, nb):
        conv, hh, sre, sim = st
        return (y, jnp.swapaxes(conv.reshape(L, halo, nb, DL), 1, 2), hh,
                sre.reshape(L, nb, G, N), sim.reshape(L, nb, G, N))

    p_out = finish(xp, st_p, Bp)
    s_out = finish(jnp.swapaxes(xs.reshape(Ts, Bs, D), 0, 1), st_s, Bs)
    return (p_out[0], s_out[0]) + p_out[1:] + s_out[1:]
```
